```python
import jax, jax.numpy as jnp
from jax import lax
import numpy as np

D_MODEL = 1024
BATCH = 16
SEQ = 4096
DEPTH = 4

N_META = 16
EPS = 1e-6
CONV_WIDTH = D_MODEL // 2
CONV_GROUPS = 8
SHORT_CONV_K = 3
LRU_WIDTH = D_MODEL // 2
LRU_HEADS = 8
LRU_HEAD_DIM = LRU_WIDTH // LRU_HEADS
LRU_CONV_K = 4
LRU_C = 8.0
EVEN_IN = 3 * CONV_WIDTH + 2 * LRU_WIDTH
EVEN_MIX = CONV_WIDTH + LRU_WIDTH
MLA_HEADS = 16
QK_NOPE = 64
QK_ROPE = 32
QK_HEAD = QK_NOPE + QK_ROPE
V_HEAD = 64
Q_LORA = 384
KV_LORA = 256
ODD_IN = Q_LORA + KV_LORA + QK_ROPE
ROPE_BASE = 10000.0
ATTN_BLOCK = 128
D_FF = 2816
FFN_CONV_K = 3
N_EVEN = (DEPTH + 1) // 2
N_ODD = DEPTH // 2

kernel_name = "hybrid_conv_rglru_mla_convffn"


def rms_norm(x, g):
    xf = x.astype(jnp.float32)
    y = xf * lax.rsqrt(jnp.mean(xf * xf, axis=-1, keepdims=True) + EPS)
    return (y * g.astype(jnp.float32)).astype(x.dtype)


def causal_dwconv(x, w):
    k_width = w.shape[0]
    t_len = x.shape[1]
    xp = jnp.pad(x, ((0, 0), (k_width - 1, 0), (0, 0)))
    y = xp[:, 0:t_len] * w[0]
    for k in range(1, k_width):
        y = y + xp[:, k:k + t_len] * w[k]
    return y


def rope_tables(t_len):
    pos = jnp.arange(t_len, dtype=jnp.float32)
    inv_freq = ROPE_BASE ** (-jnp.arange(0, QK_ROPE, 2, dtype=jnp.float32) / QK_ROPE)
    ang = pos[:, None] * inv_freq[None, :]
    return jnp.cos(ang), jnp.sin(ang)


def apply_rope(x, cos, sin):
    xf = x.astype(jnp.float32)
    x1, x2 = jnp.split(xf, 2, axis=-1)
    out = jnp.concatenate([x1 * cos - x2 * sin, x2 * cos + x1 * sin], axis=-1)
    return out.astype(x.dtype)


def rg_lru(xc, r_w, r_b, i_w, i_b, lam):
    b, t, _ = xc.shape
    xh = xc.reshape(b, t, LRU_HEADS, LRU_HEAD_DIM)
    r = jax.nn.sigmoid(jnp.einsum('bthi,hij->bthj', xh, r_w).reshape(b, t, LRU_WIDTH) + r_b)
    i = jax.nn.sigmoid(jnp.einsum('bthi,hij->bthj', xh, i_w).reshape(b, t, LRU_WIDTH) + i_b)
    log_a = -LRU_C * r.astype(jnp.float32) * jax.nn.softplus(-lam.astype(jnp.float32))
    a = jnp.exp(log_a)
    mult = jnp.sqrt(-jnp.expm1(2.0 * log_a))
    u = mult * (i * xc).astype(jnp.float32)

    def combine(left, right):
        a1, b1 = left
        a2, b2 = right
        return a1 * a2, a2 * b1 + b2

    _, h = lax.associative_scan(combine, (a, u), axis=1)
    return h.astype(xc.dtype)


def even_layer(x, norm, w_in, conv_a, conv_b, conv_b_bias, r_w, r_b, i_w, i_b, lam, w_out):
    h = rms_norm(x, norm)
    u = h @ w_in
    gb, gc, xa, xb, gate = jnp.split(
        u, [CONV_WIDTH, 2 * CONV_WIDTH, 3 * CONV_WIDTH, 3 * CONV_WIDTH + LRU_WIDTH], axis=-1)
    y_a = gb * causal_dwconv(gc * xa, conv_a)
    xc = causal_dwconv(xb, conv_b) + conv_b_bias
    y_b = jax.nn.gelu(gate) * rg_lru(xc, r_w, r_b, i_w, i_b, lam)
    return x + jnp.concatenate([y_a, y_b], axis=-1) @ w_out


def causal_block_attention(q, k, v):
    b, t, nh, dq = q.shape
    nb = -(-t // ATTN_BLOCK)
    tp = nb * ATTN_BLOCK
    pad = ((0, 0), (0, tp - t), (0, 0), (0, 0))
    q, k, v = jnp.pad(q, pad), jnp.pad(k, pad), jnp.pad(v, pad)
    qb = q.reshape(b, nb, ATTN_BLOCK, nh, dq).transpose(1, 0, 2, 3, 4)
    key_pos = jnp.arange(tp)
    scale = QK_HEAD ** -0.5
    neg = jnp.finfo(jnp.float32).min

    def one_block(args):
        q_blk, blk = args
        s = jnp.einsum('bqhd,bkhd->bhqk', q_blk, k).astype(jnp.float32) * scale
        q_pos = blk * ATTN_BLOCK + jnp.arange(ATTN_BLOCK)
        mask = key_pos[None, :] <= q_pos[:, None]
        s = jnp.where(mask[None, None], s, neg)
        p = jax.nn.softmax(s, axis=-1).astype(v.dtype)
        return jnp.einsum('bhqk,bkhd->bqhd', p, v)

    out = lax.map(one_block, (qb, jnp.arange(nb)))
    out = out.transpose(1, 0, 2, 3, 4).reshape(b, tp, nh, V_HEAD)
    return out[:, :t]


def odd_layer(x, cos, sin, norm, w_in, q_norm, kv_norm, w_uq, w_ukv, w_out):
    b, t, _ = x.shape
    h = rms_norm(x, norm)
    u = h @ w_in
    cq, ckv, k_r = jnp.split(u, [Q_LORA, Q_LORA + KV_LORA], axis=-1)
    q = (rms_norm(cq, q_norm) @ w_uq).reshape(b, t, MLA_HEADS, QK_HEAD)
    q_nope, q_rope = jnp.split(q, [QK_NOPE], axis=-1)
    q_rope = apply_rope(q_rope, cos[:, None, :], sin[:, None, :])
    kv = (rms_norm(ckv, kv_norm) @ w_ukv).reshape(b, t, MLA_HEADS, QK_NOPE + V_HEAD)
    k_nope, v = jnp.split(kv, [QK_NOPE], axis=-1)
    k_rope = apply_rope(k_r, cos, sin)
    k_rope = jnp.broadcast_to(k_rope[:, :, None, :], (b, t, MLA_HEADS, QK_ROPE))
    qf = jnp.concatenate([q_nope, q_rope], axis=-1)
    kf = jnp.concatenate([k_nope, k_rope], axis=-1)
    o = causal_block_attention(qf, kf, v).reshape(b, t, MLA_HEADS * V_HEAD)
    return x + o @ w_out


def ffn_layer(x, norm, w_up, conv_w, conv_b, w_down):
    h = rms_norm(x, norm)
    u = causal_dwconv(h @ w_up, conv_w) + conv_b
    a, g = jnp.split(u, 2, axis=-1)
    return x + (jax.nn.silu(a) * g) @ w_down


def setup_inputs(seed: int = 0) -> dict:
    key = jax.random.key(seed)
    ks = iter(jax.random.split(key, 40))

    def nrm(shape, scale):
        return jax.random.normal(next(ks), shape, jnp.float32) * scale

    def gain(shape):
        return 1.0 + nrm(shape, 0.01)

    u = jax.random.uniform(next(ks), (N_EVEN, LRU_WIDTH), jnp.float32, 0.9, 0.999)
    a_base = u ** (1.0 / LRU_C)
    lam = jnp.log(a_base) - jnp.log1p(-a_base)
    return {
        "x": nrm((BATCH, SEQ, D_MODEL), 1.0),
        "meta_tokens": nrm((N_META, D_MODEL), 1.0),
        "ev_norm": gain((N_EVEN, D_MODEL)),
        "ev_w_in": nrm((N_EVEN, D_MODEL, EVEN_IN), D_MODEL ** -0.5),
        "ev_conv_a": nrm((N_EVEN, SHORT_CONV_K, CONV_WIDTH), SHORT_CONV_K ** -0.5),
        "ev_conv_b": nrm((N_EVEN, LRU_CONV_K, LRU_WIDTH), LRU_CONV_K ** -0.5),
        "ev_conv_b_bias": nrm((N_EVEN, LRU_WIDTH), 0.02),
        "ev_gate_r_w": nrm((N_EVEN, LRU_HEADS, LRU_HEAD_DIM, LRU_HEAD_DIM), LRU_HEAD_DIM ** -0.5),
        "ev_gate_r_b": nrm((N_EVEN, LRU_WIDTH), 0.02),
        "ev_gate_i_w": nrm((N_EVEN, LRU_HEADS, LRU_HEAD_DIM, LRU_HEAD_DIM), LRU_HEAD_DIM ** -0.5),
        "ev_gate_i_b": nrm((N_EVEN, LRU_WIDTH), 0.02),
        "ev_lru_lambda": lam,
        "ev_w_out": nrm((N_EVEN, EVEN_MIX, D_MODEL), EVEN_MIX ** -0.5),
        "od_norm": gain((N_ODD, D_MODEL)),
        "od_w_in": nrm((N_ODD, D_MODEL, ODD_IN), D_MODEL ** -0.5),
        "od_q_norm": gain((N_ODD, Q_LORA)),
        "od_kv_norm": gain((N_ODD, KV_LORA)),
        "od_w_uq": nrm((N_ODD, Q_LORA, MLA_HEADS * QK_HEAD), Q_LORA ** -0.5),
        "od_w_ukv": nrm((N_ODD, KV_LORA, MLA_HEADS * (QK_NOPE + V_HEAD)), KV_LORA ** -0.5),
        "od_w_out": nrm((N_ODD, MLA_HEADS * V_HEAD, D_MODEL), (MLA_HEADS * V_HEAD) ** -0.5),
        "ffn_norm": gain((DEPTH, D_MODEL)),
        "ffn_w_up": nrm((DEPTH, D_MODEL, 2 * D_FF), D_MODEL ** -0.5),
        "ffn_conv_w": nrm((DEPTH, FFN_CONV_K, 2 * D_FF), FFN_CONV_K ** -0.5),
        "ffn_conv_b": nrm((DEPTH, 2 * D_FF), 0.02),
        "ffn_w_down": nrm((DEPTH, D_FF, D_MODEL), D_FF ** -0.5),
        "final_norm": gain((D_MODEL,)),
    }


def reference(x, meta_tokens, ev_norm, ev_w_in, ev_conv_a, ev_conv_b, ev_conv_b_bias,
              ev_gate_r_w, ev_gate_r_b, ev_gate_i_w, ev_gate_i_b, ev_lru_lambda, ev_w_out,
              od_norm, od_w_in, od_q_norm, od_kv_norm, od_w_uq, od_w_ukv, od_w_out,
              ffn_norm, ffn_w_up, ffn_conv_w, ffn_conv_b, ffn_w_down, final_norm):
    b = x.shape[0]
    meta = jnp.broadcast_to(meta_tokens[None].astype(x.dtype), (b, N_META, D_MODEL))
    h = jnp.concatenate([meta, x], axis=1)
    cos, sin = rope_tables(h.shape[1])
    for layer in range(DEPTH):
        j = layer // 2
        if layer % 2 == 0:
            h = even_layer(h, ev_norm[j], ev_w_in[j], ev_conv_a[j], ev_conv_b[j], ev_conv_b_bias[j],
                           ev_gate_r_w[j], ev_gate_r_b[j], ev_gate_i_w[j], ev_gate_i_b[j],
                           ev_lru_lambda[j], ev_w_out[j])
        else:
            h = odd_layer(h, cos, sin, od_norm[j], od_w_in[j], od_q_norm[j], od_kv_norm[j],
                          od_w_uq[j], od_w_ukv[j], od_w_out[j])
        h = ffn_layer(h, ffn_norm[layer], ffn_w_up[layer], ffn_conv_w[layer], ffn_conv_b[layer],
                      ffn_w_down[layer])
    h = rms_norm(h, final_norm)
    return h[:, N_META:]
```

```python
import functools
import math

import jax
import jax.numpy as jnp
from jax import lax
from jax.experimental import pallas as pl
from jax.experimental.pallas import tpu as pltpu

D_MODEL = 1024
N_META = 16
EPS = 1e-6
CONV_WIDTH = 512
LRU_WIDTH = 512
LRU_HEADS = 8
LRU_HEAD_DIM = 64
LRU_C = 8.0
EVEN_IN = 3 * CONV_WIDTH + 2 * LRU_WIDTH
MLA_HEADS = 16
QK_NOPE = 64
QK_ROPE = 32
QK_HEAD = QK_NOPE + QK_ROPE
V_HEAD = 64
Q_LORA = 384
KV_LORA = 256
ROPE_BASE = 10000.0
D_FF = 2816

LANES = 128
SUBLANES = 8
HEAD_PAD = 128
TILE_T = 256
FF_CHUNK = 256
VMEM_LIMIT = 56 * 1024 * 1024

F32 = jnp.float32
BF16 = jnp.bfloat16


def _rms(x, g):
    ms = jnp.mean(x * x, axis=-1, keepdims=True)
    return x * lax.rsqrt(ms + EPS) * g


def _sigmoid(x):
    return 1.0 / (1.0 + jnp.exp(-x))


def _gelu_tanh(x):
    c = math.sqrt(2.0 / math.pi)
    return x * (0.5 * (1.0 + jnp.tanh(c * (x + 0.044715 * (x * x * x)))))


def _dot(a, b):
    return jnp.dot(a, b, preferred_element_type=F32)


def _dot_nt(a, b):
    return lax.dot_general(a, b, (((1,), (1,)), ((), ())), preferred_element_type=F32)


def _shifted_conv(buf, tile, taps, width):
    k = taps.shape[0]
    acc = buf[pl.ds(SUBLANES - (k - 1), tile), :] * taps[0:1, :]
    for j in range(1, k):
        acc = acc + buf[pl.ds(SUBLANES - (k - 1) + j, tile), :] * taps[j:j + 1, :]
    return acc


def _even_kernel(x_ref, g_ref, win_ref, ca_ref, cb_ref, cbb_ref, rw_ref, rb_ref,
                 iw_ref, ib_ref, lam_ref, wout_ref, o_ref, zbuf, xbbuf, hstate, hbuf):
    tile = x_ref.shape[1]
    t = pl.program_id(1)

    @pl.when(t == 0)
    def _():
        zbuf[0:SUBLANES, :] = jnp.zeros((SUBLANES, CONV_WIDTH), F32)
        xbbuf[0:SUBLANES, :] = jnp.zeros((SUBLANES, LRU_WIDTH), F32)
        hstate[...] = jnp.zeros_like(hstate)

    x = x_ref[0]
    hn = _rms(x, g_ref[...]).astype(BF16)
    u = _dot(hn, win_ref[...])
    cw = CONV_WIDTH
    gb = u[:, 0:cw]
    gc = u[:, cw:2 * cw]
    xa = u[:, 2 * cw:3 * cw]
    xb = u[:, 3 * cw:3 * cw + LRU_WIDTH]
    gate = u[:, 3 * cw + LRU_WIDTH:]

    zbuf[pl.ds(SUBLANES, tile), :] = gc * xa
    y_a = gb * _shifted_conv(zbuf, tile, ca_ref[...], cw)
    zbuf[0:SUBLANES, :] = zbuf[pl.ds(tile, SUBLANES), :]

    xbbuf[pl.ds(SUBLANES, tile), :] = xb
    xc = _shifted_conv(xbbuf, tile, cb_ref[...], LRU_WIDTH) + cbb_ref[...]
    xbbuf[0:SUBLANES, :] = xbbuf[pl.ds(tile, SUBLANES), :]

    xcb = xc.astype(BF16)
    r = _sigmoid(_dot(xcb, rw_ref[...]) + rb_ref[...])
    i = _sigmoid(_dot(xcb, iw_ref[...]) + ib_ref[...])
    nlam = -lam_ref[...]
    softplus = jnp.maximum(nlam, 0.0) + jnp.log1p(jnp.exp(-jnp.abs(nlam)))
    log_a = -LRU_C * r * softplus
    a = jnp.exp(log_a)
    th = jnp.tanh(log_a)
    mult = jnp.sqrt(-2.0 * th / (1.0 - th))
    b = mult * (i * xc)

    row = lax.broadcasted_iota(jnp.int32, (tile, LRU_WIDTH), 0) & (SUBLANES - 1)
    for d in (1, 2, 4):
        a_sh = pltpu.roll(a, d, axis=0)
        b_sh = pltpu.roll(b, d, axis=0)
        ok = row >= d
        b = jnp.where(ok, a * b_sh + b, b)
        a = jnp.where(ok, a * a_sh, a)
    hprev = hstate[...]
    for gidx in range(tile // SUBLANES):
        sl = slice(gidx * SUBLANES, (gidx + 1) * SUBLANES)
        hg = b[sl, :] + a[sl, :] * hprev
        hbuf[sl, :] = hg
        hprev = hg[SUBLANES - 1:SUBLANES, :]
    hstate[...] = hprev

    y_b = _gelu_tanh(gate) * hbuf[...]
    y = jnp.concatenate([y_a, y_b], axis=-1).astype(BF16)
    o_ref[0] = x + _dot(y, wout_ref[...])


def _const_spec(shape):
    nd = len(shape)
    return pl.BlockSpec(shape, lambda b, t: (0,) * nd)


def _even_layer(h, g, w_in, conv_a, conv_b, conv_b_bias, rw, rb, iw, ib, lam, w_out):
    bsz, tlen, _ = h.shape
    tile = TILE_T
    row_spec = pl.BlockSpec((1, tile, D_MODEL), lambda b, t: (b, t, 0))
    args = (g.reshape(1, D_MODEL), w_in.astype(BF16), conv_a, conv_b,
            conv_b_bias.reshape(1, LRU_WIDTH), rw, rb.reshape(1, LRU_WIDTH), iw,
            ib.reshape(1, LRU_WIDTH), lam.reshape(1, LRU_WIDTH), w_out.astype(BF16))
    return pl.pallas_call(
        _even_kernel,
        grid=(bsz, tlen // tile),
        in_specs=[row_spec] + [_const_spec(a.shape) for a in args],
        out_specs=row_spec,
        out_shape=jax.ShapeDtypeStruct(h.shape, F32),
        scratch_shapes=[
            pltpu.VMEM((tile + SUBLANES, CONV_WIDTH), F32),
            pltpu.VMEM((tile + SUBLANES, LRU_WIDTH), F32),
            pltpu.VMEM((1, LRU_WIDTH), F32),
            pltpu.VMEM((tile, LRU_WIDTH), F32),
        ],
        compiler_params=pltpu.CompilerParams(
            dimension_semantics=("arbitrary", "arbitrary"), vmem_limit_bytes=VMEM_LIMIT),
        name="even_mixer",
    )(h, *args)


def _block_diag(w):
    nh, d, _ = w.shape
    eye = jnp.eye(nh, dtype=w.dtype)
    return jnp.einsum("hij,hg->higj", w, eye).reshape(nh * d, nh * d)


def _ffn_kernel(x_ref, g_ref, wup_ref, cw_ref, cb_ref, wdn_ref, o_ref, ubuf, tail):
    tile = x_ref.shape[1]
    t = pl.program_id(1)

    @pl.when(t == 0)
    def _():
        tail[...] = jnp.zeros_like(tail)

    x = x_ref[0]
    hn = _rms(x, g_ref[...]).astype(BF16)
    o_ref[0] = x
    for c in range(D_FF // FF_CHUNK):
        halves = []
        for part in range(2):
            col = part * D_FF + c * FF_CHUNK
            cs = slice(col, col + FF_CHUNK)
            ubuf[0:SUBLANES, :] = tail[:, cs]
            ubuf[pl.ds(SUBLANES, tile), :] = _dot(hn, wup_ref[:, cs])
            halves.append(_shifted_conv(ubuf, tile, cw_ref[:, cs], FF_CHUNK) + cb_ref[:, cs])
            tail[:, cs] = ubuf[pl.ds(tile, SUBLANES), :]
        a, gte = halves
        act = (a * _sigmoid(a) * gte).astype(BF16)
        o_ref[0] += _dot(act, wdn_ref[c * FF_CHUNK:(c + 1) * FF_CHUNK, :])


def _ffn_layer(h, g, w_up, conv_w, conv_b, w_down):
    bsz, tlen, _ = h.shape
    tile = TILE_T
    row_spec = pl.BlockSpec((1, tile, D_MODEL), lambda b, t: (b, t, 0))
    args = (g.reshape(1, D_MODEL), w_up.astype(BF16), conv_w,
            conv_b.reshape(1, 2 * D_FF), w_down.astype(BF16))
    return pl.pallas_call(
        _ffn_kernel,
        grid=(bsz, tlen // tile),
        in_specs=[row_spec] + [_const_spec(a.shape) for a in args],
        out_specs=row_spec,
        out_shape=jax.ShapeDtypeStruct(h.shape, F32),
        scratch_shapes=[
            pltpu.VMEM((tile + SUBLANES, FF_CHUNK), F32),
            pltpu.VMEM((SUBLANES, 2 * D_FF), F32),
        ],
        compiler_params=pltpu.CompilerParams(
            dimension_semantics=("arbitrary", "arbitrary"), vmem_limit_bytes=VMEM_LIMIT),
        name="conv_ffn",
    )(h, *args)


def _mla_proj_kernel(x_ref, g_ref, win_ref, qn_ref, kvn_ref, wqt_ref, wk_ref, wvt_ref,
                     cosk_ref, sink_ref, cosq_ref, sinq_ref, qt_ref, k_ref, vt_ref):
    x = x_ref[0]
    hn = _rms(x, g_ref[...]).astype(BF16)
    u = _dot(hn, win_ref[...])
    cq = u[:, 0:Q_LORA]
    ckv = u[:, Q_LORA:Q_LORA + KV_LORA]
    kr = u[:, Q_LORA + KV_LORA:Q_LORA + KV_LORA + HEAD_PAD]
    kr_rot = u[:, Q_LORA + KV_LORA + HEAD_PAD:]
    cqn = _rms(cq, qn_ref[...]).astype(BF16)
    ckvn = _rms(ckv, kvn_ref[...]).astype(BF16)

    qt = _dot_nt(wqt_ref[...], cqn)
    kn = _dot(ckvn, wk_ref[...])
    vt = _dot_nt(wvt_ref[...], ckvn)
    k_rope = kr * cosk_ref[...] + kr_rot * sink_ref[...]
    cosq = cosq_ref[...]
    sinq = sinq_ref[...]
    scale = QK_HEAD ** -0.5
    zeros = jnp.zeros((HEAD_PAD - QK_HEAD, qt.shape[1]), F32)
    for h in range(MLA_HEADS):
        qh = qt[h * HEAD_PAD:(h + 1) * HEAD_PAD, :]
        roped = qh[QK_NOPE:QK_HEAD, :] * cosq + qh[QK_HEAD:, :] * sinq
        qt_ref[0, h] = jnp.concatenate(
            [qh[0:QK_NOPE, :] * scale, roped * scale, zeros], axis=0).astype(BF16)
        k_ref[0, h, 0] = (kn[:, h * HEAD_PAD:(h + 1) * HEAD_PAD] + k_rope).astype(BF16)
        vt_ref[0, h, 0] = vt[h * V_HEAD:(h + 1) * V_HEAD, :].astype(BF16)


def _rot_cols(w):
    half = QK_ROPE // 2
    return jnp.concatenate([-w[..., half:], w[..., :half]], axis=-1)


def _mla_proj(h, cos, sin, g, w_in, q_norm, kv_norm, w_uq, w_ukv):
    bsz, tlen, _ = h.shape
    tile = TILE_T
    nt = tlen // tile
    w_kr = w_in[:, Q_LORA + KV_LORA:]
    padl = jnp.zeros((D_MODEL, QK_NOPE), F32)
    padr = jnp.zeros((D_MODEL, HEAD_PAD - QK_HEAD), F32)
    win_ext = jnp.concatenate(
        [w_in[:, :Q_LORA + KV_LORA], padl, w_kr, padr, padl, _rot_cols(w_kr), padr],
        axis=-1).astype(BF16)
    wq = w_uq.reshape(Q_LORA, MLA_HEADS, QK_HEAD)
    wq_ext = jnp.concatenate([wq, _rot_cols(wq[..., QK_NOPE:])], axis=-1)
    wqt = wq_ext.reshape(Q_LORA, MLA_HEADS * HEAD_PAD).T.astype(BF16)
    wkv = w_ukv.reshape(KV_LORA, MLA_HEADS, QK_NOPE + V_HEAD)
    wk = jnp.concatenate(
        [wkv[..., :QK_NOPE], jnp.zeros((KV_LORA, MLA_HEADS, HEAD_PAD - QK_NOPE), F32)],
        axis=-1).reshape(KV_LORA, MLA_HEADS * HEAD_PAD).astype(BF16)
    wvt = wkv[..., QK_NOPE:].reshape(KV_LORA, MLA_HEADS * V_HEAD).T.astype(BF16)
    cos2 = jnp.concatenate([cos, cos], axis=-1)
    sin2 = jnp.concatenate([sin, sin], axis=-1)
    lpad = ((0, 0), (QK_NOPE, HEAD_PAD - QK_HEAD))
    cosk, sink = jnp.pad(cos2, lpad), jnp.pad(sin2, lpad)
    cosq, sinq = cos2.T, sin2.T

    args = (g.reshape(1, D_MODEL), win_ext, q_norm.reshape(1, Q_LORA),
            kv_norm.reshape(1, KV_LORA), wqt, wk, wvt)
    row_spec = pl.BlockSpec((1, tile, D_MODEL), lambda b, t: (b, t, 0))
    in_specs = [row_spec] + [_const_spec(a.shape) for a in args] + [
        pl.BlockSpec((tile, HEAD_PAD), lambda b, t: (t, 0)),
        pl.BlockSpec((tile, HEAD_PAD), lambda b, t: (t, 0)),
        pl.BlockSpec((QK_ROPE, tile), lambda b, t: (0, t)),
        pl.BlockSpec((QK_ROPE, tile), lambda b, t: (0, t)),
    ]
    out_shape = (
        jax.ShapeDtypeStruct((bsz, MLA_HEADS, HEAD_PAD, tlen), BF16),
        jax.ShapeDtypeStruct((bsz, MLA_HEADS, nt, tile, HEAD_PAD), BF16),
        jax.ShapeDtypeStruct((bsz, MLA_HEADS, nt, V_HEAD, tile), BF16),
    )
    out_specs = (
        pl.BlockSpec((1, MLA_HEADS, HEAD_PAD, tile), lambda b, t: (b, 0, 0, t)),
        pl.BlockSpec((1, MLA_HEADS, 1, tile, HEAD_PAD), lambda b, t: (b, 0, t, 0, 0)),
        pl.BlockSpec((1, MLA_HEADS, 1, V_HEAD, tile), lambda b, t: (b, 0, t, 0, 0)),
    )
    return pl.pallas_call(
        _mla_proj_kernel,
        grid=(bsz, nt),
        in_specs=in_specs,
        out_specs=out_specs,
        out_shape=out_shape,
        compiler_params=pltpu.CompilerParams(
            dimension_semantics=("arbitrary", "arbitrary"), vmem_limit_bytes=VMEM_LIMIT),
        name="mla_proj",
    )(h, *args, cosk, sink, cosq, sinq)


HEADS_PER_STEP = 2


def _attn_kernel(qt_ref, k_ref, vt_ref, o_ref):
    tile = qt_ref.shape[3]
    qi = pl.program_id(2)
    neg = jnp.finfo(F32).min
    outs = []
    for hh in range(HEADS_PER_STEP):
        qt = qt_ref[0, hh]

        def step(j, carry, masked):
            m, l, acc = carry
            s = _dot(k_ref[0, hh, j], qt)
            if masked:
                kpos = lax.broadcasted_iota(jnp.int32, s.shape, 0)
                qpos = lax.broadcasted_iota(jnp.int32, s.shape, 1)
                s = jnp.where(kpos <= qpos, s, neg)
            m_new = jnp.maximum(m, jnp.max(s, axis=0, keepdims=True))
            alpha = jnp.exp(m - m_new)
            p = jnp.exp(s - m_new)
            l = alpha * l + jnp.sum(p, axis=0, keepdims=True)
            acc = alpha * acc + _dot(vt_ref[0, hh, j], p.astype(BF16))
            return m_new, l, acc

        init = (jnp.full((1, tile), neg, F32), jnp.zeros((1, tile), F32),
                jnp.zeros((V_HEAD, tile), F32))
        carry = lax.fori_loop(0, qi, functools.partial(step, masked=False), init)
        m, l, acc = step(qi, carry, masked=True)
        outs.append((acc / l).T)
    o_ref[0] = jnp.concatenate(outs, axis=-1).astype(o_ref.dtype)


def _attention(qt, k5, vt5):
    bsz, nh, _, tlen = qt.shape
    nt, tile = k5.shape[2], k5.shape[3]
    hps = HEADS_PER_STEP
    return pl.pallas_call(
        _attn_kernel,
        grid=(bsz, nh // hps, nt),
        in_specs=[
            pl.BlockSpec((1, hps, HEAD_PAD, tile), lambda b, h, q: (b, h, 0, q)),
            pl.BlockSpec((1, hps, nt, tile, HEAD_PAD), lambda b, h, q: (b, h, 0, 0, 0)),
            pl.BlockSpec((1, hps, nt, V_HEAD, tile), lambda b, h, q: (b, h, 0, 0, 0)),
        ],
        out_specs=pl.BlockSpec((1, tile, hps * V_HEAD), lambda b, h, q: (b, q, h)),
        out_shape=jax.ShapeDtypeStruct((bsz, tlen, nh * V_HEAD), BF16),
        compiler_params=pltpu.CompilerParams(
            dimension_semantics=("arbitrary", "arbitrary", "arbitrary"),
            vmem_limit_bytes=VMEM_LIMIT),
        name="mla_attention",
    )(qt, k5, vt5)


def _out_proj_kernel(x_ref, o_ref, w_ref, y_ref):
    y_ref[0] = x_ref[0] + _dot(o_ref[0], w_ref[...])


def _out_proj(h, o, w_out):
    bsz, tlen, _ = h.shape
    tile = TILE_T
    row_spec = pl.BlockSpec((1, tile, D_MODEL), lambda b, t: (b, t, 0))
    w = w_out.astype(BF16)
    return pl.pallas_call(
        _out_proj_kernel,
        grid=(bsz, tlen // tile),
        in_specs=[row_spec, row_spec, _const_spec(w.shape)],
        out_specs=row_spec,
        out_shape=jax.ShapeDtypeStruct(h.shape, F32),
        compiler_params=pltpu.CompilerParams(
            dimension_semantics=("arbitrary", "arbitrary"), vmem_limit_bytes=VMEM_LIMIT),
        name="mla_out_proj",
    )(h, o, w)


def _final_norm_kernel(x_ref, g_ref, y_ref):
    y_ref[0] = _rms(x_ref[0], g_ref[...])


def _final_norm(h, g):
    bsz, tlen, _ = h.shape
    tile = TILE_T
    row_spec = pl.BlockSpec((1, tile, D_MODEL), lambda b, t: (b, t, 0))
    return pl.pallas_call(
        _final_norm_kernel,
        grid=(bsz, tlen // tile),
        in_specs=[row_spec, _const_spec((1, D_MODEL))],
        out_specs=row_spec,
        out_shape=jax.ShapeDtypeStruct(h.shape, F32),
        compiler_params=pltpu.CompilerParams(
            dimension_semantics=("arbitrary", "arbitrary"), vmem_limit_bytes=VMEM_LIMIT),
        name="final_norm",
    )(h, g.reshape(1, D_MODEL))


def _rope_tables(t_len):
    pos = jnp.arange(t_len, dtype=F32)
    inv_freq = ROPE_BASE ** (-jnp.arange(0, QK_ROPE, 2, dtype=F32) / QK_ROPE)
    ang = pos[:, None] * inv_freq[None, :]
    return jnp.cos(ang), jnp.sin(ang)


def kernel(x, meta_tokens, ev_norm, ev_w_in, ev_conv_a, ev_conv_b, ev_conv_b_bias, ev_gate_r_w, ev_gate_r_b, ev_gate_i_w, ev_gate_i_b, ev_lru_lambda, ev_w_out, od_norm, od_w_in, od_q_norm, od_kv_norm, od_w_uq, od_w_ukv, od_w_out, ffn_norm, ffn_w_up, ffn_conv_w, ffn_conv_b, ffn_w_down, final_norm):
    bsz, seq, _ = x.shape
    depth = ffn_norm.shape[0]
    t_real = N_META + seq
    t_pad = -(-t_real // TILE_T) * TILE_T
    meta = jnp.broadcast_to(meta_tokens[None].astype(x.dtype), (bsz, N_META, D_MODEL))
    h = jnp.concatenate(
        [meta, x, jnp.zeros((bsz, t_pad - t_real, D_MODEL), x.dtype)], axis=1)
    cos, sin = _rope_tables(t_pad)
    for layer in range(depth):
        j = layer // 2
        if layer % 2 == 0:
            h = _even_layer(h, ev_norm[j], ev_w_in[j], ev_conv_a[j], ev_conv_b[j],
                            ev_conv_b_bias[j], _block_diag(ev_gate_r_w[j]).astype(BF16),
                            ev_gate_r_b[j], _block_diag(ev_gate_i_w[j]).astype(BF16),
                            ev_gate_i_b[j], ev_lru_lambda[j], ev_w_out[j])
        else:
            qt, k5, vt5 = _mla_proj(h, cos, sin, od_norm[j], od_w_in[j], od_q_norm[j],
                                    od_kv_norm[j], od_w_uq[j], od_w_ukv[j])
            o = _attention(qt, k5, vt5)
            h = _out_proj(h, o, od_w_out[j])
        h = _ffn_layer(h, ffn_norm[layer], ffn_w_up[layer], ffn_conv_w[layer],
                       ffn_conv_b[layer], ffn_w_down[layer])
    h = _final_norm(h, final_norm)
    return h[:, N_META:t_real]
```

```python
import functools
import math

import jax
import jax.numpy as jnp
from jax import lax
from jax.experimental import pallas as pl
from jax.experimental.pallas import tpu as pltpu

D_MODEL = 1024
N_META = 16
EPS = 1e-6
CONV_WIDTH = 512
LRU_WIDTH = 512
LRU_HEADS = 8
LRU_HEAD_DIM = 64
LRU_C = 8.0
EVEN_IN = 3 * CONV_WIDTH + 2 * LRU_WIDTH
MLA_HEADS = 16
QK_NOPE = 64
QK_ROPE = 32
QK_HEAD = QK_NOPE + QK_ROPE
V_HEAD = 64
Q_LORA = 384
KV_LORA = 256
ROPE_BASE = 10000.0
D_FF = 2816

LANES = 128
SUBLANES = 8
HEAD_PAD = 128
V_ROWS = V_HEAD + 16
HEADS_PER_STEP = 8
TILE_T = 256
FF_CHUNK = 256
VMEM_LIMIT = 56 * 1024 * 1024

F32 = jnp.float32
BF16 = jnp.bfloat16


def _rms(x, g):
    ms = jnp.mean(x * x, axis=-1, keepdims=True)
    return x * lax.rsqrt(ms + EPS) * g


def _sigmoid(x):
    return 1.0 / (1.0 + jnp.exp(-x))


def _gelu_tanh(x):
    c = math.sqrt(2.0 / math.pi)
    return x * (0.5 * (1.0 + jnp.tanh(c * (x + 0.044715 * (x * x * x)))))


def _dot(a, b):
    return jnp.dot(a, b, preferred_element_type=F32)


def _dot_nt(a, b):
    return lax.dot_general(a, b, (((1,), (1,)), ((), ())), preferred_element_type=F32)


def _shifted_conv(buf, tile, taps, width):
    k = taps.shape[0]
    acc = buf[pl.ds(SUBLANES - (k - 1), tile), :] * taps[0:1, :]
    for j in range(1, k):
        acc = acc + buf[pl.ds(SUBLANES - (k - 1) + j, tile), :] * taps[j:j + 1, :]
    return acc


def _even_kernel(x_ref, g_ref, win_ref, ca_ref, cb_ref, cbb_ref, rw_ref, rb_ref,
                 iw_ref, ib_ref, lam_ref, wout_ref, o_ref, zbuf, xbbuf, hstate, hbuf):
    tile = x_ref.shape[1]
    t = pl.program_id(1)

    @pl.when(t == 0)
    def _():
        zbuf[0:SUBLANES, :] = jnp.zeros((SUBLANES, CONV_WIDTH), F32)
        xbbuf[0:SUBLANES, :] = jnp.zeros((SUBLANES, LRU_WIDTH), F32)
        hstate[...] = jnp.zeros_like(hstate)

    x = x_ref[0]
    hn = _rms(x, g_ref[...]).astype(BF16)
    u = _dot(hn, win_ref[...])
    cw = CONV_WIDTH
    gb = u[:, 0:cw]
    gc = u[:, cw:2 * cw]
    xa = u[:, 2 * cw:3 * cw]
    xb = u[:, 3 * cw:3 * cw + LRU_WIDTH]
    gate = u[:, 3 * cw + LRU_WIDTH:]

    zbuf[pl.ds(SUBLANES, tile), :] = gc * xa
    y_a = gb * _shifted_conv(zbuf, tile, ca_ref[...], cw)
    zbuf[0:SUBLANES, :] = zbuf[pl.ds(tile, SUBLANES), :]

    xbbuf[pl.ds(SUBLANES, tile), :] = xb
    xc = _shifted_conv(xbbuf, tile, cb_ref[...], LRU_WIDTH) + cbb_ref[...]
    xbbuf[0:SUBLANES, :] = xbbuf[pl.ds(tile, SUBLANES), :]

    xcb = xc.astype(BF16)
    r = _sigmoid(_dot(xcb, rw_ref[...]) + rb_ref[...])
    i = _sigmoid(_dot(xcb, iw_ref[...]) + ib_ref[...])
    nlam = -lam_ref[...]
    softplus = jnp.maximum(nlam, 0.0) + jnp.log1p(jnp.exp(-jnp.abs(nlam)))
    log_a = -LRU_C * r * softplus
    a = jnp.exp(log_a)
    th = jnp.tanh(log_a)
    mult = jnp.sqrt(-2.0 * th / (1.0 - th))
    b = mult * (i * xc)

    row = lax.broadcasted_iota(jnp.int32, (tile, LRU_WIDTH), 0) & (SUBLANES - 1)
    for d in (1, 2, 4):
        a_sh = pltpu.roll(a, d, axis=0)
        b_sh = pltpu.roll(b, d, axis=0)
        ok = row >= d
        b = jnp.where(ok, a * b_sh + b, b)
        a = jnp.where(ok, a * a_sh, a)
    hprev = hstate[...]
    for gidx in range(tile // SUBLANES):
        sl = slice(gidx * SUBLANES, (gidx + 1) * SUBLANES)
        hg = b[sl, :] + a[sl, :] * hprev
        hbuf[sl, :] = hg
        hprev = hg[SUBLANES - 1:SUBLANES, :]
    hstate[...] = hprev

    y_b = _gelu_tanh(gate) * hbuf[...]
    y = jnp.concatenate([y_a, y_b], axis=-1).astype(BF16)
    o_ref[0] = x + _dot(y, wout_ref[...])


def _const_spec(shape):
    nd = len(shape)
    return pl.BlockSpec(shape, lambda b, t: (0,) * nd)


def _even_layer(h, g, w_in, conv_a, conv_b, conv_b_bias, rw, rb, iw, ib, lam, w_out):
    bsz, tlen, _ = h.shape
    tile = TILE_T
    row_spec = pl.BlockSpec((1, tile, D_MODEL), lambda b, t: (b, t, 0))
    args = (g.reshape(1, D_MODEL), w_in.astype(BF16), conv_a, conv_b,
            conv_b_bias.reshape(1, LRU_WIDTH), rw, rb.reshape(1, LRU_WIDTH), iw,
            ib.reshape(1, LRU_WIDTH), lam.reshape(1, LRU_WIDTH), w_out.astype(BF16))
    return pl.pallas_call(
        _even_kernel,
        grid=(bsz, tlen // tile),
        in_specs=[row_spec] + [_const_spec(a.shape) for a in args],
        out_specs=row_spec,
        out_shape=jax.ShapeDtypeStruct(h.shape, F32),
        scratch_shapes=[
            pltpu.VMEM((tile + SUBLANES, CONV_WIDTH), F32),
            pltpu.VMEM((tile + SUBLANES, LRU_WIDTH), F32),
            pltpu.VMEM((1, LRU_WIDTH), F32),
            pltpu.VMEM((tile, LRU_WIDTH), F32),
        ],
        compiler_params=pltpu.CompilerParams(
            dimension_semantics=("arbitrary", "arbitrary"), vmem_limit_bytes=VMEM_LIMIT),
        name="even_mixer",
    )(h, *args)


def _block_diag(w):
    nh, d, _ = w.shape
    eye = jnp.eye(nh, dtype=w.dtype)
    return jnp.einsum("hij,hg->higj", w, eye).reshape(nh * d, nh * d)


def _ffn_kernel(x_ref, g_ref, wup_ref, cw_ref, cb_ref, wdn_ref, o_ref, ubuf, tail):
    tile = x_ref.shape[1]
    t = pl.program_id(1)

    @pl.when(t == 0)
    def _():
        tail[...] = jnp.zeros_like(tail)

    x = x_ref[0]
    hn = _rms(x, g_ref[...]).astype(BF16)
    o_ref[0] = x
    for c in range(D_FF // FF_CHUNK):
        halves = []
        for part in range(2):
            col = part * D_FF + c * FF_CHUNK
            cs = slice(col, col + FF_CHUNK)
            ubuf[0:SUBLANES, :] = tail[:, cs]
            ubuf[pl.ds(SUBLANES, tile), :] = _dot(hn, wup_ref[:, cs])
            halves.append(_shifted_conv(ubuf, tile, cw_ref[:, cs], FF_CHUNK) + cb_ref[:, cs])
            tail[:, cs] = ubuf[pl.ds(tile, SUBLANES), :]
        a, gte = halves
        act = (a * _sigmoid(a) * gte).astype(BF16)
        o_ref[0] += _dot(act, wdn_ref[c * FF_CHUNK:(c + 1) * FF_CHUNK, :])


def _ffn_layer(h, g, w_up, conv_w, conv_b, w_down):
    bsz, tlen, _ = h.shape
    tile = TILE_T
    row_spec = pl.BlockSpec((1, tile, D_MODEL), lambda b, t: (b, t, 0))
    args = (g.reshape(1, D_MODEL), w_up.astype(BF16), conv_w,
            conv_b.reshape(1, 2 * D_FF), w_down.astype(BF16))
    return pl.pallas_call(
        _ffn_kernel,
        grid=(bsz, tlen // tile),
        in_specs=[row_spec] + [_const_spec(a.shape) for a in args],
        out_specs=row_spec,
        out_shape=jax.ShapeDtypeStruct(h.shape, F32),
        scratch_shapes=[
            pltpu.VMEM((tile + SUBLANES, FF_CHUNK), F32),
            pltpu.VMEM((SUBLANES, 2 * D_FF), F32),
        ],
        compiler_params=pltpu.CompilerParams(
            dimension_semantics=("arbitrary", "arbitrary"), vmem_limit_bytes=VMEM_LIMIT),
        name="conv_ffn",
    )(h, *args)


def _mla_proj_kernel(x_ref, g_ref, win_ref, qn_ref, kvn_ref, wqt_ref, wk_ref, wvt_ref,
                     cosk_ref, sink_ref, cosq_ref, sinq_ref, qt_ref, k_ref, vt_ref):
    x = x_ref[0]
    hn = _rms(x, g_ref[...]).astype(BF16)
    u = _dot(hn, win_ref[...])
    cq = u[:, 0:Q_LORA]
    ckv = u[:, Q_LORA:Q_LORA + KV_LORA]
    kr = u[:, Q_LORA + KV_LORA:Q_LORA + KV_LORA + HEAD_PAD]
    kr_rot = u[:, Q_LORA + KV_LORA + HEAD_PAD:]
    cqn = _rms(cq, qn_ref[...]).astype(BF16)
    ckvn = _rms(ckv, kvn_ref[...]).astype(BF16)

    qt = _dot_nt(wqt_ref[...], cqn)
    kn = _dot(ckvn, wk_ref[...])
    vt = _dot_nt(wvt_ref[...], ckvn)
    k_rope = kr * cosk_ref[...] + kr_rot * sink_ref[...]
    cosq = cosq_ref[...]
    sinq = sinq_ref[...]
    scale = QK_HEAD ** -0.5 * math.log2(math.e)
    zeros = jnp.zeros((HEAD_PAD - QK_HEAD, qt.shape[1]), F32)
    ones = jnp.ones((V_ROWS - V_HEAD, qt.shape[1]), F32)
    for h in range(MLA_HEADS):
        qh = qt[h * HEAD_PAD:(h + 1) * HEAD_PAD, :]
        roped = qh[QK_NOPE:QK_HEAD, :] * cosq + qh[QK_HEAD:, :] * sinq
        qt_ref[0, h] = jnp.concatenate(
            [qh[0:QK_NOPE, :] * scale, roped * scale, zeros], axis=0).astype(BF16)
        k_ref[0, h, 0] = (kn[:, h * HEAD_PAD:(h + 1) * HEAD_PAD] + k_rope).astype(BF16)
        vt_ref[0, h, 0] = jnp.concatenate(
            [vt[h * V_HEAD:(h + 1) * V_HEAD, :], ones], axis=0).astype(BF16)


def _rot_cols(w):
    half = QK_ROPE // 2
    return jnp.concatenate([-w[..., half:], w[..., :half]], axis=-1)


def _mla_proj(h, cos, sin, g, w_in, q_norm, kv_norm, w_uq, w_ukv):
    bsz, tlen, _ = h.shape
    tile = TILE_T
    nt = tlen // tile
    w_kr = w_in[:, Q_LORA + KV_LORA:]
    padl = jnp.zeros((D_MODEL, QK_NOPE), F32)
    padr = jnp.zeros((D_MODEL, HEAD_PAD - QK_HEAD), F32)
    win_ext = jnp.concatenate(
        [w_in[:, :Q_LORA + KV_LORA], padl, w_kr, padr, padl, _rot_cols(w_kr), padr],
        axis=-1).astype(BF16)
    wq = w_uq.reshape(Q_LORA, MLA_HEADS, QK_HEAD)
    wq_ext = jnp.concatenate([wq, _rot_cols(wq[..., QK_NOPE:])], axis=-1)
    wqt = wq_ext.reshape(Q_LORA, MLA_HEADS * HEAD_PAD).T.astype(BF16)
    wkv = w_ukv.reshape(KV_LORA, MLA_HEADS, QK_NOPE + V_HEAD)
    wk = jnp.concatenate(
        [wkv[..., :QK_NOPE], jnp.zeros((KV_LORA, MLA_HEADS, HEAD_PAD - QK_NOPE), F32)],
        axis=-1).reshape(KV_LORA, MLA_HEADS * HEAD_PAD).astype(BF16)
    wvt = wkv[..., QK_NOPE:].reshape(KV_LORA, MLA_HEADS * V_HEAD).T.astype(BF16)
    cos2 = jnp.concatenate([cos, cos], axis=-1)
    sin2 = jnp.concatenate([sin, sin], axis=-1)
    lpad = ((0, 0), (QK_NOPE, HEAD_PAD - QK_HEAD))
    cosk, sink = jnp.pad(cos2, lpad), jnp.pad(sin2, lpad)
    cosq, sinq = cos2.T, sin2.T

    args = (g.reshape(1, D_MODEL), win_ext, q_norm.reshape(1, Q_LORA),
            kv_norm.reshape(1, KV_LORA), wqt, wk, wvt)
    row_spec = pl.BlockSpec((1, tile, D_MODEL), lambda b, t: (b, t, 0))
    in_specs = [row_spec] + [_const_spec(a.shape) for a in args] + [
        pl.BlockSpec((tile, HEAD_PAD), lambda b, t: (t, 0)),
        pl.BlockSpec((tile, HEAD_PAD), lambda b, t: (t, 0)),
        pl.BlockSpec((QK_ROPE, tile), lambda b, t: (0, t)),
        pl.BlockSpec((QK_ROPE, tile), lambda b, t: (0, t)),
    ]
    out_shape = (
        jax.ShapeDtypeStruct((bsz, MLA_HEADS, HEAD_PAD, tlen), BF16),
        jax.ShapeDtypeStruct((bsz, MLA_HEADS, nt, tile, HEAD_PAD), BF16),
        jax.ShapeDtypeStruct((bsz, MLA_HEADS, nt, V_ROWS, tile), BF16),
    )
    out_specs = (
        pl.BlockSpec((1, MLA_HEADS, HEAD_PAD, tile), lambda b, t: (b, 0, 0, t)),
        pl.BlockSpec((1, MLA_HEADS, 1, tile, HEAD_PAD), lambda b, t: (b, 0, t, 0, 0)),
        pl.BlockSpec((1, MLA_HEADS, 1, V_ROWS, tile), lambda b, t: (b, 0, t, 0, 0)),
    )
    return pl.pallas_call(
        _mla_proj_kernel,
        grid=(bsz, nt),
        in_specs=in_specs,
        out_specs=out_specs,
        out_shape=out_shape,
        compiler_params=pltpu.CompilerParams(
            dimension_semantics=("arbitrary", "arbitrary"), vmem_limit_bytes=VMEM_LIMIT),
        name="mla_proj",
    )(h, *args, cosk, sink, cosq, sinq)


def _attn_kernel(qt_ref, k_ref, vt_ref, o_ref):
    tile = qt_ref.shape[3]
    qi = pl.program_id(2)
    neg = jnp.finfo(F32).min

    def step(j, carry, masked):
        heads = range(HEADS_PER_STEP)
        scores = [_dot(k_ref[0, hh, j], qt_ref[0, hh]) for hh in heads]
        weights = []
        for hh in heads:
            s = scores[hh]
            if masked:
                kpos = lax.broadcasted_iota(jnp.int32, s.shape, 0)
                qpos = lax.broadcasted_iota(jnp.int32, s.shape, 1)
                s = jnp.where(kpos <= qpos, s, neg)
            m_new = jnp.maximum(carry[hh][0], jnp.max(s, axis=0, keepdims=True))
            weights.append((m_new, jnp.exp2(s - m_new).astype(BF16)))
        out = []
        for hh in heads:
            m, acc = carry[hh]
            m_new, p = weights[hh]
            alpha = jnp.exp2(m - m_new)
            acc = alpha * acc + _dot(vt_ref[0, hh, j], p)
            out.append((m_new, acc))
        return tuple(out)

    init = tuple((jnp.full((1, tile), neg, F32), jnp.zeros((V_ROWS, tile), F32))
                 for _ in range(HEADS_PER_STEP))
    carry = lax.fori_loop(0, qi, functools.partial(step, masked=False), init)
    carry = step(qi, carry, masked=True)
    outs = [acc[0:V_HEAD, :] / acc[V_HEAD:V_HEAD + 1, :] for _, acc in carry]
    o_ref[0] = jnp.concatenate(outs, axis=0).T.astype(o_ref.dtype)


def _attention(qt, k5, vt5):
    bsz, nh, _, tlen = qt.shape
    nt, tile = k5.shape[2], k5.shape[3]
    hps = HEADS_PER_STEP
    return pl.pallas_call(
        _attn_kernel,
        grid=(bsz, nh // hps, nt),
        in_specs=[
            pl.BlockSpec((1, hps, HEAD_PAD, tile), lambda b, h, q: (b, h, 0, q)),
            pl.BlockSpec((1, hps, nt, tile, HEAD_PAD), lambda b, h, q: (b, h, 0, 0, 0)),
            pl.BlockSpec((1, hps, nt, V_ROWS, tile), lambda b, h, q: (b, h, 0, 0, 0)),
        ],
        out_specs=pl.BlockSpec((1, tile, hps * V_HEAD), lambda b, h, q: (b, q, h)),
        out_shape=jax.ShapeDtypeStruct((bsz, tlen, nh * V_HEAD), BF16),
        compiler_params=pltpu.CompilerParams(
            dimension_semantics=("arbitrary", "arbitrary", "arbitrary"),
            vmem_limit_bytes=VMEM_LIMIT),
        name="mla_attention",
    )(qt, k5, vt5)


def _out_proj_kernel(x_ref, o_ref, w_ref, y_ref):
    y_ref[0] = x_ref[0] + _dot(o_ref[0], w_ref[...])


def _out_proj(h, o, w_out):
    bsz, tlen, _ = h.shape
    tile = TILE_T
    row_spec = pl.BlockSpec((1, tile, D_MODEL), lambda b, t: (b, t, 0))
    w = w_out.astype(BF16)
    return pl.pallas_call(
        _out_proj_kernel,
        grid=(bsz, tlen // tile),
        in_specs=[row_spec, row_spec, _const_spec(w.shape)],
        out_specs=row_spec,
        out_shape=jax.ShapeDtypeStruct(h.shape, F32),
        compiler_params=pltpu.CompilerParams(
            dimension_semantics=("arbitrary", "arbitrary"), vmem_limit_bytes=VMEM_LIMIT),
        name="mla_out_proj",
    )(h, o, w)


def _final_norm_kernel(x_ref, g_ref, y_ref):
    y_ref[0] = _rms(x_ref[0], g_ref[...])


def _final_norm(h, g):
    bsz, tlen, _ = h.shape
    tile = TILE_T
    row_spec = pl.BlockSpec((1, tile, D_MODEL), lambda b, t: (b, t, 0))
    return pl.pallas_call(
        _final_norm_kernel,
        grid=(bsz, tlen // tile),
        in_specs=[row_spec, _const_spec((1, D_MODEL))],
        out_specs=row_spec,
        out_shape=jax.ShapeDtypeStruct(h.shape, F32),
        compiler_params=pltpu.CompilerParams(
            dimension_semantics=("arbitrary", "arbitrary"), vmem_limit_bytes=VMEM_LIMIT),
        name="final_norm",
    )(h, g.reshape(1, D_MODEL))


def _rope_tables(t_len):
    pos = jnp.arange(t_len, dtype=F32)
    inv_freq = ROPE_BASE ** (-jnp.arange(0, QK_ROPE, 2, dtype=F32) / QK_ROPE)
    ang = pos[:, None] * inv_freq[None, :]
    return jnp.cos(ang), jnp.sin(ang)


def kernel(x, meta_tokens, ev_norm, ev_w_in, ev_conv_a, ev_conv_b, ev_conv_b_bias, ev_gate_r_w, ev_gate_r_b, ev_gate_i_w, ev_gate_i_b, ev_lru_lambda, ev_w_out, od_norm, od_w_in, od_q_norm, od_kv_norm, od_w_uq, od_w_ukv, od_w_out, ffn_norm, ffn_w_up, ffn_conv_w, ffn_conv_b, ffn_w_down, final_norm):
    bsz, seq, _ = x.shape
    depth = ffn_norm.shape[0]
    t_real = N_META + seq
    t_pad = -(-t_real // TILE_T) * TILE_T
    meta = jnp.broadcast_to(meta_tokens[None].astype(x.dtype), (bsz, N_META, D_MODEL))
    h = jnp.concatenate(
        [meta, x, jnp.zeros((bsz, t_pad - t_real, D_MODEL), x.dtype)], axis=1)
    cos, sin = _rope_tables(t_pad)
    for layer in range(depth):
        j = layer // 2
        if layer % 2 == 0:
            h = _even_layer(h, ev_norm[j], ev_w_in[j], ev_conv_a[j], ev_conv_b[j],
                            ev_conv_b_bias[j], _block_diag(ev_gate_r_w[j]).astype(BF16),
                            ev_gate_r_b[j], _block_diag(ev_gate_i_w[j]).astype(BF16),
                            ev_gate_i_b[j], ev_lru_lambda[j], ev_w_out[j])
        else:
            qt, k5, vt5 = _mla_proj(h, cos, sin, od_norm[j], od_w_in[j], od_q_norm[j],
                                    od_kv_norm[j], od_w_uq[j], od_w_ukv[j])
            o = _attention(qt, k5, vt5)
            h = _out_proj(h, o, od_w_out[j])
        h = _ffn_layer(h, ffn_norm[layer], ffn_w_up[layer], ffn_conv_w[layer],
                       ffn_conv_b[layer], ffn_w_down[layer])
    h = _final_norm(h, final_norm)
    return h[:, N_META:t_real]
```

```python
import functools
import math

import jax
import jax.numpy as jnp
from jax import lax
from jax.experimental import pallas as pl
from jax.experimental.pallas import tpu as pltpu

D_MODEL = 1024
N_META = 16
EPS = 1e-6
CONV_WIDTH = 512
LRU_WIDTH = 512
LRU_C = 8.0
MLA_HEADS = 16
QK_NOPE = 64
QK_ROPE = 32
QK_HEAD = QK_NOPE + QK_ROPE
V_HEAD = 64
Q_LORA = 384
KV_LORA = 256
ROPE_BASE = 10000.0
D_FF = 2816

LANES = 128
SUBLANES = 8
HEAD_PAD = 128
V_ROWS = V_HEAD + 16
HEADS_PER_STEP = 8
TILE_T = 256
GROUPS = TILE_T // SUBLANES
FF_CHUNK = 256
VMEM_LIMIT = 56 * 1024 * 1024

F32 = jnp.float32
BF16 = jnp.bfloat16


def _rms(x, g):
    ms = jnp.mean(x * x, axis=-1, keepdims=True)
    return x * lax.rsqrt(ms + EPS) * g


def _sigmoid(x):
    return 1.0 / (1.0 + jnp.exp(-x))


def _gelu_tanh(x):
    c = math.sqrt(2.0 / math.pi)
    return x * (0.5 * (1.0 + jnp.tanh(c * (x + 0.044715 * (x * x * x)))))


def _dot(a, b):
    return jnp.dot(a, b, preferred_element_type=F32)


def _dot_nt(a, b):
    return lax.dot_general(a, b, (((1,), (1,)), ((), ())), preferred_element_type=F32)


def _row_groups(x):
    return [x[v * SUBLANES:(v + 1) * SUBLANES, :] for v in range(x.shape[0] // SUBLANES)]


def _delays(u, prev, kmax):
    tile = u.shape[0]
    first = lax.broadcasted_iota(jnp.int32, (SUBLANES, u.shape[1]), 0) == 0
    wrapped = []
    for i in range(kmax):
        cur = u[tile - (kmax - i) * SUBLANES:tile - (kmax - i - 1) * SUBLANES, :]
        old = prev[i * SUBLANES:(i + 1) * SUBLANES, :]
        wrapped.append(jnp.where(first, pltpu.roll(old, 1, axis=0), pltpu.roll(cur, 1, axis=0)))
    return [jnp.concatenate(wrapped[kmax - k:] + [u[:tile - k * SUBLANES, :]], axis=0)
            for k in range(1, kmax + 1)]


def _causal_conv(u, prev, taps):
    k = taps.shape[0]
    delayed = _delays(u, prev, k - 1)
    acc = delayed[k - 2] * taps[0:1, :]
    for j in range(1, k - 1):
        acc = acc + delayed[k - 2 - j] * taps[j:j + 1, :]
    return acc + u * taps[k - 1:k, :]


def _lru_scan(a, b, carry):
    a_rows, b_rows = _row_groups(a), _row_groups(b)
    prod, hzero = [a_rows[0]], [b_rows[0]]
    for v in range(1, len(a_rows)):
        prod.append(a_rows[v] * prod[-1])
        hzero.append(a_rows[v] * hzero[-1] + b_rows[v])
    pa, ph = prod[-1], hzero[-1]
    sub = lax.broadcasted_iota(jnp.int32, pa.shape, 0)
    for d in (1, 2, 4):
        ok = sub >= d
        ph = jnp.where(ok, pa * pltpu.roll(ph, d, axis=0) + ph, ph)
        pa = jnp.where(ok, pa * pltpu.roll(pa, d, axis=0), pa)
    ends = ph + pa * carry
    init = jnp.where(sub == 0, carry, pltpu.roll(ends, 1, axis=0))
    h = jnp.concatenate([hz + pr * init for hz, pr in zip(hzero, prod)], axis=0)
    return h, ends[SUBLANES - 1:SUBLANES, :]


def _even_kernel(x_ref, g_ref, win_ref, ca_ref, cb_ref, cbb_ref, rw_ref, rb_ref,
                 iw_ref, ib_ref, lam_ref, wout_ref, o_ref, ztail, xbtail, hstate):
    tile = x_ref.shape[1]
    t = pl.program_id(1)

    @pl.when(t == 0)
    def _():
        ztail[...] = jnp.zeros_like(ztail)
        xbtail[...] = jnp.zeros_like(xbtail)
        hstate[...] = jnp.zeros_like(hstate)

    hn = _rms(x_ref[0], g_ref[...]).astype(BF16)
    u = _dot(hn, win_ref[...])
    cw = CONV_WIDTH
    gb = u[:, 0:cw]
    gc = u[:, cw:2 * cw]
    xa = u[:, 2 * cw:3 * cw]
    xb = u[:, 3 * cw:3 * cw + LRU_WIDTH]
    gate = u[:, 3 * cw + LRU_WIDTH:]

    z = gc * xa
    y_a = gb * _causal_conv(z, ztail[...], ca_ref[...])
    ztail[...] = z[tile - ztail.shape[0]:, :]

    xc = _causal_conv(xb, xbtail[...], cb_ref[...]) + cbb_ref[...]
    xbtail[...] = xb[tile - xbtail.shape[0]:, :]

    xcb = xc.astype(BF16)
    r = _sigmoid(_dot(xcb, rw_ref[...]) + rb_ref[...])
    i = _sigmoid(_dot(xcb, iw_ref[...]) + ib_ref[...])
    nlam = -lam_ref[...]
    softplus = jnp.maximum(nlam, 0.0) + jnp.log1p(jnp.exp(-jnp.abs(nlam)))
    log_a = -LRU_C * r * softplus
    a = jnp.exp(log_a)
    th = jnp.tanh(log_a)
    mult = jnp.sqrt(-2.0 * th / (1.0 - th))
    h, hstate[...] = _lru_scan(a, mult * (i * xc), hstate[...])

    y_b = _gelu_tanh(gate) * h
    y = jnp.concatenate([y_a, y_b], axis=-1).astype(BF16)
    o_ref[0] = x_ref[0] + _dot(y, wout_ref[...])


def _const_spec(shape):
    nd = len(shape)
    return pl.BlockSpec(shape, lambda b, t: (0,) * nd)


def _even_layer(h, g, w_in, conv_a, conv_b, conv_b_bias, rw, rb, iw, ib, lam, w_out):
    bsz, tlen, _ = h.shape
    tile = TILE_T
    row_spec = pl.BlockSpec((1, tile, D_MODEL), lambda b, t: (b, t, 0))
    args = (g.reshape(1, D_MODEL), w_in.astype(BF16), conv_a, conv_b,
            conv_b_bias.reshape(1, LRU_WIDTH), rw, rb.reshape(1, LRU_WIDTH), iw,
            ib.reshape(1, LRU_WIDTH), lam.reshape(1, LRU_WIDTH), w_out.astype(BF16))
    return pl.pallas_call(
        _even_kernel,
        grid=(bsz, tlen // tile),
        in_specs=[row_spec] + [_const_spec(a.shape) for a in args],
        out_specs=row_spec,
        out_shape=jax.ShapeDtypeStruct(h.shape, F32),
        scratch_shapes=[
            pltpu.VMEM(((conv_a.shape[0] - 1) * SUBLANES, CONV_WIDTH), F32),
            pltpu.VMEM(((conv_b.shape[0] - 1) * SUBLANES, LRU_WIDTH), F32),
            pltpu.VMEM((1, LRU_WIDTH), F32),
        ],
        compiler_params=pltpu.CompilerParams(
            dimension_semantics=("arbitrary", "arbitrary"), vmem_limit_bytes=VMEM_LIMIT),
        name="even_mixer",
    )(h, *args)


def _block_diag(w):
    nh, d, _ = w.shape
    eye = jnp.eye(nh, dtype=w.dtype)
    return jnp.einsum("hij,hg->higj", w, eye).reshape(nh * d, nh * d)


def _ffn_kernel(*refs, attn_input, final_norm):
    refs = list(refs)
    x_ref = refs.pop(0)
    attn_ref, wo_ref = (refs.pop(0), refs.pop(0)) if attn_input else (None, None)
    g_ref, wup_ref, cw_ref, cb_ref, wdn_ref = refs[:5]
    fg_ref = refs[5] if final_norm else None
    o_ref, hn_ref, act_ref, tail = refs[-4:]
    tile = x_ref.shape[1]
    t = pl.program_id(1)

    @pl.when(t == 0)
    def _():
        tail[...] = jnp.zeros_like(tail)

    if attn_input:
        o_ref[0] = x_ref[0] + _dot(attn_ref[0], wo_ref[...])
    else:
        o_ref[0] = x_ref[0]
    hn_ref[...] = _rms(o_ref[0], g_ref[...]).astype(BF16)
    for c in range(D_FF // FF_CHUNK):
        halves = []
        for part in range(2):
            col = part * D_FF + c * FF_CHUNK
            cs = slice(col, col + FF_CHUNK)
            u = _dot(hn_ref[...], wup_ref[:, cs])
            halves.append(_causal_conv(u, tail[:, cs], cw_ref[:, cs]) + cb_ref[:, cs])
            tail[:, cs] = u[tile - tail.shape[0]:, :]
        a, gte = halves
        act_ref[:, c * FF_CHUNK:(c + 1) * FF_CHUNK] = (a * _sigmoid(a) * gte).astype(BF16)
    y = o_ref[0] + _dot(act_ref[...], wdn_ref[...])
    o_ref[0] = _rms(y, fg_ref[...]) if final_norm else y


def _ffn_layer(h, g, w_up, conv_w, conv_b, w_down, attn=None, final_g=None):
    bsz, tlen, _ = h.shape
    tile = TILE_T
    row_spec = pl.BlockSpec((1, tile, D_MODEL), lambda b, t: (b, t, 0))
    args = (g.reshape(1, D_MODEL), w_up.astype(BF16), conv_w,
            conv_b.reshape(1, 2 * D_FF), w_down.astype(BF16))
    if final_g is not None:
        args = args + (final_g.reshape(1, D_MODEL),)
    specs = [_const_spec(a.shape) for a in args]
    if attn is not None:
        w_o = attn[1].astype(BF16)
        args = (attn[0], w_o) + args
        specs = [row_spec, _const_spec(w_o.shape)] + specs
    return pl.pallas_call(
        functools.partial(_ffn_kernel, attn_input=attn is not None,
                          final_norm=final_g is not None),
        grid=(bsz, tlen // tile),
        in_specs=[row_spec] + specs,
        out_specs=row_spec,
        out_shape=jax.ShapeDtypeStruct(h.shape, F32),
        scratch_shapes=[
            pltpu.VMEM((tile, D_MODEL), BF16),
            pltpu.VMEM((tile, D_FF), BF16),
            pltpu.VMEM(((conv_w.shape[0] - 1) * SUBLANES, 2 * D_FF), F32),
        ],
        compiler_params=pltpu.CompilerParams(
            dimension_semantics=("arbitrary", "arbitrary"), vmem_limit_bytes=VMEM_LIMIT),
        name="conv_ffn",
    )(h, *args)


def _mla_proj_kernel(x_ref, g_ref, win_ref, qn_ref, kvn_ref, wqt_ref, wk_ref, wvt_ref,
                     cosk_ref, sink_ref, cosq_ref, sinq_ref, qt_ref, k_ref, vt_ref):
    x = x_ref[0]
    hn = _rms(x, g_ref[...]).astype(BF16)
    u = _dot(hn, win_ref[...])
    cq = u[:, 0:Q_LORA]
    ckv = u[:, Q_LORA:Q_LORA + KV_LORA]
    kr = u[:, Q_LORA + KV_LORA:Q_LORA + KV_LORA + HEAD_PAD]
    kr_rot = u[:, Q_LORA + KV_LORA + HEAD_PAD:]
    cqn = _rms(cq, qn_ref[...]).astype(BF16)
    ckvn = _rms(ckv, kvn_ref[...]).astype(BF16)

    qt = _dot_nt(wqt_ref[...], cqn)
    kn = _dot(ckvn, wk_ref[...])
    vt = _dot_nt(wvt_ref[...], ckvn)
    k_rope = kr * cosk_ref[...] + kr_rot * sink_ref[...]
    cosq = cosq_ref[...]
    sinq = sinq_ref[...]
    scale = QK_HEAD ** -0.5 * math.log2(math.e)
    zeros = jnp.zeros((HEAD_PAD - QK_HEAD, qt.shape[1]), F32)
    ones = jnp.ones((V_ROWS - V_HEAD, qt.shape[1]), F32)
    for h in range(MLA_HEADS):
        qh = qt[h * HEAD_PAD:(h + 1) * HEAD_PAD, :]
        roped = qh[QK_NOPE:QK_HEAD, :] * cosq + qh[QK_HEAD:, :] * sinq
        qt_ref[0, h] = jnp.concatenate(
            [qh[0:QK_NOPE, :] * scale, roped * scale, zeros], axis=0).astype(BF16)
        k_ref[0, h, 0] = (kn[:, h * HEAD_PAD:(h + 1) * HEAD_PAD] + k_rope).astype(BF16)
        vt_ref[0, h, 0] = jnp.concatenate(
            [vt[h * V_HEAD:(h + 1) * V_HEAD, :], ones], axis=0).astype(BF16)


def _rot_cols(w):
    half = QK_ROPE // 2
    return jnp.concatenate([-w[..., half:], w[..., :half]], axis=-1)


def _mla_proj(h, cos, sin, g, w_in, q_norm, kv_norm, w_uq, w_ukv):
    bsz, tlen, _ = h.shape
    tile = TILE_T
    nt = tlen // tile
    w_kr = w_in[:, Q_LORA + KV_LORA:]
    padl = jnp.zeros((D_MODEL, QK_NOPE), F32)
    padr = jnp.zeros((D_MODEL, HEAD_PAD - QK_HEAD), F32)
    win_ext = jnp.concatenate(
        [w_in[:, :Q_LORA + KV_LORA], padl, w_kr, padr, padl, _rot_cols(w_kr), padr],
        axis=-1).astype(BF16)
    wq = w_uq.reshape(Q_LORA, MLA_HEADS, QK_HEAD)
    wq_ext = jnp.concatenate([wq, _rot_cols(wq[..., QK_NOPE:])], axis=-1)
    wqt = wq_ext.reshape(Q_LORA, MLA_HEADS * HEAD_PAD).T.astype(BF16)
    wkv = w_ukv.reshape(KV_LORA, MLA_HEADS, QK_NOPE + V_HEAD)
    wk = jnp.concatenate(
        [wkv[..., :QK_NOPE], jnp.zeros((KV_LORA, MLA_HEADS, HEAD_PAD - QK_NOPE), F32)],
        axis=-1).reshape(KV_LORA, MLA_HEADS * HEAD_PAD).astype(BF16)
    wvt = wkv[..., QK_NOPE:].reshape(KV_LORA, MLA_HEADS * V_HEAD).T.astype(BF16)
    cos2 = jnp.concatenate([cos, cos], axis=-1)
    sin2 = jnp.concatenate([sin, sin], axis=-1)
    lpad = ((0, 0), (QK_NOPE, HEAD_PAD - QK_HEAD))
    cosk, sink = jnp.pad(cos2, lpad), jnp.pad(sin2, lpad)
    cosq, sinq = cos2.T, sin2.T

    args = (g.reshape(1, D_MODEL), win_ext, q_norm.reshape(1, Q_LORA),
            kv_norm.reshape(1, KV_LORA), wqt, wk, wvt)
    row_spec = pl.BlockSpec((1, tile, D_MODEL), lambda b, t: (b, t, 0))
    in_specs = [row_spec] + [_const_spec(a.shape) for a in args] + [
        pl.BlockSpec((tile, HEAD_PAD), lambda b, t: (t, 0)),
        pl.BlockSpec((tile, HEAD_PAD), lambda b, t: (t, 0)),
        pl.BlockSpec((QK_ROPE, tile), lambda b, t: (0, t)),
        pl.BlockSpec((QK_ROPE, tile), lambda b, t: (0, t)),
    ]
    out_shape = (
        jax.ShapeDtypeStruct((bsz, MLA_HEADS, HEAD_PAD, tlen), BF16),
        jax.ShapeDtypeStruct((bsz, MLA_HEADS, nt, tile, HEAD_PAD), BF16),
        jax.ShapeDtypeStruct((bsz, MLA_HEADS, nt, V_ROWS, tile), BF16),
    )
    out_specs = (
        pl.BlockSpec((1, MLA_HEADS, HEAD_PAD, tile), lambda b, t: (b, 0, 0, t)),
        pl.BlockSpec((1, MLA_HEADS, 1, tile, HEAD_PAD), lambda b, t: (b, 0, t, 0, 0)),
        pl.BlockSpec((1, MLA_HEADS, 1, V_ROWS, tile), lambda b, t: (b, 0, t, 0, 0)),
    )
    return pl.pallas_call(
        _mla_proj_kernel,
        grid=(bsz, nt),
        in_specs=in_specs,
        out_specs=out_specs,
        out_shape=out_shape,
        compiler_params=pltpu.CompilerParams(
            dimension_semantics=("arbitrary", "arbitrary"), vmem_limit_bytes=VMEM_LIMIT),
        name="mla_proj",
    )(h, *args, cosk, sink, cosq, sinq)


def _strided_time(i):
    return (i & (SUBLANES - 1)) * GROUPS + (i >> 3)


def _attn_kernel(qt_ref, k_ref, vt_ref, o_ref):
    tile = qt_ref.shape[3]
    qi = pl.program_id(2)
    neg = jnp.finfo(F32).min

    def step(j, carry, masked):
        heads = range(HEADS_PER_STEP)
        scores = [_dot(k_ref[0, hh, j], qt_ref[0, hh]) for hh in heads]
        if masked:
            kpos = _strided_time(lax.broadcasted_iota(jnp.int32, (tile, tile), 0))
            qpos = _strided_time(lax.broadcasted_iota(jnp.int32, (tile, tile), 1))
            keep = kpos <= qpos
        weights = []
        for hh in heads:
            s = scores[hh]
            if masked:
                s = jnp.where(keep, s, neg)
            m_new = jnp.maximum(carry[hh][0], jnp.max(s, axis=0, keepdims=True))
            weights.append((m_new, jnp.exp2(s - m_new).astype(BF16)))
        out = []
        for hh in heads:
            m, acc = carry[hh]
            m_new, p = weights[hh]
            alpha = jnp.exp2(m - m_new)
            acc = alpha * acc + _dot(vt_ref[0, hh, j], p)
            out.append((m_new, acc))
        return tuple(out)

    init = tuple((jnp.full((1, tile), neg, F32), jnp.zeros((V_ROWS, tile), F32))
                 for _ in range(HEADS_PER_STEP))
    carry = lax.fori_loop(0, qi, functools.partial(step, masked=False), init)
    carry = step(qi, carry, masked=True)
    outs = [acc[0:V_HEAD, :] / acc[V_HEAD:V_HEAD + 1, :] for _, acc in carry]
    o_ref[0] = jnp.concatenate(outs, axis=0).T.astype(o_ref.dtype)


def _attention(qt, k5, vt5):
    bsz, nh, _, tlen = qt.shape
    nt, tile = k5.shape[2], k5.shape[3]
    hps = HEADS_PER_STEP
    return pl.pallas_call(
        _attn_kernel,
        grid=(bsz, nh // hps, nt),
        in_specs=[
            pl.BlockSpec((1, hps, HEAD_PAD, tile), lambda b, h, q: (b, h, 0, q)),
            pl.BlockSpec((1, hps, nt, tile, HEAD_PAD), lambda b, h, q: (b, h, 0, 0, 0)),
            pl.BlockSpec((1, hps, nt, V_ROWS, tile), lambda b, h, q: (b, h, 0, 0, 0)),
        ],
        out_specs=pl.BlockSpec((1, tile, hps * V_HEAD), lambda b, h, q: (b, q, h)),
        out_shape=jax.ShapeDtypeStruct((bsz, tlen, nh * V_HEAD), BF16),
        compiler_params=pltpu.CompilerParams(
            dimension_semantics=("arbitrary", "arbitrary", "arbitrary"),
            vmem_limit_bytes=VMEM_LIMIT),
        name="mla_attention",
    )(qt, k5, vt5)


def _rope_tables(t_len):
    pos = jnp.arange(t_len, dtype=F32)
    inv_freq = ROPE_BASE ** (-jnp.arange(0, QK_ROPE, 2, dtype=F32) / QK_ROPE)
    ang = pos[:, None] * inv_freq[None, :]
    return jnp.cos(ang), jnp.sin(ang)


def _to_strided(a, axis):
    shp = a.shape
    nt = shp[axis] // TILE_T
    a = a.reshape(shp[:axis] + (nt, SUBLANES, GROUPS) + shp[axis + 1:])
    return jnp.swapaxes(a, axis + 1, axis + 2).reshape(shp)


def _from_strided(a, axis):
    shp = a.shape
    nt = shp[axis] // TILE_T
    a = a.reshape(shp[:axis] + (nt, GROUPS, SUBLANES) + shp[axis + 1:])
    return jnp.swapaxes(a, axis + 1, axis + 2).reshape(shp)


def kernel(x, meta_tokens, ev_norm, ev_w_in, ev_conv_a, ev_conv_b, ev_conv_b_bias, ev_gate_r_w, ev_gate_r_b, ev_gate_i_w, ev_gate_i_b, ev_lru_lambda, ev_w_out, od_norm, od_w_in, od_q_norm, od_kv_norm, od_w_uq, od_w_ukv, od_w_out, ffn_norm, ffn_w_up, ffn_conv_w, ffn_conv_b, ffn_w_down, final_norm):
    bsz, seq, _ = x.shape
    depth = ffn_norm.shape[0]
    t_real = N_META + seq
    t_pad = -(-t_real // TILE_T) * TILE_T
    meta = jnp.broadcast_to(meta_tokens[None].astype(x.dtype), (bsz, N_META, D_MODEL))
    h = jnp.concatenate(
        [meta, x, jnp.zeros((bsz, t_pad - t_real, D_MODEL), x.dtype)], axis=1)
    h = _to_strided(h, 1)
    cos, sin = _rope_tables(t_pad)
    cos, sin = _to_strided(cos, 0), _to_strided(sin, 0)
    for layer in range(depth):
        j = layer // 2
        attn = None
        if layer % 2 == 0:
            h = _even_layer(h, ev_norm[j], ev_w_in[j], ev_conv_a[j], ev_conv_b[j],
                            ev_conv_b_bias[j], _block_diag(ev_gate_r_w[j]).astype(BF16),
                            ev_gate_r_b[j], _block_diag(ev_gate_i_w[j]).astype(BF16),
                            ev_gate_i_b[j], ev_lru_lambda[j], ev_w_out[j])
        else:
            qt, k5, vt5 = _mla_proj(h, cos, sin, od_norm[j], od_w_in[j], od_q_norm[j],
                                    od_kv_norm[j], od_w_uq[j], od_w_ukv[j])
            attn = (_attention(qt, k5, vt5), od_w_out[j])
        h = _ffn_layer(h, ffn_norm[layer], ffn_w_up[layer], ffn_conv_w[layer],
                       ffn_conv_b[layer], ffn_w_down[layer], attn=attn,
                       final_g=final_norm if layer == depth - 1 else None)
    return _from_strided(h, 1)[:, N_META:t_real]
```

```python
import functools
import math

import jax
import jax.numpy as jnp
from jax import lax
from jax.experimental import pallas as pl
from jax.experimental.pallas import tpu as pltpu

D_MODEL = 1024
N_META = 16
EPS = 1e-6
CONV_WIDTH = 512
LRU_WIDTH = 512
LRU_C = 8.0
MLA_HEADS = 16
QK_NOPE = 64
QK_ROPE = 32
QK_HEAD = QK_NOPE + QK_ROPE
V_HEAD = 64
Q_LORA = 384
KV_LORA = 256
ROPE_BASE = 10000.0
D_FF = 2816

LANES = 128
SUBLANES = 8
HEAD_PAD = 128
V_ROWS = V_HEAD + 16
HEADS_PER_STEP = 8
TILE_T = 256
GROUPS = TILE_T // SUBLANES
FF_CHUNK = 256
VMEM_LIMIT = 56 * 1024 * 1024

F32 = jnp.float32
BF16 = jnp.bfloat16


def _rms(x, g):
    ms = jnp.mean(x * x, axis=-1, keepdims=True)
    return x * lax.rsqrt(ms + EPS) * g


def _sigmoid(x):
    return 1.0 / (1.0 + jnp.exp(-x))


def _gelu_tanh(x):
    c = math.sqrt(2.0 / math.pi)
    return x * (0.5 * (1.0 + jnp.tanh(c * (x + 0.044715 * (x * x * x)))))


def _dot(a, b):
    return jnp.dot(a, b, preferred_element_type=F32)


def _dot_nt(a, b):
    return lax.dot_general(a, b, (((1,), (1,)), ((), ())), preferred_element_type=F32)


def _row_groups(x):
    return [x[v * SUBLANES:(v + 1) * SUBLANES, :] for v in range(x.shape[0] // SUBLANES)]


def _delays(u, prev, kmax):
    tile = u.shape[0]
    first = lax.broadcasted_iota(jnp.int32, (SUBLANES, u.shape[1]), 0) == 0
    wrapped = []
    for i in range(kmax):
        cur = u[tile - (kmax - i) * SUBLANES:tile - (kmax - i - 1) * SUBLANES, :]
        old = prev[i * SUBLANES:(i + 1) * SUBLANES, :]
        wrapped.append(jnp.where(first, pltpu.roll(old, 1, axis=0), pltpu.roll(cur, 1, axis=0)))
    return [jnp.concatenate(wrapped[kmax - k:] + [u[:tile - k * SUBLANES, :]], axis=0)
            for k in range(1, kmax + 1)]


def _causal_conv(u, prev, taps):
    k = taps.shape[0]
    delayed = _delays(u, prev, k - 1)
    acc = delayed[k - 2] * taps[0:1, :]
    for j in range(1, k - 1):
        acc = acc + delayed[k - 2 - j] * taps[j:j + 1, :]
    return acc + u * taps[k - 1:k, :]


def _conv_tail(u, kmax, state_at):
    tile = u.shape[0]
    if state_at is None:
        return u[tile - kmax * SUBLANES:, :]
    s, v = divmod(state_at, GROUPS)
    assert v >= kmax - 1
    rows = []
    for i in range(kmax):
        grp = u[(v - (kmax - 1 - i)) * SUBLANES:(v - (kmax - 2 - i)) * SUBLANES, :]
        rows.append(pltpu.roll(grp, SUBLANES - 1 - s, axis=0) if s != SUBLANES - 1 else grp)
    return jnp.concatenate(rows, axis=0)


def _lru_scan(a, b, carry, state_at):
    a_rows, b_rows = _row_groups(a), _row_groups(b)
    prod, hzero = [a_rows[0]], [b_rows[0]]
    for v in range(1, len(a_rows)):
        prod.append(a_rows[v] * prod[-1])
        hzero.append(a_rows[v] * hzero[-1] + b_rows[v])
    pa, ph = prod[-1], hzero[-1]
    sub = lax.broadcasted_iota(jnp.int32, pa.shape, 0)
    for d in (1, 2, 4):
        ok = sub >= d
        ph = jnp.where(ok, pa * pltpu.roll(ph, d, axis=0) + ph, ph)
        pa = jnp.where(ok, pa * pltpu.roll(pa, d, axis=0), pa)
    ends = ph + pa * carry
    init = jnp.where(sub == 0, carry, pltpu.roll(ends, 1, axis=0))
    h = jnp.concatenate([hz + pr * init for hz, pr in zip(hzero, prod)], axis=0)
    if state_at is None:
        return h, ends[SUBLANES - 1:SUBLANES, :]
    s, v = divmod(state_at, GROUPS)
    row = v * SUBLANES + s
    return h, h[row:row + 1, :]


def _even_kernel(*refs, state_at):
    (x_ref, zin_ref, xbin_ref, hin_ref, g_ref, win_ref, ca_ref, cb_ref, cbb_ref, rw_ref, rb_ref,
     iw_ref, ib_ref, lam_ref, wout_ref, o_ref) = refs[:16]
    ztail, xbtail, hstate = refs[-3:]
    t = pl.program_id(1)

    @pl.when(t == 0)
    def _():
        ztail[...] = zin_ref[...]
        xbtail[...] = xbin_ref[...]
        hstate[...] = hin_ref[...]

    hn = _rms(x_ref[0], g_ref[...]).astype(BF16)
    u = _dot(hn, win_ref[...])
    cw = CONV_WIDTH
    gb = u[:, 0:cw]
    gc = u[:, cw:2 * cw]
    xa = u[:, 2 * cw:3 * cw]
    xb = u[:, 3 * cw:3 * cw + LRU_WIDTH]
    gate = u[:, 3 * cw + LRU_WIDTH:]

    z = gc * xa
    y_a = gb * _causal_conv(z, ztail[...], ca_ref[...])
    ztail[...] = _conv_tail(z, ca_ref.shape[0] - 1, state_at)

    xc = _causal_conv(xb, xbtail[...], cb_ref[...]) + cbb_ref[...]
    xbtail[...] = _conv_tail(xb, cb_ref.shape[0] - 1, state_at)

    xcb = xc.astype(BF16)
    r = _sigmoid(_dot(xcb, rw_ref[...]) + rb_ref[...])
    i = _sigmoid(_dot(xcb, iw_ref[...]) + ib_ref[...])
    nlam = -lam_ref[...]
    softplus = jnp.maximum(nlam, 0.0) + jnp.log1p(jnp.exp(-jnp.abs(nlam)))
    log_a = -LRU_C * r * softplus
    a = jnp.exp(log_a)
    th = jnp.tanh(log_a)
    mult = jnp.sqrt(-2.0 * th / (1.0 - th))
    h, hstate[...] = _lru_scan(a, mult * (i * xc), hstate[...], state_at)

    y_b = _gelu_tanh(gate) * h
    y = jnp.concatenate([y_a, y_b], axis=-1).astype(BF16)
    o_ref[0] = x_ref[0] + _dot(y, wout_ref[...])
    if state_at is not None:
        zout_ref, xbout_ref, hout_ref = refs[16:19]
        zout_ref[...] = ztail[...]
        xbout_ref[...] = xbtail[...]
        hout_ref[...] = hstate[...]


def _const_spec(shape):
    nd = len(shape)
    return pl.BlockSpec(shape, lambda b, t: (0,) * nd)


def _shapes(arrays):
    return tuple(jax.ShapeDtypeStruct(a.shape, a.dtype) for a in arrays)


def _even_layer(h, state, g, w_in, conv_a, conv_b, conv_b_bias, rw, rb, iw, ib, lam, w_out,
                state_at=None):
    bsz, tlen, _ = h.shape
    tile = TILE_T
    row_spec = pl.BlockSpec((1, tile, D_MODEL), lambda b, t: (b, t, 0))
    args = tuple(state) + (
        g.reshape(1, D_MODEL), w_in, conv_a, conv_b,
        conv_b_bias.reshape(1, LRU_WIDTH), rw, rb.reshape(1, LRU_WIDTH), iw,
        ib.reshape(1, LRU_WIDTH), lam.reshape(1, LRU_WIDTH), w_out)
    out_shape = [jax.ShapeDtypeStruct(h.shape, F32)]
    out_specs = [row_spec]
    if state_at is not None:
        assert bsz == 1 and tlen == tile
        out_shape += list(_shapes(state))
        out_specs += [_const_spec(s.shape) for s in state]
    res = pl.pallas_call(
        functools.partial(_even_kernel, state_at=state_at),
        grid=(bsz, tlen // tile),
        in_specs=[row_spec] + [_const_spec(a.shape) for a in args],
        out_specs=out_specs,
        out_shape=out_shape,
        scratch_shapes=[pltpu.VMEM(s.shape, F32) for s in state],
        compiler_params=pltpu.CompilerParams(
            dimension_semantics=("arbitrary", "arbitrary"), vmem_limit_bytes=VMEM_LIMIT),
        name="even_mixer",
    )(h, *args)
    return res[0], (tuple(res[1:]) if state_at is not None else None)


def _block_diag(w):
    nh, d, _ = w.shape
    eye = jnp.eye(nh, dtype=w.dtype)
    return jnp.einsum("hij,hg->higj", w, eye).reshape(nh * d, nh * d)


def _ffn_kernel(*refs, attn_input, final_norm, state_at):
    refs = list(refs)
    x_ref, tin_ref = refs.pop(0), refs.pop(0)
    attn_ref, wo_ref = (refs.pop(0), refs.pop(0)) if attn_input else (None, None)
    g_ref, wup_ref, cw_ref, cb_ref, wdn_ref = refs[:5]
    del refs[:5]
    fg_ref = refs.pop(0) if final_norm else None
    o_ref = refs.pop(0)
    tout_ref = refs.pop(0) if state_at is not None else None
    hn_ref, act_ref, tail = refs
    t = pl.program_id(1)

    @pl.when(t == 0)
    def _():
        tail[...] = tin_ref[...]

    if attn_input:
        o_ref[0] = x_ref[0] + _dot(attn_ref[0], wo_ref[...])
    else:
        o_ref[0] = x_ref[0]
    hn_ref[...] = _rms(o_ref[0], g_ref[...]).astype(BF16)
    for c in range(D_FF // FF_CHUNK):
        halves = []
        for part in range(2):
            col = part * D_FF + c * FF_CHUNK
            cs = slice(col, col + FF_CHUNK)
            u = _dot(hn_ref[...], wup_ref[:, cs])
            halves.append(_causal_conv(u, tail[:, cs], cw_ref[:, cs]) + cb_ref[:, cs])
            tail[:, cs] = _conv_tail(u, cw_ref.shape[0] - 1, state_at)
        a, gte = halves
        act_ref[:, c * FF_CHUNK:(c + 1) * FF_CHUNK] = (a * _sigmoid(a) * gte).astype(BF16)
    y = o_ref[0] + _dot(act_ref[...], wdn_ref[...])
    o_ref[0] = _rms(y, fg_ref[...]) if final_norm else y
    if state_at is not None:
        tout_ref[...] = tail[...]


def _ffn_layer(h, tail, g, w_up, conv_w, conv_b, w_down, attn=None, final_g=None, state_at=None):
    bsz, tlen, _ = h.shape
    tile = TILE_T
    row_spec = pl.BlockSpec((1, tile, D_MODEL), lambda b, t: (b, t, 0))
    args = (g.reshape(1, D_MODEL), w_up, conv_w, conv_b.reshape(1, 2 * D_FF), w_down)
    if final_g is not None:
        args = args + (final_g.reshape(1, D_MODEL),)
    specs = [_const_spec(a.shape) for a in args]
    if attn is not None:
        args = attn + args
        specs = [row_spec, _const_spec(attn[1].shape)] + specs
    out_shape = [jax.ShapeDtypeStruct(h.shape, F32)]
    out_specs = [row_spec]
    if state_at is not None:
        assert bsz == 1 and tlen == tile
        out_shape.append(jax.ShapeDtypeStruct(tail.shape, F32))
        out_specs.append(_const_spec(tail.shape))
    res = pl.pallas_call(
        functools.partial(_ffn_kernel, attn_input=attn is not None,
                          final_norm=final_g is not None, state_at=state_at),
        grid=(bsz, tlen // tile),
        in_specs=[row_spec, _const_spec(tail.shape)] + specs,
        out_specs=out_specs,
        out_shape=out_shape,
        scratch_shapes=[
            pltpu.VMEM((tile, D_MODEL), BF16),
            pltpu.VMEM((tile, D_FF), BF16),
            pltpu.VMEM(tail.shape, F32),
        ],
        compiler_params=pltpu.CompilerParams(
            dimension_semantics=("arbitrary", "arbitrary"), vmem_limit_bytes=VMEM_LIMIT),
        name="conv_ffn",
    )(h, tail, *args)
    return res[0], (res[1] if state_at is not None else None)


def _mla_proj_kernel(x_ref, g_ref, win_ref, qn_ref, kvn_ref, wqt_ref, wk_ref, wvt_ref,
                     cosk_ref, sink_ref, cosq_ref, sinq_ref, qt_ref, k_ref, vt_ref):
    x = x_ref[0]
    hn = _rms(x, g_ref[...]).astype(BF16)
    u = _dot(hn, win_ref[...])
    cq = u[:, 0:Q_LORA]
    ckv = u[:, Q_LORA:Q_LORA + KV_LORA]
    kr = u[:, Q_LORA + KV_LORA:Q_LORA + KV_LORA + HEAD_PAD]
    kr_rot = u[:, Q_LORA + KV_LORA + HEAD_PAD:]
    cqn = _rms(cq, qn_ref[...]).astype(BF16)
    ckvn = _rms(ckv, kvn_ref[...]).astype(BF16)

    qt = _dot_nt(wqt_ref[...], cqn)
    kn = _dot(ckvn, wk_ref[...])
    vt = _dot_nt(wvt_ref[...], ckvn)
    k_rope = kr * cosk_ref[...] + kr_rot * sink_ref[...]
    cosq = cosq_ref[...]
    sinq = sinq_ref[...]
    scale = QK_HEAD ** -0.5 * math.log2(math.e)
    zeros = jnp.zeros((HEAD_PAD - QK_HEAD, qt.shape[1]), F32)
    ones = jnp.ones((V_ROWS - V_HEAD, qt.shape[1]), F32)
    for h in range(MLA_HEADS):
        qh = qt[h * HEAD_PAD:(h + 1) * HEAD_PAD, :]
        roped = qh[QK_NOPE:QK_HEAD, :] * cosq + qh[QK_HEAD:, :] * sinq
        qt_ref[0, h] = jnp.concatenate(
            [qh[0:QK_NOPE, :] * scale, roped * scale, zeros], axis=0).astype(BF16)
        k_ref[0, h, 0] = (kn[:, h * HEAD_PAD:(h + 1) * HEAD_PAD] + k_rope).astype(BF16)
        vt_ref[0, h, 0] = jnp.concatenate(
            [vt[h * V_HEAD:(h + 1) * V_HEAD, :], ones], axis=0).astype(BF16)


def _rot_cols(w):
    half = QK_ROPE // 2
    return jnp.concatenate([-w[..., half:], w[..., :half]], axis=-1)


def _mla_weights(w_in, w_uq, w_ukv):
    w_kr = w_in[:, Q_LORA + KV_LORA:]
    padl = jnp.zeros((D_MODEL, QK_NOPE), F32)
    padr = jnp.zeros((D_MODEL, HEAD_PAD - QK_HEAD), F32)
    win_ext = jnp.concatenate(
        [w_in[:, :Q_LORA + KV_LORA], padl, w_kr, padr, padl, _rot_cols(w_kr), padr],
        axis=-1).astype(BF16)
    wq = w_uq.reshape(Q_LORA, MLA_HEADS, QK_HEAD)
    wq_ext = jnp.concatenate([wq, _rot_cols(wq[..., QK_NOPE:])], axis=-1)
    wqt = wq_ext.reshape(Q_LORA, MLA_HEADS * HEAD_PAD).T.astype(BF16)
    wkv = w_ukv.reshape(KV_LORA, MLA_HEADS, QK_NOPE + V_HEAD)
    wk = jnp.concatenate(
        [wkv[..., :QK_NOPE], jnp.zeros((KV_LORA, MLA_HEADS, HEAD_PAD - QK_NOPE), F32)],
        axis=-1).reshape(KV_LORA, MLA_HEADS * HEAD_PAD).astype(BF16)
    wvt = wkv[..., QK_NOPE:].reshape(KV_LORA, MLA_HEADS * V_HEAD).T.astype(BF16)
    return win_ext, wqt, wk, wvt


def _rope_tables(first_pos, t_len):
    pos = first_pos + jnp.arange(t_len, dtype=F32)
    inv_freq = ROPE_BASE ** (-jnp.arange(0, QK_ROPE, 2, dtype=F32) / QK_ROPE)
    ang = _to_strided(pos[:, None] * inv_freq[None, :], 0)
    cos2 = jnp.concatenate([jnp.cos(ang)] * 2, axis=-1)
    sin2 = jnp.concatenate([jnp.sin(ang)] * 2, axis=-1)
    lpad = ((0, 0), (QK_NOPE, HEAD_PAD - QK_HEAD))
    return jnp.pad(cos2, lpad), jnp.pad(sin2, lpad), cos2.T, sin2.T


def _mla_proj(h, tables, g, q_norm, kv_norm, weights):
    bsz, tlen, _ = h.shape
    tile = TILE_T
    nt = tlen // tile
    win_ext, wqt, wk, wvt = weights
    args = (g.reshape(1, D_MODEL), win_ext, q_norm.reshape(1, Q_LORA),
            kv_norm.reshape(1, KV_LORA), wqt, wk, wvt)
    row_spec = pl.BlockSpec((1, tile, D_MODEL), lambda b, t: (b, t, 0))
    in_specs = [row_spec] + [_const_spec(a.shape) for a in args] + [
        pl.BlockSpec((tile, HEAD_PAD), lambda b, t: (t, 0)),
        pl.BlockSpec((tile, HEAD_PAD), lambda b, t: (t, 0)),
        pl.BlockSpec((QK_ROPE, tile), lambda b, t: (0, t)),
        pl.BlockSpec((QK_ROPE, tile), lambda b, t: (0, t)),
    ]
    out_shape = (
        jax.ShapeDtypeStruct((bsz, MLA_HEADS, HEAD_PAD, tlen), BF16),
        jax.ShapeDtypeStruct((bsz, MLA_HEADS, nt, tile, HEAD_PAD), BF16),
        jax.ShapeDtypeStruct((bsz, MLA_HEADS, nt, V_ROWS, tile), BF16),
    )
    out_specs = (
        pl.BlockSpec((1, MLA_HEADS, HEAD_PAD, tile), lambda b, t: (b, 0, 0, t)),
        pl.BlockSpec((1, MLA_HEADS, 1, tile, HEAD_PAD), lambda b, t: (b, 0, t, 0, 0)),
        pl.BlockSpec((1, MLA_HEADS, 1, V_ROWS, tile), lambda b, t: (b, 0, t, 0, 0)),
    )
    return pl.pallas_call(
        _mla_proj_kernel,
        grid=(bsz, nt),
        in_specs=in_specs,
        out_specs=out_specs,
        out_shape=out_shape,
        compiler_params=pltpu.CompilerParams(
            dimension_semantics=("arbitrary", "arbitrary"), vmem_limit_bytes=VMEM_LIMIT),
        name="mla_proj",
    )(h, *args, *tables)


def _strided_time(i):
    return (i & (SUBLANES - 1)) * GROUPS + (i >> 3)


def _attn_kernel(*refs, has_meta):
    if has_meta:
        qt_ref, k_ref, vt_ref, km_ref, vtm_ref, o_ref = refs
    else:
        qt_ref, k_ref, vt_ref, o_ref = refs
    tile = qt_ref.shape[3]
    qi = pl.program_id(2)
    neg = jnp.finfo(F32).min
    heads = range(HEADS_PER_STEP)

    def update(carry, scores, values):
        weights = []
        for hh in heads:
            m_new = carry[hh][0]
            for s in scores[hh]:
                m_new = jnp.maximum(m_new, jnp.max(s, axis=0, keepdims=True))
            weights.append((m_new, [jnp.exp2(s - m_new).astype(BF16) for s in scores[hh]]))
        out = []
        for hh in heads:
            m, acc = carry[hh]
            m_new, ps = weights[hh]
            p = ps[0] if len(ps) == 1 else jnp.concatenate(ps, axis=0)
            vt = values[hh][0] if len(ps) == 1 else jnp.concatenate(values[hh], axis=1)
            out.append((m_new, jnp.exp2(m - m_new) * acc + _dot(vt, p)))
        return tuple(out)

    def step(carry, tiles, mask_last):
        scores = [[_dot(k_ref[0, hh, j], qt_ref[0, hh]) for j in tiles] for hh in heads]
        if mask_last:
            kpos = _strided_time(lax.broadcasted_iota(jnp.int32, (tile, tile), 0))
            qpos = _strided_time(lax.broadcasted_iota(jnp.int32, (tile, tile), 1))
            keep = kpos <= qpos
            for hh in heads:
                scores[hh][-1] = jnp.where(keep, scores[hh][-1], neg)
        return update(carry, scores, [[vt_ref[0, hh, j] for j in tiles] for hh in heads])

    carry = tuple((jnp.full((1, tile), neg, F32), jnp.zeros((V_ROWS, tile), F32))
                  for _ in heads)
    if has_meta:
        carry = update(carry, [[_dot(km_ref[hh], qt_ref[0, hh])] for hh in heads],
                       [[vtm_ref[hh]] for hh in heads])
    carry = lax.fori_loop(
        0, qi // 2, lambda jj, c: step(c, [2 * jj, 2 * jj + 1], False), carry)
    carry = lax.cond(qi % 2 == 1,
                     lambda c: step(c, [qi - 1, qi], True),
                     lambda c: step(c, [qi], True), carry)
    outs = [acc[0:V_HEAD, :] / acc[V_HEAD:V_HEAD + 1, :] for _, acc in carry]
    o_ref[0] = jnp.concatenate(outs, axis=0).T.astype(o_ref.dtype)


def _attention(qt, k5, vt5, meta_kv=None):
    bsz, nh, _, tlen = qt.shape
    nt, tile = k5.shape[2], k5.shape[3]
    hps = HEADS_PER_STEP
    in_specs = [
        pl.BlockSpec((1, hps, HEAD_PAD, tile), lambda b, h, q: (b, h, 0, q)),
        pl.BlockSpec((1, hps, nt, tile, HEAD_PAD), lambda b, h, q: (b, h, 0, 0, 0)),
        pl.BlockSpec((1, hps, nt, V_ROWS, tile), lambda b, h, q: (b, h, 0, 0, 0)),
    ]
    args = (qt, k5, vt5)
    if meta_kv is not None:
        in_specs += [pl.BlockSpec((hps,) + a.shape[1:], lambda b, h, q: (h, 0, 0))
                     for a in meta_kv]
        args += tuple(meta_kv)
    return pl.pallas_call(
        functools.partial(_attn_kernel, has_meta=meta_kv is not None),
        grid=(bsz, nh // hps, nt),
        in_specs=in_specs,
        out_specs=pl.BlockSpec((1, tile, hps * V_HEAD), lambda b, h, q: (b, q, h)),
        out_shape=jax.ShapeDtypeStruct((bsz, tlen, nh * V_HEAD), BF16),
        compiler_params=pltpu.CompilerParams(
            dimension_semantics=("arbitrary", "arbitrary", "arbitrary"),
            vmem_limit_bytes=VMEM_LIMIT),
        name="mla_attention",
    )(*args)


def _to_strided(a, axis):
    shp = a.shape
    nt = shp[axis] // TILE_T
    a = a.reshape(shp[:axis] + (nt, SUBLANES, GROUPS) + shp[axis + 1:])
    return jnp.swapaxes(a, axis + 1, axis + 2).reshape(shp)


def _from_strided(a, axis):
    shp = a.shape
    nt = shp[axis] // TILE_T
    a = a.reshape(shp[:axis] + (nt, GROUPS, SUBLANES) + shp[axis + 1:])
    return jnp.swapaxes(a, axis + 1, axis + 2).reshape(shp)


def _trunk(h, first_pos, states, p, state_at):
    depth = len(p["ffn"])
    tables = _rope_tables(first_pos, h.shape[1])
    new_states = []
    for layer in range(depth):
        st = states[layer]
        ffn = p["ffn"][layer]
        attn = None
        if layer % 2 == 0:
            h, mix_state = _even_layer(h, st["mix"], *p["even"][layer // 2], state_at=state_at)
        else:
            g, q_norm, kv_norm, weights, w_out = p["odd"][layer // 2]
            qt, k5, vt5 = _mla_proj(h, tables, g, q_norm, kv_norm, weights)
            attn = (_attention(qt, k5, vt5, st["mix"]), w_out)
            mix_state = None
            if state_at is not None:
                n = state_at + 1
                assert n <= GROUPS
                mix_state = (k5[0, :, 0, 0:n * SUBLANES:SUBLANES, :],
                             vt5[0, :, 0, :, 0:n * SUBLANES:SUBLANES])
        h, tail = _ffn_layer(h, st["ffn"], *ffn, attn=attn,
                             final_g=p["final"] if layer == depth - 1 else None,
                             state_at=state_at)
        new_states.append({"mix": mix_state, "ffn": tail})
    return h, new_states


def kernel(x, meta_tokens, ev_norm, ev_w_in, ev_conv_a, ev_conv_b, ev_conv_b_bias, ev_gate_r_w, ev_gate_r_b, ev_gate_i_w, ev_gate_i_b, ev_lru_lambda, ev_w_out, od_norm, od_w_in, od_q_norm, od_kv_norm, od_w_uq, od_w_ukv, od_w_out, ffn_norm, ffn_w_up, ffn_conv_w, ffn_conv_b, ffn_w_down, final_norm):
    bsz, seq, _ = x.shape
    depth = ffn_norm.shape[0]
    assert seq % TILE_T == 0 and N_META <= GROUPS
    params = {
        "even": [(ev_norm[j], ev_w_in[j].astype(BF16), ev_conv_a[j], ev_conv_b[j],
                  ev_conv_b_bias[j], _block_diag(ev_gate_r_w[j]).astype(BF16), ev_gate_r_b[j],
                  _block_diag(ev_gate_i_w[j]).astype(BF16), ev_gate_i_b[j], ev_lru_lambda[j],
                  ev_w_out[j].astype(BF16)) for j in range(ev_norm.shape[0])],
        "odd": [(od_norm[j], od_q_norm[j], od_kv_norm[j],
                 _mla_weights(od_w_in[j], od_w_uq[j], od_w_ukv[j]), od_w_out[j].astype(BF16))
                for j in range(od_norm.shape[0])],
        "ffn": [(ffn_norm[l], ffn_w_up[l].astype(BF16), ffn_conv_w[l], ffn_conv_b[l],
                 ffn_w_down[l].astype(BF16)) for l in range(depth)],
        "final": final_norm,
    }
    zero_states = []
    for layer in range(depth):
        mix = None
        if layer % 2 == 0:
            mix = (jnp.zeros(((ev_conv_a.shape[1] - 1) * SUBLANES, CONV_WIDTH), F32),
                   jnp.zeros(((ev_conv_b.shape[1] - 1) * SUBLANES, LRU_WIDTH), F32),
                   jnp.zeros((1, LRU_WIDTH), F32))
        zero_states.append(
            {"mix": mix, "ffn": jnp.zeros(((ffn_conv_w.shape[1] - 1) * SUBLANES, 2 * D_FF), F32)})

    meta_tile = jnp.concatenate(
        [meta_tokens.astype(x.dtype), jnp.zeros((TILE_T - N_META, D_MODEL), x.dtype)], axis=0)
    _, meta_states = _trunk(_to_strided(meta_tile[None], 1), 0.0, zero_states, params,
                            state_at=N_META - 1)
    h, _ = _trunk(_to_strided(x, 1), float(N_META), meta_states, params, state_at=None)
    return _from_strided(h, 1)
```

```python
import functools
import math

import jax
import jax.numpy as jnp
from jax import lax
from jax.experimental import pallas as pl
from jax.experimental.pallas import tpu as pltpu

D_MODEL = 1024
N_META = 16
EPS = 1e-6
CONV_WIDTH = 512
LRU_WIDTH = 512
LRU_C = 8.0
MLA_HEADS = 16
QK_NOPE = 64
QK_ROPE = 32
QK_HEAD = QK_NOPE + QK_ROPE
V_HEAD = 64
Q_LORA = 384
KV_LORA = 256
ROPE_BASE = 10000.0
D_FF = 2816

LANES = 128
SUBLANES = 8
HEAD_PAD = 128
V_ROWS = V_HEAD + 16
HEADS_PER_STEP = 8
SCORE_LEAD = 4
TILE_T = 256
GROUPS = TILE_T // SUBLANES
FF_CHUNK = 256
VMEM_LIMIT = 56 * 1024 * 1024

F32 = jnp.float32
BF16 = jnp.bfloat16


def _rms(x, g):
    ms = jnp.mean(x * x, axis=-1, keepdims=True)
    return x * lax.rsqrt(ms + EPS) * g


def _sigmoid(x):
    return 1.0 / (1.0 + jnp.exp(-x))


def _gelu_tanh(x):
    c = math.sqrt(2.0 / math.pi)
    return x * (0.5 * (1.0 + jnp.tanh(c * (x + 0.044715 * (x * x * x)))))


def _dot(a, b):
    return jnp.dot(a, b, preferred_element_type=F32)


def _dot_nt(a, b):
    return lax.dot_general(a, b, (((1,), (1,)), ((), ())), preferred_element_type=F32)


def _row_groups(x):
    return [x[v * SUBLANES:(v + 1) * SUBLANES, :] for v in range(x.shape[0] // SUBLANES)]


def _delays(u, prev, kmax):
    tile = u.shape[0]
    first = lax.broadcasted_iota(jnp.int32, (SUBLANES, u.shape[1]), 0) == 0
    wrapped = []
    for i in range(kmax):
        cur = u[tile - (kmax - i) * SUBLANES:tile - (kmax - i - 1) * SUBLANES, :]
        old = prev[i * SUBLANES:(i + 1) * SUBLANES, :]
        wrapped.append(jnp.where(first, pltpu.roll(old, 1, axis=0), pltpu.roll(cur, 1, axis=0)))
    return [jnp.concatenate(wrapped[kmax - k:] + [u[:tile - k * SUBLANES, :]], axis=0)
            for k in range(1, kmax + 1)]


def _causal_conv(u, prev, taps):
    k = taps.shape[0]
    delayed = _delays(u, prev, k - 1)
    acc = delayed[k - 2] * taps[0:1, :]
    for j in range(1, k - 1):
        acc = acc + delayed[k - 2 - j] * taps[j:j + 1, :]
    return acc + u * taps[k - 1:k, :]


def _conv_tail(u, kmax, state_at):
    tile = u.shape[0]
    if state_at is None:
        return u[tile - kmax * SUBLANES:, :]
    s, v = divmod(state_at, GROUPS)
    assert v >= kmax - 1
    rows = []
    for i in range(kmax):
        grp = u[(v - (kmax - 1 - i)) * SUBLANES:(v - (kmax - 2 - i)) * SUBLANES, :]
        rows.append(pltpu.roll(grp, SUBLANES - 1 - s, axis=0) if s != SUBLANES - 1 else grp)
    return jnp.concatenate(rows, axis=0)


def _lru_scan(a, b, carry, state_at):
    a_rows, b_rows = _row_groups(a), _row_groups(b)
    prod, hzero = [a_rows[0]], [b_rows[0]]
    for v in range(1, len(a_rows)):
        prod.append(a_rows[v] * prod[-1])
        hzero.append(a_rows[v] * hzero[-1] + b_rows[v])
    pa, ph = prod[-1], hzero[-1]
    sub = lax.broadcasted_iota(jnp.int32, pa.shape, 0)
    for d in (1, 2, 4):
        ok = sub >= d
        ph = jnp.where(ok, pa * pltpu.roll(ph, d, axis=0) + ph, ph)
        pa = jnp.where(ok, pa * pltpu.roll(pa, d, axis=0), pa)
    ends = ph + pa * carry
    init = jnp.where(sub == 0, carry, pltpu.roll(ends, 1, axis=0))
    h = jnp.concatenate([hz + pr * init for hz, pr in zip(hzero, prod)], axis=0)
    if state_at is None:
        return h, ends[SUBLANES - 1:SUBLANES, :]
    s, v = divmod(state_at, GROUPS)
    row = v * SUBLANES + s
    return h, h[row:row + 1, :]


def _even_kernel(*refs, state_at):
    (x_ref, zin_ref, xbin_ref, hin_ref, g_ref, win_ref, ca_ref, cb_ref, cbb_ref, rw_ref, rb_ref,
     iw_ref, ib_ref, lam_ref, wout_ref, o_ref) = refs[:16]
    ztail, xbtail, hstate = refs[-3:]
    t = pl.program_id(1)

    @pl.when(t == 0)
    def _():
        ztail[...] = zin_ref[...]
        xbtail[...] = xbin_ref[...]
        hstate[...] = hin_ref[...]

    hn = _rms(x_ref[0], g_ref[...]).astype(BF16)
    u = _dot(hn, win_ref[...])
    cw = CONV_WIDTH
    gb = u[:, 0:cw]
    gc = u[:, cw:2 * cw]
    xa = u[:, 2 * cw:3 * cw]
    xb = u[:, 3 * cw:3 * cw + LRU_WIDTH]
    gate = u[:, 3 * cw + LRU_WIDTH:]

    z = gc * xa
    y_a = gb * _causal_conv(z, ztail[...], ca_ref[...])
    ztail[...] = _conv_tail(z, ca_ref.shape[0] - 1, state_at)

    xc = _causal_conv(xb, xbtail[...], cb_ref[...]) + cbb_ref[...]
    xbtail[...] = _conv_tail(xb, cb_ref.shape[0] - 1, state_at)

    xcb = xc.astype(BF16)
    r = _sigmoid(_dot(xcb, rw_ref[...]) + rb_ref[...])
    i = _sigmoid(_dot(xcb, iw_ref[...]) + ib_ref[...])
    nlam = -lam_ref[...]
    softplus = jnp.maximum(nlam, 0.0) + jnp.log1p(jnp.exp(-jnp.abs(nlam)))
    log_a = -LRU_C * r * softplus
    a = jnp.exp(log_a)
    th = jnp.tanh(log_a)
    mult = jnp.sqrt(-2.0 * th / (1.0 - th))
    h, hstate[...] = _lru_scan(a, mult * (i * xc), hstate[...], state_at)

    y_b = _gelu_tanh(gate) * h
    y = jnp.concatenate([y_a, y_b], axis=-1).astype(BF16)
    o_ref[0] = x_ref[0] + _dot(y, wout_ref[...])
    if state_at is not None:
        zout_ref, xbout_ref, hout_ref = refs[16:19]
        zout_ref[...] = ztail[...]
        xbout_ref[...] = xbtail[...]
        hout_ref[...] = hstate[...]


def _const_spec(shape):
    nd = len(shape)
    return pl.BlockSpec(shape, lambda b, t: (0,) * nd)


def _shapes(arrays):
    return tuple(jax.ShapeDtypeStruct(a.shape, a.dtype) for a in arrays)


def _even_layer(h, state, g, w_in, conv_a, conv_b, conv_b_bias, rw, rb, iw, ib, lam, w_out,
                state_at=None):
    bsz, tlen, _ = h.shape
    tile = TILE_T
    row_spec = pl.BlockSpec((1, tile, D_MODEL), lambda b, t: (b, t, 0))
    args = tuple(state) + (
        g.reshape(1, D_MODEL), w_in, conv_a, conv_b,
        conv_b_bias.reshape(1, LRU_WIDTH), rw, rb.reshape(1, LRU_WIDTH), iw,
        ib.reshape(1, LRU_WIDTH), lam.reshape(1, LRU_WIDTH), w_out)
    out_shape = [jax.ShapeDtypeStruct(h.shape, F32)]
    out_specs = [row_spec]
    if state_at is not None:
        assert bsz == 1 and tlen == tile
        out_shape += list(_shapes(state))
        out_specs += [_const_spec(s.shape) for s in state]
    res = pl.pallas_call(
        functools.partial(_even_kernel, state_at=state_at),
        grid=(bsz, tlen // tile),
        in_specs=[row_spec] + [_const_spec(a.shape) for a in args],
        out_specs=out_specs,
        out_shape=out_shape,
        scratch_shapes=[pltpu.VMEM(s.shape, F32) for s in state],
        compiler_params=pltpu.CompilerParams(
            dimension_semantics=("arbitrary", "arbitrary"), vmem_limit_bytes=VMEM_LIMIT),
        name="even_mixer",
    )(h, *args)
    return res[0], (tuple(res[1:]) if state_at is not None else None)


def _block_diag(w):
    nh, d, _ = w.shape
    eye = jnp.eye(nh, dtype=w.dtype)
    return jnp.einsum("hij,hg->higj", w, eye).reshape(nh * d, nh * d)


def _ffn_kernel(*refs, attn_input, final_norm, state_at):
    refs = list(refs)
    x_ref, tin_ref = refs.pop(0), refs.pop(0)
    attn_ref, wo_ref = (refs.pop(0), refs.pop(0)) if attn_input else (None, None)
    g_ref, wup_ref, cw_ref, cb_ref, wdn_ref = refs[:5]
    del refs[:5]
    fg_ref = refs.pop(0) if final_norm else None
    o_ref = refs.pop(0)
    tout_ref = refs.pop(0) if state_at is not None else None
    hn_ref, act_ref, tail = refs
    t = pl.program_id(1)

    @pl.when(t == 0)
    def _():
        tail[...] = tin_ref[...]

    if attn_input:
        o_ref[0] = x_ref[0] + _dot(attn_ref[0], wo_ref[...])
    else:
        o_ref[0] = x_ref[0]
    hn_ref[...] = _rms(o_ref[0], g_ref[...]).astype(BF16)
    for c in range(D_FF // FF_CHUNK):
        halves = []
        for part in range(2):
            col = part * D_FF + c * FF_CHUNK
            cs = slice(col, col + FF_CHUNK)
            u = _dot(hn_ref[...], wup_ref[:, cs])
            halves.append(_causal_conv(u, tail[:, cs], cw_ref[:, cs]) + cb_ref[:, cs])
            tail[:, cs] = _conv_tail(u, cw_ref.shape[0] - 1, state_at)
        a, gte = halves
        act_ref[:, c * FF_CHUNK:(c + 1) * FF_CHUNK] = (a * _sigmoid(a) * gte).astype(BF16)
    y = o_ref[0] + _dot(act_ref[...], wdn_ref[...])
    o_ref[0] = _rms(y, fg_ref[...]) if final_norm else y
    if state_at is not None:
        tout_ref[...] = tail[...]


def _ffn_layer(h, tail, g, w_up, conv_w, conv_b, w_down, attn=None, final_g=None, state_at=None):
    bsz, tlen, _ = h.shape
    tile = TILE_T
    row_spec = pl.BlockSpec((1, tile, D_MODEL), lambda b, t: (b, t, 0))
    args = (g.reshape(1, D_MODEL), w_up, conv_w, conv_b.reshape(1, 2 * D_FF), w_down)
    if final_g is not None:
        args = args + (final_g.reshape(1, D_MODEL),)
    specs = [_const_spec(a.shape) for a in args]
    if attn is not None:
        args = attn + args
        specs = [row_spec, _const_spec(attn[1].shape)] + specs
    out_shape = [jax.ShapeDtypeStruct(h.shape, F32)]
    out_specs = [row_spec]
    if state_at is not None:
        assert bsz == 1 and tlen == tile
        out_shape.append(jax.ShapeDtypeStruct(tail.shape, F32))
        out_specs.append(_const_spec(tail.shape))
    res = pl.pallas_call(
        functools.partial(_ffn_kernel, attn_input=attn is not None,
                          final_norm=final_g is not None, state_at=state_at),
        grid=(bsz, tlen // tile),
        in_specs=[row_spec, _const_spec(tail.shape)] + specs,
        out_specs=out_specs,
        out_shape=out_shape,
        scratch_shapes=[
            pltpu.VMEM((tile, D_MODEL), BF16),
            pltpu.VMEM((tile, D_FF), BF16),
            pltpu.VMEM(tail.shape, F32),
        ],
        compiler_params=pltpu.CompilerParams(
            dimension_semantics=("arbitrary", "arbitrary"), vmem_limit_bytes=VMEM_LIMIT),
        name="conv_ffn",
    )(h, tail, *args)
    return res[0], (res[1] if state_at is not None else None)


def _mla_proj_kernel(x_ref, g_ref, win_ref, qn_ref, kvn_ref, wqt_ref, wk_ref, wvt_ref,
                     cosk_ref, sink_ref, cosq_ref, sinq_ref, qt_ref, k_ref, vt_ref):
    x = x_ref[0]
    hn = _rms(x, g_ref[...]).astype(BF16)
    u = _dot(hn, win_ref[...])
    cq = u[:, 0:Q_LORA]
    ckv = u[:, Q_LORA:Q_LORA + KV_LORA]
    kr = u[:, Q_LORA + KV_LORA:Q_LORA + KV_LORA + HEAD_PAD]
    kr_rot = u[:, Q_LORA + KV_LORA + HEAD_PAD:]
    cqn = _rms(cq, qn_ref[...]).astype(BF16)
    ckvn = _rms(ckv, kvn_ref[...]).astype(BF16)

    qt = _dot_nt(wqt_ref[...], cqn)
    kn = _dot(ckvn, wk_ref[...])
    vt = _dot_nt(wvt_ref[...], ckvn)
    k_rope = kr * cosk_ref[...] + kr_rot * sink_ref[...]
    cosq = cosq_ref[...]
    sinq = sinq_ref[...]
    scale = QK_HEAD ** -0.5 * math.log2(math.e)
    zeros = jnp.zeros((HEAD_PAD - QK_HEAD, qt.shape[1]), F32)
    ones = jnp.ones((V_ROWS - V_HEAD, qt.shape[1]), F32)
    for h in range(MLA_HEADS):
        qh = qt[h * HEAD_PAD:(h + 1) * HEAD_PAD, :]
        roped = qh[QK_NOPE:QK_HEAD, :] * cosq + qh[QK_HEAD:, :] * sinq
        qt_ref[0, h] = jnp.concatenate(
            [qh[0:QK_NOPE, :] * scale, roped * scale, zeros], axis=0).astype(BF16)
        k_ref[0, h, 0] = (kn[:, h * HEAD_PAD:(h + 1) * HEAD_PAD] + k_rope).astype(BF16)
        vt_ref[0, h, 0] = jnp.concatenate(
            [vt[h * V_HEAD:(h + 1) * V_HEAD, :], ones], axis=0).astype(BF16)


def _rot_cols(w):
    half = QK_ROPE // 2
    return jnp.concatenate([-w[..., half:], w[..., :half]], axis=-1)


def _mla_weights(w_in, w_uq, w_ukv):
    w_kr = w_in[:, Q_LORA + KV_LORA:]
    padl = jnp.zeros((D_MODEL, QK_NOPE), F32)
    padr = jnp.zeros((D_MODEL, HEAD_PAD - QK_HEAD), F32)
    win_ext = jnp.concatenate(
        [w_in[:, :Q_LORA + KV_LORA], padl, w_kr, padr, padl, _rot_cols(w_kr), padr],
        axis=-1).astype(BF16)
    wq = w_uq.reshape(Q_LORA, MLA_HEADS, QK_HEAD)
    wq_ext = jnp.concatenate([wq, _rot_cols(wq[..., QK_NOPE:])], axis=-1)
    wqt = wq_ext.reshape(Q_LORA, MLA_HEADS * HEAD_PAD).T.astype(BF16)
    wkv = w_ukv.reshape(KV_LORA, MLA_HEADS, QK_NOPE + V_HEAD)
    wk = jnp.concatenate(
        [wkv[..., :QK_NOPE], jnp.zeros((KV_LORA, MLA_HEADS, HEAD_PAD - QK_NOPE), F32)],
        axis=-1).reshape(KV_LORA, MLA_HEADS * HEAD_PAD).astype(BF16)
    wvt = wkv[..., QK_NOPE:].reshape(KV_LORA, MLA_HEADS * V_HEAD).T.astype(BF16)
    return win_ext, wqt, wk, wvt


def _rope_tables(first_pos, t_len):
    pos = first_pos + jnp.arange(t_len, dtype=F32)
    inv_freq = ROPE_BASE ** (-jnp.arange(0, QK_ROPE, 2, dtype=F32) / QK_ROPE)
    ang = _to_strided(pos[:, None] * inv_freq[None, :], 0)
    cos2 = jnp.concatenate([jnp.cos(ang)] * 2, axis=-1)
    sin2 = jnp.concatenate([jnp.sin(ang)] * 2, axis=-1)
    lpad = ((0, 0), (QK_NOPE, HEAD_PAD - QK_HEAD))
    return jnp.pad(cos2, lpad), jnp.pad(sin2, lpad), cos2.T, sin2.T


def _mla_proj(h, tables, g, q_norm, kv_norm, weights):
    bsz, tlen, _ = h.shape
    tile = TILE_T
    nt = tlen // tile
    win_ext, wqt, wk, wvt = weights
    args = (g.reshape(1, D_MODEL), win_ext, q_norm.reshape(1, Q_LORA),
            kv_norm.reshape(1, KV_LORA), wqt, wk, wvt)
    row_spec = pl.BlockSpec((1, tile, D_MODEL), lambda b, t: (b, t, 0))
    in_specs = [row_spec] + [_const_spec(a.shape) for a in args] + [
        pl.BlockSpec((tile, HEAD_PAD), lambda b, t: (t, 0)),
        pl.BlockSpec((tile, HEAD_PAD), lambda b, t: (t, 0)),
        pl.BlockSpec((QK_ROPE, tile), lambda b, t: (0, t)),
        pl.BlockSpec((QK_ROPE, tile), lambda b, t: (0, t)),
    ]
    out_shape = (
        jax.ShapeDtypeStruct((bsz, MLA_HEADS, HEAD_PAD, tlen), BF16),
        jax.ShapeDtypeStruct((bsz, MLA_HEADS, nt, tile, HEAD_PAD), BF16),
        jax.ShapeDtypeStruct((bsz, MLA_HEADS, nt, V_ROWS, tile), BF16),
    )
    out_specs = (
        pl.BlockSpec((1, MLA_HEADS, HEAD_PAD, tile), lambda b, t: (b, 0, 0, t)),
        pl.BlockSpec((1, MLA_HEADS, 1, tile, HEAD_PAD), lambda b, t: (b, 0, t, 0, 0)),
        pl.BlockSpec((1, MLA_HEADS, 1, V_ROWS, tile), lambda b, t: (b, 0, t, 0, 0)),
    )
    return pl.pallas_call(
        _mla_proj_kernel,
        grid=(bsz, nt),
        in_specs=in_specs,
        out_specs=out_specs,
        out_shape=out_shape,
        compiler_params=pltpu.CompilerParams(
            dimension_semantics=("arbitrary", "arbitrary"), vmem_limit_bytes=VMEM_LIMIT),
        name="mla_proj",
    )(h, *args, *tables)


def _strided_time(i):
    return (i & (SUBLANES - 1)) * GROUPS + (i >> 3)


def _attn_kernel(*refs, has_meta):
    if has_meta:
        qt_ref, k_ref, vt_ref, km_ref, vtm_ref, o_ref, s_ref = refs
    else:
        qt_ref, k_ref, vt_ref, o_ref, s_ref = refs
    tile = qt_ref.shape[3]
    qi = pl.program_id(2)
    neg = jnp.finfo(F32).min
    heads = range(HEADS_PER_STEP)

    def softmax_update(state, scores, vt, keep=None):
        def block():
            s = scores()
            return s if keep is None else jnp.where(keep, s, neg)
        m, acc = state
        m_new = jnp.maximum(m, jnp.max(block(), axis=0, keepdims=True))
        p = jnp.exp2(block() - m_new).astype(BF16)
        return m_new, jnp.exp2(m - m_new) * acc + _dot(vt, p)

    def produce(slot, j, hh):
        s_ref[slot, hh] = _dot(k_ref[0, hh, j], qt_ref[0, hh])

    def consume(carry, slot, j, next_j, keep=None):
        if next_j is not None:
            for hh in range(SCORE_LEAD):
                produce(1 - slot, next_j, hh)
        out = []
        for hh in heads:
            out.append(softmax_update(
                carry[hh], lambda: s_ref[slot, hh], vt_ref[0, hh, j], keep))
            if next_j is not None and hh + SCORE_LEAD < HEADS_PER_STEP:
                produce(1 - slot, next_j, hh + SCORE_LEAD)
        return tuple(out)

    carry = tuple((jnp.full((1, tile), neg, F32), jnp.zeros((V_ROWS, tile), F32))
                  for _ in heads)
    if has_meta:
        meta_scores = [_dot(km_ref[hh], qt_ref[0, hh]) for hh in heads]
    for hh in heads:
        produce(0, 0, hh)
    if has_meta:
        carry = tuple(softmax_update(carry[hh], lambda: meta_scores[hh], vtm_ref[hh])
                      for hh in heads)

    def pair(jj, c):
        c = consume(c, 0, 2 * jj, 2 * jj + 1)
        return consume(c, 1, 2 * jj + 1, 2 * jj + 2)

    carry = lax.fori_loop(0, qi // 2, pair, carry)
    kpos = _strided_time(lax.broadcasted_iota(jnp.int32, (tile, tile), 0))
    qpos = _strided_time(lax.broadcasted_iota(jnp.int32, (tile, tile), 1))
    keep = kpos <= qpos
    carry = lax.cond(
        qi % 2 == 1,
        lambda c: consume(consume(c, 0, qi - 1, qi), 1, qi, None, keep),
        lambda c: consume(c, 0, qi, None, keep), carry)
    outs = [acc[0:V_HEAD, :] / acc[V_HEAD:V_HEAD + 1, :] for _, acc in carry]
    o_ref[0] = jnp.concatenate(outs, axis=0).T.astype(o_ref.dtype)


def _attention(qt, k5, vt5, meta_kv=None):
    bsz, nh, _, tlen = qt.shape
    nt, tile = k5.shape[2], k5.shape[3]
    hps = HEADS_PER_STEP
    in_specs = [
        pl.BlockSpec((1, hps, HEAD_PAD, tile), lambda b, h, q: (b, h, 0, q)),
        pl.BlockSpec((1, hps, nt, tile, HEAD_PAD), lambda b, h, q: (b, h, 0, 0, 0)),
        pl.BlockSpec((1, hps, nt, V_ROWS, tile), lambda b, h, q: (b, h, 0, 0, 0)),
    ]
    args = (qt, k5, vt5)
    if meta_kv is not None:
        in_specs += [pl.BlockSpec((hps,) + a.shape[1:], lambda b, h, q: (h, 0, 0))
                     for a in meta_kv]
        args += tuple(meta_kv)
    return pl.pallas_call(
        functools.partial(_attn_kernel, has_meta=meta_kv is not None),
        grid=(bsz, nh // hps, nt),
        in_specs=in_specs,
        out_specs=pl.BlockSpec((1, tile, hps * V_HEAD), lambda b, h, q: (b, q, h)),
        out_shape=jax.ShapeDtypeStruct((bsz, tlen, nh * V_HEAD), BF16),
        scratch_shapes=[pltpu.VMEM((2, hps, tile, tile), F32)],
        compiler_params=pltpu.CompilerParams(
            dimension_semantics=("arbitrary", "arbitrary", "arbitrary"),
            vmem_limit_bytes=VMEM_LIMIT),
        name="mla_attention",
    )(*args)


def _to_strided(a, axis):
    shp = a.shape
    nt = shp[axis] // TILE_T
    a = a.reshape(shp[:axis] + (nt, SUBLANES, GROUPS) + shp[axis + 1:])
    return jnp.swapaxes(a, axis + 1, axis + 2).reshape(shp)


def _from_strided(a, axis):
    shp = a.shape
    nt = shp[axis] // TILE_T
    a = a.reshape(shp[:axis] + (nt, GROUPS, SUBLANES) + shp[axis + 1:])
    return jnp.swapaxes(a, axis + 1, axis + 2).reshape(shp)


def _trunk(h, first_pos, states, p, state_at):
    depth = len(p["ffn"])
    tables = _rope_tables(first_pos, h.shape[1])
    new_states = []
    for layer in range(depth):
        st = states[layer]
        ffn = p["ffn"][layer]
        attn = None
        if layer % 2 == 0:
            h, mix_state = _even_layer(h, st["mix"], *p["even"][layer // 2], state_at=state_at)
        else:
            g, q_norm, kv_norm, weights, w_out = p["odd"][layer // 2]
            qt, k5, vt5 = _mla_proj(h, tables, g, q_norm, kv_norm, weights)
            attn = (_attention(qt, k5, vt5, st["mix"]), w_out)
            mix_state = None
            if state_at is not None:
                n = state_at + 1
                assert n <= GROUPS
                mix_state = (k5[0, :, 0, 0:n * SUBLANES:SUBLANES, :],
                             vt5[0, :, 0, :, 0:n * SUBLANES:SUBLANES])
        h, tail = _ffn_layer(h, st["ffn"], *ffn, attn=attn,
                             final_g=p["final"] if layer == depth - 1 else None,
                             state_at=state_at)
        new_states.append({"mix": mix_state, "ffn": tail})
    return h, new_states


def kernel(x, meta_tokens, ev_norm, ev_w_in, ev_conv_a, ev_conv_b, ev_conv_b_bias, ev_gate_r_w, ev_gate_r_b, ev_gate_i_w, ev_gate_i_b, ev_lru_lambda, ev_w_out, od_norm, od_w_in, od_q_norm, od_kv_norm, od_w_uq, od_w_ukv, od_w_out, ffn_norm, ffn_w_up, ffn_conv_w, ffn_conv_b, ffn_w_down, final_norm):
    bsz, seq, _ = x.shape
    depth = ffn_norm.shape[0]
    assert seq % TILE_T == 0 and N_META <= GROUPS
    params = {
        "even": [(ev_norm[j], ev_w_in[j].astype(BF16), ev_conv_a[j], ev_conv_b[j],
                  ev_conv_b_bias[j], _block_diag(ev_gate_r_w[j]).astype(BF16), ev_gate_r_b[j],
                  _block_diag(ev_gate_i_w[j]).astype(BF16), ev_gate_i_b[j], ev_lru_lambda[j],
                  ev_w_out[j].astype(BF16)) for j in range(ev_norm.shape[0])],
        "odd": [(od_norm[j], od_q_norm[j], od_kv_norm[j],
                 _mla_weights(od_w_in[j], od_w_uq[j], od_w_ukv[j]), od_w_out[j].astype(BF16))
                for j in range(od_norm.shape[0])],
        "ffn": [(ffn_norm[l], ffn_w_up[l].astype(BF16), ffn_conv_w[l], ffn_conv_b[l],
                 ffn_w_down[l].astype(BF16)) for l in range(depth)],
        "final": final_norm,
    }
    zero_states = []
    for layer in range(depth):
        mix = None
        if layer % 2 == 0:
            mix = (jnp.zeros(((ev_conv_a.shape[1] - 1) * SUBLANES, CONV_WIDTH), F32),
                   jnp.zeros(((ev_conv_b.shape[1] - 1) * SUBLANES, LRU_WIDTH), F32),
                   jnp.zeros((1, LRU_WIDTH), F32))
        zero_states.append(
            {"mix": mix, "ffn": jnp.zeros(((ffn_conv_w.shape[1] - 1) * SUBLANES, 2 * D_FF), F32)})

    meta_tile = jnp.concatenate(
        [meta_tokens.astype(x.dtype), jnp.zeros((TILE_T - N_META, D_MODEL), x.dtype)], axis=0)
    _, meta_states = _trunk(_to_strided(meta_tile[None], 1), 0.0, zero_states, params,
                            state_at=N_META - 1)
    h, _ = _trunk(_to_strided(x, 1), float(N_META), meta_states, params, state_at=None)
    return _from_strided(h, 1)
```

```python
import functools
import math

import jax
import jax.numpy as jnp
from jax import lax
from jax.experimental import pallas as pl
from jax.experimental.pallas import tpu as pltpu

D_MODEL = 1024
N_META = 16
EPS = 1e-6
CONV_WIDTH = 512
LRU_WIDTH = 512
LRU_C = 8.0
MLA_HEADS = 16
QK_NOPE = 64
QK_ROPE = 32
QK_HEAD = QK_NOPE + QK_ROPE
V_HEAD = 64
Q_LORA = 384
KV_LORA = 256
ROPE_BASE = 10000.0
D_FF = 2816

LANES = 128
SUBLANES = 8
HEAD_PAD = 128
V_ROWS = V_HEAD + 16
HEADS_PER_STEP = 8
SCORE_LEAD = 4
TILE_T = 256
GROUPS = TILE_T // SUBLANES
FF_CHUNK = 256
VMEM_LIMIT = 56 * 1024 * 1024

F32 = jnp.float32
BF16 = jnp.bfloat16


def _rms(x, g):
    ms = jnp.mean(x * x, axis=-1, keepdims=True)
    return x * lax.rsqrt(ms + EPS) * g


def _sigmoid(x):
    return 1.0 / (1.0 + jnp.exp(-x))


def _gelu_tanh(x):
    c = math.sqrt(2.0 / math.pi)
    return x * (0.5 * (1.0 + jnp.tanh(c * (x + 0.044715 * (x * x * x)))))


def _dot(a, b):
    return jnp.dot(a, b, preferred_element_type=F32)


def _dot_nt(a, b):
    return lax.dot_general(a, b, (((1,), (1,)), ((), ())), preferred_element_type=F32)


def _row_groups(x):
    return [x[v * SUBLANES:(v + 1) * SUBLANES, :] for v in range(x.shape[0] // SUBLANES)]


def _delays(u, prev, kmax):
    tile = u.shape[0]
    first = lax.broadcasted_iota(jnp.int32, (SUBLANES, u.shape[1]), 0) == 0
    wrapped = []
    for i in range(kmax):
        cur = u[tile - (kmax - i) * SUBLANES:tile - (kmax - i - 1) * SUBLANES, :]
        old = prev[i * SUBLANES:(i + 1) * SUBLANES, :]
        wrapped.append(jnp.where(first, pltpu.roll(old, 1, axis=0), pltpu.roll(cur, 1, axis=0)))
    return [jnp.concatenate(wrapped[kmax - k:] + [u[:tile - k * SUBLANES, :]], axis=0)
            for k in range(1, kmax + 1)]


def _causal_conv(u, prev, taps):
    k = taps.shape[0]
    delayed = _delays(u, prev, k - 1)
    acc = delayed[k - 2] * taps[0:1, :]
    for j in range(1, k - 1):
        acc = acc + delayed[k - 2 - j] * taps[j:j + 1, :]
    return acc + u * taps[k - 1:k, :]


def _conv_tail(u, kmax, state_at):
    tile = u.shape[0]
    if state_at is None:
        return u[tile - kmax * SUBLANES:, :]
    s, v = divmod(state_at, GROUPS)
    assert v >= kmax - 1
    rows = []
    for i in range(kmax):
        grp = u[(v - (kmax - 1 - i)) * SUBLANES:(v - (kmax - 2 - i)) * SUBLANES, :]
        rows.append(pltpu.roll(grp, SUBLANES - 1 - s, axis=0) if s != SUBLANES - 1 else grp)
    return jnp.concatenate(rows, axis=0)


def _lru_scan(a, b, carry, state_at):
    a_rows, b_rows = _row_groups(a), _row_groups(b)
    prod, hzero = [a_rows[0]], [b_rows[0]]
    for v in range(1, len(a_rows)):
        prod.append(a_rows[v] * prod[-1])
        hzero.append(a_rows[v] * hzero[-1] + b_rows[v])
    pa, ph = prod[-1], hzero[-1]
    sub = lax.broadcasted_iota(jnp.int32, pa.shape, 0)
    for d in (1, 2, 4):
        ok = sub >= d
        ph = jnp.where(ok, pa * pltpu.roll(ph, d, axis=0) + ph, ph)
        pa = jnp.where(ok, pa * pltpu.roll(pa, d, axis=0), pa)
    ends = ph + pa * carry
    init = jnp.where(sub == 0, carry, pltpu.roll(ends, 1, axis=0))
    h = jnp.concatenate([hz + pr * init for hz, pr in zip(hzero, prod)], axis=0)
    if state_at is None:
        return h, ends[SUBLANES - 1:SUBLANES, :]
    s, v = divmod(state_at, GROUPS)
    row = v * SUBLANES + s
    return h, h[row:row + 1, :]


def _even_kernel(*refs, state_at):
    (x_ref, zin_ref, xbin_ref, hin_ref, g_ref, win_ref, ca_ref, cb_ref, cbb_ref, rw_ref, rb_ref,
     iw_ref, ib_ref, lam_ref, wout_ref, o_ref) = refs[:16]
    ztail, xbtail, hstate = refs[-3:]
    t = pl.program_id(1)

    @pl.when(t == 0)
    def _():
        ztail[...] = zin_ref[...]
        xbtail[...] = xbin_ref[...]
        hstate[...] = hin_ref[...]

    hn = _rms(x_ref[0], g_ref[...]).astype(BF16)
    u = _dot(hn, win_ref[...])
    cw = CONV_WIDTH
    gb = u[:, 0:cw]
    gc = u[:, cw:2 * cw]
    xa = u[:, 2 * cw:3 * cw]
    xb = u[:, 3 * cw:3 * cw + LRU_WIDTH]
    gate = u[:, 3 * cw + LRU_WIDTH:]

    z = gc * xa
    y_a = gb * _causal_conv(z, ztail[...], ca_ref[...])
    ztail[...] = _conv_tail(z, ca_ref.shape[0] - 1, state_at)

    xc = _causal_conv(xb, xbtail[...], cb_ref[...]) + cbb_ref[...]
    xbtail[...] = _conv_tail(xb, cb_ref.shape[0] - 1, state_at)

    xcb = xc.astype(BF16)
    r = _sigmoid(_dot(xcb, rw_ref[...]) + rb_ref[...])
    i = _sigmoid(_dot(xcb, iw_ref[...]) + ib_ref[...])
    nlam = -lam_ref[...]
    softplus = jnp.maximum(nlam, 0.0) + jnp.log1p(jnp.exp(-jnp.abs(nlam)))
    log_a = -LRU_C * r * softplus
    a = jnp.exp(log_a)
    th = jnp.tanh(log_a)
    mult = jnp.sqrt(-2.0 * th / (1.0 - th))
    h, hstate[...] = _lru_scan(a, mult * (i * xc), hstate[...], state_at)

    y_b = _gelu_tanh(gate) * h
    y = jnp.concatenate([y_a, y_b], axis=-1).astype(BF16)
    o_ref[0] = x_ref[0] + _dot(y, wout_ref[...])
    if state_at is not None:
        zout_ref, xbout_ref, hout_ref = refs[16:19]
        zout_ref[...] = ztail[...]
        xbout_ref[...] = xbtail[...]
        hout_ref[...] = hstate[...]


def _const_spec(shape):
    nd = len(shape)
    return pl.BlockSpec(shape, lambda b, t: (0,) * nd, pipeline_mode=pl.Buffered(1))


def _shapes(arrays):
    return tuple(jax.ShapeDtypeStruct(a.shape, a.dtype) for a in arrays)


def _even_layer(h, state, g, w_in, conv_a, conv_b, conv_b_bias, rw, rb, iw, ib, lam, w_out,
                state_at=None):
    bsz, tlen, _ = h.shape
    tile = TILE_T
    row_spec = pl.BlockSpec((1, tile, D_MODEL), lambda b, t: (b, t, 0))
    args = tuple(state) + (
        g.reshape(1, D_MODEL), w_in, conv_a, conv_b,
        conv_b_bias.reshape(1, LRU_WIDTH), rw, rb.reshape(1, LRU_WIDTH), iw,
        ib.reshape(1, LRU_WIDTH), lam.reshape(1, LRU_WIDTH), w_out)
    out_shape = [jax.ShapeDtypeStruct(h.shape, F32)]
    out_specs = [row_spec]
    if state_at is not None:
        assert bsz == 1 and tlen == tile
        out_shape += list(_shapes(state))
        out_specs += [_const_spec(s.shape) for s in state]
    res = pl.pallas_call(
        functools.partial(_even_kernel, state_at=state_at),
        grid=(bsz, tlen // tile),
        in_specs=[row_spec] + [_const_spec(a.shape) for a in args],
        out_specs=out_specs,
        out_shape=out_shape,
        scratch_shapes=[pltpu.VMEM(s.shape, F32) for s in state],
        compiler_params=pltpu.CompilerParams(
            dimension_semantics=("arbitrary", "arbitrary"), vmem_limit_bytes=VMEM_LIMIT),
        name="even_mixer",
    )(h, *args)
    return res[0], (tuple(res[1:]) if state_at is not None else None)


def _block_diag(w):
    nh, d, _ = w.shape
    eye = jnp.eye(nh, dtype=w.dtype)
    return jnp.einsum("hij,hg->higj", w, eye).reshape(nh * d, nh * d)


def _ffn_kernel(*refs, attn_input, final_norm, state_at):
    refs = list(refs)
    x_ref, tin_ref = refs.pop(0), refs.pop(0)
    attn_ref, wo_ref = (refs.pop(0), refs.pop(0)) if attn_input else (None, None)
    g_ref, wup_ref, cw_ref, cb_ref, wdn_ref = refs[:5]
    del refs[:5]
    fg_ref = refs.pop(0) if final_norm else None
    o_ref = refs.pop(0)
    tout_ref = refs.pop(0) if state_at is not None else None
    hn_ref, act_ref, tail = refs
    t = pl.program_id(1)

    @pl.when(t == 0)
    def _():
        tail[...] = tin_ref[...]

    if attn_input:
        wg = attn_ref.shape[3]
        y = x_ref[0]
        for grp in range(attn_ref.shape[1]):
            y = y + _dot(attn_ref[0, grp], wo_ref[grp * wg:(grp + 1) * wg, :])
        o_ref[0] = y
    else:
        o_ref[0] = x_ref[0]
    hn_ref[...] = _rms(o_ref[0], g_ref[...]).astype(BF16)
    for c in range(D_FF // FF_CHUNK):
        halves = []
        for part in range(2):
            col = part * D_FF + c * FF_CHUNK
            cs = slice(col, col + FF_CHUNK)
            u = _dot(hn_ref[...], wup_ref[:, cs])
            halves.append(_causal_conv(u, tail[:, cs], cw_ref[:, cs]) + cb_ref[:, cs])
            tail[:, cs] = _conv_tail(u, cw_ref.shape[0] - 1, state_at)
        a, gte = halves
        act_ref[:, c * FF_CHUNK:(c + 1) * FF_CHUNK] = (a * _sigmoid(a) * gte).astype(BF16)
    y = o_ref[0] + _dot(act_ref[...], wdn_ref[...])
    o_ref[0] = _rms(y, fg_ref[...]) if final_norm else y
    if state_at is not None:
        tout_ref[...] = tail[...]


def _ffn_layer(h, tail, g, w_up, conv_w, conv_b, w_down, attn=None, final_g=None, state_at=None):
    bsz, tlen, _ = h.shape
    tile = TILE_T
    row_spec = pl.BlockSpec((1, tile, D_MODEL), lambda b, t: (b, t, 0))
    args = (g.reshape(1, D_MODEL), w_up, conv_w, conv_b.reshape(1, 2 * D_FF), w_down)
    if final_g is not None:
        args = args + (final_g.reshape(1, D_MODEL),)
    specs = [_const_spec(a.shape) for a in args]
    if attn is not None:
        args = attn + args
        attn_spec = pl.BlockSpec((1, attn[0].shape[1], tile, attn[0].shape[3]),
                                 lambda b, t: (b, 0, t, 0))
        specs = [attn_spec, _const_spec(attn[1].shape)] + specs
    out_shape = [jax.ShapeDtypeStruct(h.shape, F32)]
    out_specs = [row_spec]
    if state_at is not None:
        assert bsz == 1 and tlen == tile
        out_shape.append(jax.ShapeDtypeStruct(tail.shape, F32))
        out_specs.append(_const_spec(tail.shape))
    res = pl.pallas_call(
        functools.partial(_ffn_kernel, attn_input=attn is not None,
                          final_norm=final_g is not None, state_at=state_at),
        grid=(bsz, tlen // tile),
        in_specs=[row_spec, _const_spec(tail.shape)] + specs,
        out_specs=out_specs,
        out_shape=out_shape,
        scratch_shapes=[
            pltpu.VMEM((tile, D_MODEL), BF16),
            pltpu.VMEM((tile, D_FF), BF16),
            pltpu.VMEM(tail.shape, F32),
        ],
        compiler_params=pltpu.CompilerParams(
            dimension_semantics=("arbitrary", "arbitrary"), vmem_limit_bytes=VMEM_LIMIT),
        name="conv_ffn",
    )(h, tail, *args)
    return res[0], (res[1] if state_at is not None else None)


def _mla_proj_kernel(x_ref, g_ref, win_ref, qn_ref, kvn_ref, wqt_ref, wk_ref, wvt_ref,
                     cosk_ref, sink_ref, cosq_ref, sinq_ref, qt_ref, k_ref, vt_ref):
    x = x_ref[0]
    hn = _rms(x, g_ref[...]).astype(BF16)
    u = _dot(hn, win_ref[...])
    cq = u[:, 0:Q_LORA]
    ckv = u[:, Q_LORA:Q_LORA + KV_LORA]
    kr = u[:, Q_LORA + KV_LORA:Q_LORA + KV_LORA + HEAD_PAD]
    kr_rot = u[:, Q_LORA + KV_LORA + HEAD_PAD:]
    cqn = _rms(cq, qn_ref[...]).astype(BF16)
    ckvn = _rms(ckv, kvn_ref[...]).astype(BF16)

    qt = _dot_nt(wqt_ref[...], cqn)
    kn = _dot(ckvn, wk_ref[...])
    vt = _dot_nt(wvt_ref[...], ckvn)
    k_rope = kr * cosk_ref[...] + kr_rot * sink_ref[...]
    cosq = cosq_ref[...]
    sinq = sinq_ref[...]
    scale = QK_HEAD ** -0.5 * math.log2(math.e)
    zeros = jnp.zeros((HEAD_PAD - QK_HEAD, qt.shape[1]), F32)
    ones = jnp.ones((V_ROWS - V_HEAD, qt.shape[1]), F32)
    for h in range(MLA_HEADS):
        qh = qt[h * HEAD_PAD:(h + 1) * HEAD_PAD, :]
        roped = qh[QK_NOPE:QK_HEAD, :] * cosq + qh[QK_HEAD:, :] * sinq
        qt_ref[0, h, 0] = jnp.concatenate(
            [qh[0:QK_NOPE, :] * scale, roped * scale, zeros], axis=0).astype(BF16)
        k_ref[0, h, 0] = (kn[:, h * HEAD_PAD:(h + 1) * HEAD_PAD] + k_rope).astype(BF16)
        vt_ref[0, h, 0] = jnp.concatenate(
            [vt[h * V_HEAD:(h + 1) * V_HEAD, :], ones], axis=0).astype(BF16)


def _rot_cols(w):
    half = QK_ROPE // 2
    return jnp.concatenate([-w[..., half:], w[..., :half]], axis=-1)


def _mla_weights(w_in, w_uq, w_ukv):
    w_kr = w_in[:, Q_LORA + KV_LORA:]
    padl = jnp.zeros((D_MODEL, QK_NOPE), F32)
    padr = jnp.zeros((D_MODEL, HEAD_PAD - QK_HEAD), F32)
    win_ext = jnp.concatenate(
        [w_in[:, :Q_LORA + KV_LORA], padl, w_kr, padr, padl, _rot_cols(w_kr), padr],
        axis=-1).astype(BF16)
    wq = w_uq.reshape(Q_LORA, MLA_HEADS, QK_HEAD)
    wq_ext = jnp.concatenate([wq, _rot_cols(wq[..., QK_NOPE:])], axis=-1)
    wqt = wq_ext.reshape(Q_LORA, MLA_HEADS * HEAD_PAD).T.astype(BF16)
    wkv = w_ukv.reshape(KV_LORA, MLA_HEADS, QK_NOPE + V_HEAD)
    wk = jnp.concatenate(
        [wkv[..., :QK_NOPE], jnp.zeros((KV_LORA, MLA_HEADS, HEAD_PAD - QK_NOPE), F32)],
        axis=-1).reshape(KV_LORA, MLA_HEADS * HEAD_PAD).astype(BF16)
    wvt = wkv[..., QK_NOPE:].reshape(KV_LORA, MLA_HEADS * V_HEAD).T.astype(BF16)
    return win_ext, wqt, wk, wvt


def _rope_tables(first_pos, t_len):
    pos = first_pos + jnp.arange(t_len, dtype=F32)
    inv_freq = ROPE_BASE ** (-jnp.arange(0, QK_ROPE, 2, dtype=F32) / QK_ROPE)
    ang = _to_strided(pos[:, None] * inv_freq[None, :], 0)
    cos2 = jnp.concatenate([jnp.cos(ang)] * 2, axis=-1)
    sin2 = jnp.concatenate([jnp.sin(ang)] * 2, axis=-1)
    lpad = ((0, 0), (QK_NOPE, HEAD_PAD - QK_HEAD))
    return jnp.pad(cos2, lpad), jnp.pad(sin2, lpad), cos2.T, sin2.T


def _mla_proj(h, tables, g, q_norm, kv_norm, weights):
    bsz, tlen, _ = h.shape
    tile = TILE_T
    nt = tlen // tile
    win_ext, wqt, wk, wvt = weights
    args = (g.reshape(1, D_MODEL), win_ext, q_norm.reshape(1, Q_LORA),
            kv_norm.reshape(1, KV_LORA), wqt, wk, wvt)
    row_spec = pl.BlockSpec((1, tile, D_MODEL), lambda b, t: (b, t, 0))
    in_specs = [row_spec] + [_const_spec(a.shape) for a in args] + [
        pl.BlockSpec((tile, HEAD_PAD), lambda b, t: (t, 0)),
        pl.BlockSpec((tile, HEAD_PAD), lambda b, t: (t, 0)),
        pl.BlockSpec((QK_ROPE, tile), lambda b, t: (0, t)),
        pl.BlockSpec((QK_ROPE, tile), lambda b, t: (0, t)),
    ]
    out_shape = (
        jax.ShapeDtypeStruct((bsz, MLA_HEADS, nt, HEAD_PAD, tile), BF16),
        jax.ShapeDtypeStruct((bsz, MLA_HEADS, nt, tile, HEAD_PAD), BF16),
        jax.ShapeDtypeStruct((bsz, MLA_HEADS, nt, V_ROWS, tile), BF16),
    )
    out_specs = (
        pl.BlockSpec((1, MLA_HEADS, 1, HEAD_PAD, tile), lambda b, t: (b, 0, t, 0, 0)),
        pl.BlockSpec((1, MLA_HEADS, 1, tile, HEAD_PAD), lambda b, t: (b, 0, t, 0, 0)),
        pl.BlockSpec((1, MLA_HEADS, 1, V_ROWS, tile), lambda b, t: (b, 0, t, 0, 0)),
    )
    return pl.pallas_call(
        _mla_proj_kernel,
        grid=(bsz, nt),
        in_specs=in_specs,
        out_specs=out_specs,
        out_shape=out_shape,
        compiler_params=pltpu.CompilerParams(
            dimension_semantics=("arbitrary", "arbitrary"), vmem_limit_bytes=VMEM_LIMIT),
        name="mla_proj",
    )(h, *args, *tables)


def _strided_time(i):
    return (i & (SUBLANES - 1)) * GROUPS + (i >> 3)


def _attn_kernel(*refs, has_meta):
    if has_meta:
        qt_ref, k_ref, vt_ref, km_ref, vtm_ref, o_ref, s_ref = refs
    else:
        qt_ref, k_ref, vt_ref, o_ref, s_ref = refs
    tile = qt_ref.shape[4]
    qi = pl.program_id(2)
    neg = jnp.finfo(F32).min
    heads = range(HEADS_PER_STEP)

    def softmax_update(state, scores, vt, keep=None):
        def block():
            s = scores()
            return s if keep is None else jnp.where(keep, s, neg)
        m, acc = state
        m_new = jnp.maximum(m, jnp.max(block(), axis=0, keepdims=True))
        p = jnp.exp2(block() - m_new).astype(BF16)
        return m_new, jnp.exp2(m - m_new) * acc + _dot(vt, p)

    def produce(slot, j, hh):
        s_ref[slot, hh] = _dot(k_ref[0, hh, j], qt_ref[0, hh, 0])

    def consume(carry, slot, j, next_j, keep=None):
        if next_j is not None:
            for hh in range(SCORE_LEAD):
                produce(1 - slot, next_j, hh)
        out = []
        for hh in heads:
            out.append(softmax_update(
                carry[hh], lambda: s_ref[slot, hh], vt_ref[0, hh, j], keep))
            if next_j is not None and hh + SCORE_LEAD < HEADS_PER_STEP:
                produce(1 - slot, next_j, hh + SCORE_LEAD)
        return tuple(out)

    carry = tuple((jnp.full((1, tile), neg, F32), jnp.zeros((V_ROWS, tile), F32))
                  for _ in heads)
    if has_meta:
        meta_scores = [_dot(km_ref[hh], qt_ref[0, hh, 0]) for hh in heads]
    for hh in heads:
        produce(0, 0, hh)
    if has_meta:
        carry = tuple(softmax_update(carry[hh], lambda: meta_scores[hh], vtm_ref[hh])
                      for hh in heads)

    def pair(jj, c):
        c = consume(c, 0, 2 * jj, 2 * jj + 1)
        return consume(c, 1, 2 * jj + 1, 2 * jj + 2)

    carry = lax.fori_loop(0, qi // 2, pair, carry)
    kpos = _strided_time(lax.broadcasted_iota(jnp.int32, (tile, tile), 0))
    qpos = _strided_time(lax.broadcasted_iota(jnp.int32, (tile, tile), 1))
    keep = kpos <= qpos
    carry = lax.cond(
        qi % 2 == 1,
        lambda c: consume(consume(c, 0, qi - 1, qi), 1, qi, None, keep),
        lambda c: consume(c, 0, qi, None, keep), carry)
    outs = [acc[0:V_HEAD, :] / acc[V_HEAD:V_HEAD + 1, :] for _, acc in carry]
    o_ref[0, 0] = jnp.concatenate(outs, axis=0).T.astype(o_ref.dtype)


def _attention(qt, k5, vt5, meta_kv=None):
    bsz, nh, nt, tile, _ = k5.shape
    tlen = nt * tile
    hps = HEADS_PER_STEP
    in_specs = [
        pl.BlockSpec((1, hps, 1, HEAD_PAD, tile), lambda b, h, q: (b, h, q, 0, 0)),
        pl.BlockSpec((1, hps, nt, tile, HEAD_PAD), lambda b, h, q: (b, h, 0, 0, 0)),
        pl.BlockSpec((1, hps, nt, V_ROWS, tile), lambda b, h, q: (b, h, 0, 0, 0)),
    ]
    args = (qt, k5, vt5)
    if meta_kv is not None:
        in_specs += [pl.BlockSpec((hps,) + a.shape[1:], lambda b, h, q: (h, 0, 0))
                     for a in meta_kv]
        args += tuple(meta_kv)
    return pl.pallas_call(
        functools.partial(_attn_kernel, has_meta=meta_kv is not None),
        grid=(bsz, nh // hps, nt),
        in_specs=in_specs,
        out_specs=pl.BlockSpec((1, 1, tile, hps * V_HEAD), lambda b, h, q: (b, h, q, 0)),
        out_shape=jax.ShapeDtypeStruct((bsz, nh // hps, tlen, hps * V_HEAD), BF16),
        scratch_shapes=[pltpu.VMEM((2, hps, tile, tile), F32)],
        compiler_params=pltpu.CompilerParams(
            dimension_semantics=("arbitrary", "arbitrary", "arbitrary"),
            vmem_limit_bytes=VMEM_LIMIT),
        name="mla_attention",
    )(*args)


def _to_strided(a, axis):
    shp = a.shape
    nt = shp[axis] // TILE_T
    a = a.reshape(shp[:axis] + (nt, SUBLANES, GROUPS) + shp[axis + 1:])
    return jnp.swapaxes(a, axis + 1, axis + 2).reshape(shp)


def _from_strided(a, axis):
    shp = a.shape
    nt = shp[axis] // TILE_T
    a = a.reshape(shp[:axis] + (nt, GROUPS, SUBLANES) + shp[axis + 1:])
    return jnp.swapaxes(a, axis + 1, axis + 2).reshape(shp)


def _trunk(h, first_pos, states, p, state_at):
    depth = len(p["ffn"])
    tables = _rope_tables(first_pos, h.shape[1])
    new_states = []
    for layer in range(depth):
        st = states[layer]
        ffn = p["ffn"][layer]
        attn = None
        if layer % 2 == 0:
            h, mix_state = _even_layer(h, st["mix"], *p["even"][layer // 2], state_at=state_at)
        else:
            g, q_norm, kv_norm, weights, w_out = p["odd"][layer // 2]
            qt, k5, vt5 = _mla_proj(h, tables, g, q_norm, kv_norm, weights)
            attn = (_attention(qt, k5, vt5, st["mix"]), w_out)
            mix_state = None
            if state_at is not None:
                n = state_at + 1
                assert n <= GROUPS
                mix_state = (k5[0, :, 0, 0:n * SUBLANES:SUBLANES, :],
                             vt5[0, :, 0, :, 0:n * SUBLANES:SUBLANES])
        h, tail = _ffn_layer(h, st["ffn"], *ffn, attn=attn,
                             final_g=p["final"] if layer == depth - 1 else None,
                             state_at=state_at)
        new_states.append({"mix": mix_state, "ffn": tail})
    return h, new_states


def kernel(x, meta_tokens, ev_norm, ev_w_in, ev_conv_a, ev_conv_b, ev_conv_b_bias, ev_gate_r_w, ev_gate_r_b, ev_gate_i_w, ev_gate_i_b, ev_lru_lambda, ev_w_out, od_norm, od_w_in, od_q_norm, od_kv_norm, od_w_uq, od_w_ukv, od_w_out, ffn_norm, ffn_w_up, ffn_conv_w, ffn_conv_b, ffn_w_down, final_norm):
    bsz, seq, _ = x.shape
    depth = ffn_norm.shape[0]
    assert seq % TILE_T == 0 and N_META <= GROUPS
    params = {
        "even": [(ev_norm[j], ev_w_in[j].astype(BF16), ev_conv_a[j], ev_conv_b[j],
                  ev_conv_b_bias[j], _block_diag(ev_gate_r_w[j]).astype(BF16), ev_gate_r_b[j],
                  _block_diag(ev_gate_i_w[j]).astype(BF16), ev_gate_i_b[j], ev_lru_lambda[j],
                  ev_w_out[j].astype(BF16)) for j in range(ev_norm.shape[0])],
        "odd": [(od_norm[j], od_q_norm[j], od_kv_norm[j],
                 _mla_weights(od_w_in[j], od_w_uq[j], od_w_ukv[j]), od_w_out[j].astype(BF16))
                for j in range(od_norm.shape[0])],
        "ffn": [(ffn_norm[l], ffn_w_up[l].astype(BF16), ffn_conv_w[l], ffn_conv_b[l],
                 ffn_w_down[l].astype(BF16)) for l in range(depth)],
        "final": final_norm,
    }
    zero_states = []
    for layer in range(depth):
        mix = None
        if layer % 2 == 0:
            mix = (jnp.zeros(((ev_conv_a.shape[1] - 1) * SUBLANES, CONV_WIDTH), F32),
                   jnp.zeros(((ev_conv_b.shape[1] - 1) * SUBLANES, LRU_WIDTH), F32),
                   jnp.zeros((1, LRU_WIDTH), F32))
        zero_states.append(
            {"mix": mix, "ffn": jnp.zeros(((ffn_conv_w.shape[1] - 1) * SUBLANES, 2 * D_FF), F32)})

    meta_tile = jnp.concatenate(
        [meta_tokens.astype(x.dtype), jnp.zeros((TILE_T - N_META, D_MODEL), x.dtype)], axis=0)
    _, meta_states = _trunk(_to_strided(meta_tile[None], 1), 0.0, zero_states, params,
                            state_at=N_META - 1)
    h, _ = _trunk(_to_strided(x, 1), float(N_META), meta_states, params, state_at=None)
    return _from_strided(h, 1)
```

```python
import functools
import math

import jax
import jax.numpy as jnp
from jax import lax
from jax.experimental import pallas as pl
from jax.experimental.pallas import tpu as pltpu

D_MODEL = 1024
N_META = 16
EPS = 1e-6
CONV_WIDTH = 512
LRU_WIDTH = 512
LRU_C = 8.0
MLA_HEADS = 16
QK_NOPE = 64
QK_ROPE = 32
QK_HEAD = QK_NOPE + QK_ROPE
V_HEAD = 64
Q_LORA = 384
KV_LORA = 256
ROPE_BASE = 10000.0
D_FF = 2816

LANES = 128
SUBLANES = 8
HEAD_PAD = 128
V_ROWS = V_HEAD + 16
HEADS_PER_STEP = 8
SCORE_LEAD = 4
TILE_T = 256
GROUPS = TILE_T // SUBLANES
STEP_T = 512
FF_CHUNK = 256
VMEM_LIMIT = 56 * 1024 * 1024

F32 = jnp.float32
BF16 = jnp.bfloat16


def _rms(x, g):
    ms = jnp.mean(x * x, axis=-1, keepdims=True)
    return x * lax.rsqrt(ms + EPS) * g


def _sigmoid(x):
    return 1.0 / (1.0 + jnp.exp(-x))


def _gelu_tanh(x):
    c = math.sqrt(2.0 / math.pi)
    return x * (0.5 * (1.0 + jnp.tanh(c * (x + 0.044715 * (x * x * x)))))


def _dot(a, b):
    return jnp.dot(a, b, preferred_element_type=F32)


def _dot_nt(a, b):
    return lax.dot_general(a, b, (((1,), (1,)), ((), ())), preferred_element_type=F32)


def _row_groups(x):
    return [x[v * SUBLANES:(v + 1) * SUBLANES, :] for v in range(x.shape[0] // SUBLANES)]


def _delays(u, prev, kmax):
    tile = u.shape[0]
    first = lax.broadcasted_iota(jnp.int32, (SUBLANES, u.shape[1]), 0) == 0
    wrapped = []
    for i in range(kmax):
        cur = u[tile - (kmax - i) * SUBLANES:tile - (kmax - i - 1) * SUBLANES, :]
        old = prev[i * SUBLANES:(i + 1) * SUBLANES, :]
        wrapped.append(jnp.where(first, pltpu.roll(old, 1, axis=0), pltpu.roll(cur, 1, axis=0)))
    return [jnp.concatenate(wrapped[kmax - k:] + [u[:tile - k * SUBLANES, :]], axis=0)
            for k in range(1, kmax + 1)]


def _tiles(x):
    return [x[i * TILE_T:(i + 1) * TILE_T, :] for i in range(x.shape[0] // TILE_T)]


def _stack(parts):
    return parts[0] if len(parts) == 1 else jnp.concatenate(parts, axis=0)


def _causal_conv(u, prev, taps, state_at=None):
    k = taps.shape[0]
    outs = []
    for ui in _tiles(u):
        delayed = _delays(ui, prev, k - 1)
        acc = delayed[k - 2] * taps[0:1, :]
        for j in range(1, k - 1):
            acc = acc + delayed[k - 2 - j] * taps[j:j + 1, :]
        outs.append(acc + ui * taps[k - 1:k, :])
        prev = _conv_tail(ui, k - 1, state_at)
    return _stack(outs), prev


def _conv_tail(u, kmax, state_at):
    tile = u.shape[0]
    if state_at is None:
        return u[tile - kmax * SUBLANES:, :]
    s, v = divmod(state_at, GROUPS)
    assert v >= kmax - 1
    rows = []
    for i in range(kmax):
        grp = u[(v - (kmax - 1 - i)) * SUBLANES:(v - (kmax - 2 - i)) * SUBLANES, :]
        rows.append(pltpu.roll(grp, SUBLANES - 1 - s, axis=0) if s != SUBLANES - 1 else grp)
    return jnp.concatenate(rows, axis=0)


def _lru_scan(a, b, carry, state_at=None):
    outs = []
    for ai, bi in zip(_tiles(a), _tiles(b)):
        hi, carry = _lru_scan_tile(ai, bi, carry, state_at)
        outs.append(hi)
    return _stack(outs), carry


def _lru_scan_tile(a, b, carry, state_at):
    a_rows, b_rows = _row_groups(a), _row_groups(b)
    prod, hzero = [a_rows[0]], [b_rows[0]]
    for v in range(1, len(a_rows)):
        prod.append(a_rows[v] * prod[-1])
        hzero.append(a_rows[v] * hzero[-1] + b_rows[v])
    pa, ph = prod[-1], hzero[-1]
    sub = lax.broadcasted_iota(jnp.int32, pa.shape, 0)
    for d in (1, 2, 4):
        ok = sub >= d
        ph = jnp.where(ok, pa * pltpu.roll(ph, d, axis=0) + ph, ph)
        pa = jnp.where(ok, pa * pltpu.roll(pa, d, axis=0), pa)
    ends = ph + pa * carry
    init = jnp.where(sub == 0, carry, pltpu.roll(ends, 1, axis=0))
    h = jnp.concatenate([hz + pr * init for hz, pr in zip(hzero, prod)], axis=0)
    if state_at is None:
        return h, ends[SUBLANES - 1:SUBLANES, :]
    s, v = divmod(state_at, GROUPS)
    row = v * SUBLANES + s
    return h, h[row:row + 1, :]


def _even_kernel(*refs, state_at):
    (x_ref, zin_ref, xbin_ref, hin_ref, g_ref, win_ref, ca_ref, cb_ref, cbb_ref, rw_ref, rb_ref,
     iw_ref, ib_ref, lam_ref, wout_ref, o_ref) = refs[:16]
    ztail, xbtail, hstate = refs[-3:]
    t = pl.program_id(1)

    @pl.when(t == 0)
    def _():
        ztail[...] = zin_ref[...]
        xbtail[...] = xbin_ref[...]
        hstate[...] = hin_ref[...]

    hn = _rms(x_ref[0], g_ref[...]).astype(BF16)
    u = _dot(hn, win_ref[...])
    cw = CONV_WIDTH
    gb = u[:, 0:cw]
    gc = u[:, cw:2 * cw]
    xa = u[:, 2 * cw:3 * cw]
    xb = u[:, 3 * cw:3 * cw + LRU_WIDTH]
    gate = u[:, 3 * cw + LRU_WIDTH:]

    conv_z, ztail[...] = _causal_conv(gc * xa, ztail[...], ca_ref[...], state_at)
    y_a = gb * conv_z

    conv_xb, xbtail[...] = _causal_conv(xb, xbtail[...], cb_ref[...], state_at)
    xc = conv_xb + cbb_ref[...]

    xcb = xc.astype(BF16)
    r = _sigmoid(_dot(xcb, rw_ref[...]) + rb_ref[...])
    i = _sigmoid(_dot(xcb, iw_ref[...]) + ib_ref[...])
    nlam = -lam_ref[...]
    softplus = jnp.maximum(nlam, 0.0) + jnp.log1p(jnp.exp(-jnp.abs(nlam)))
    log_a = -LRU_C * r * softplus
    a = jnp.exp(log_a)
    th = jnp.tanh(log_a)
    mult = jnp.sqrt(-2.0 * th / (1.0 - th))
    h, hstate[...] = _lru_scan(a, mult * (i * xc), hstate[...], state_at)

    y_b = _gelu_tanh(gate) * h
    y = jnp.concatenate([y_a, y_b], axis=-1).astype(BF16)
    o_ref[0] = x_ref[0] + _dot(y, wout_ref[...])
    if state_at is not None:
        zout_ref, xbout_ref, hout_ref = refs[16:19]
        zout_ref[...] = ztail[...]
        xbout_ref[...] = xbtail[...]
        hout_ref[...] = hstate[...]


def _const_spec(shape):
    nd = len(shape)
    return pl.BlockSpec(shape, lambda b, t: (0,) * nd, pipeline_mode=pl.Buffered(1))


def _shapes(arrays):
    return tuple(jax.ShapeDtypeStruct(a.shape, a.dtype) for a in arrays)


def _even_layer(h, state, g, w_in, conv_a, conv_b, conv_b_bias, rw, rb, iw, ib, lam, w_out,
                state_at=None):
    bsz, tlen, _ = h.shape
    tile = min(STEP_T, tlen)
    row_spec = pl.BlockSpec((1, tile, D_MODEL), lambda b, t: (b, t, 0))
    args = tuple(state) + (
        g.reshape(1, D_MODEL), w_in, conv_a, conv_b,
        conv_b_bias.reshape(1, LRU_WIDTH), rw, rb.reshape(1, LRU_WIDTH), iw,
        ib.reshape(1, LRU_WIDTH), lam.reshape(1, LRU_WIDTH), w_out)
    out_shape = [jax.ShapeDtypeStruct(h.shape, F32)]
    out_specs = [row_spec]
    if state_at is not None:
        assert bsz == 1 and tlen == tile
        out_shape += list(_shapes(state))
        out_specs += [_const_spec(s.shape) for s in state]
    res = pl.pallas_call(
        functools.partial(_even_kernel, state_at=state_at),
        grid=(bsz, tlen // tile),
        in_specs=[row_spec] + [_const_spec(a.shape) for a in args],
        out_specs=out_specs,
        out_shape=out_shape,
        scratch_shapes=[pltpu.VMEM(s.shape, F32) for s in state],
        compiler_params=pltpu.CompilerParams(
            dimension_semantics=("arbitrary", "arbitrary"), vmem_limit_bytes=VMEM_LIMIT),
        name="even_mixer",
    )(h, *args)
    return res[0], (tuple(res[1:]) if state_at is not None else None)


def _block_diag(w):
    nh, d, _ = w.shape
    eye = jnp.eye(nh, dtype=w.dtype)
    return jnp.einsum("hij,hg->higj", w, eye).reshape(nh * d, nh * d)


def _ffn_kernel(*refs, attn_input, final_norm, state_at):
    refs = list(refs)
    x_ref, tin_ref = refs.pop(0), refs.pop(0)
    attn_ref, wo_ref = (refs.pop(0), refs.pop(0)) if attn_input else (None, None)
    g_ref, wup_ref, cw_ref, cb_ref, wdn_ref = refs[:5]
    del refs[:5]
    fg_ref = refs.pop(0) if final_norm else None
    o_ref = refs.pop(0)
    tout_ref = refs.pop(0) if state_at is not None else None
    hn_ref, act_ref, tail = refs
    t = pl.program_id(1)

    @pl.when(t == 0)
    def _():
        tail[...] = tin_ref[...]

    if attn_input:
        wg = attn_ref.shape[3]
        y = x_ref[0]
        for grp in range(attn_ref.shape[1]):
            y = y + _dot(attn_ref[0, grp], wo_ref[grp * wg:(grp + 1) * wg, :])
        o_ref[0] = y
    else:
        o_ref[0] = x_ref[0]
    hn_ref[...] = _rms(o_ref[0], g_ref[...]).astype(BF16)
    for c in range(D_FF // FF_CHUNK):
        halves = []
        for part in range(2):
            col = part * D_FF + c * FF_CHUNK
            cs = slice(col, col + FF_CHUNK)
            u = _dot(hn_ref[...], wup_ref[:, cs])
            conv_u, tail[:, cs] = _causal_conv(u, tail[:, cs], cw_ref[:, cs], state_at)
            halves.append(conv_u + cb_ref[:, cs])
        a, gte = halves
        act_ref[:, c * FF_CHUNK:(c + 1) * FF_CHUNK] = (a * _sigmoid(a) * gte).astype(BF16)
    y = o_ref[0] + _dot(act_ref[...], wdn_ref[...])
    o_ref[0] = _rms(y, fg_ref[...]) if final_norm else y
    if state_at is not None:
        tout_ref[...] = tail[...]


def _ffn_layer(h, tail, g, w_up, conv_w, conv_b, w_down, attn=None, final_g=None, state_at=None):
    bsz, tlen, _ = h.shape
    tile = min(STEP_T, tlen)
    row_spec = pl.BlockSpec((1, tile, D_MODEL), lambda b, t: (b, t, 0))
    args = (g.reshape(1, D_MODEL), w_up, conv_w, conv_b.reshape(1, 2 * D_FF), w_down)
    if final_g is not None:
        args = args + (final_g.reshape(1, D_MODEL),)
    specs = [_const_spec(a.shape) for a in args]
    if attn is not None:
        args = attn + args
        attn_spec = pl.BlockSpec((1, attn[0].shape[1], tile, attn[0].shape[3]),
                                 lambda b, t: (b, 0, t, 0))
        specs = [attn_spec, _const_spec(attn[1].shape)] + specs
    out_shape = [jax.ShapeDtypeStruct(h.shape, F32)]
    out_specs = [row_spec]
    if state_at is not None:
        assert bsz == 1 and tlen == tile
        out_shape.append(jax.ShapeDtypeStruct(tail.shape, F32))
        out_specs.append(_const_spec(tail.shape))
    res = pl.pallas_call(
        functools.partial(_ffn_kernel, attn_input=attn is not None,
                          final_norm=final_g is not None, state_at=state_at),
        grid=(bsz, tlen // tile),
        in_specs=[row_spec, _const_spec(tail.shape)] + specs,
        out_specs=out_specs,
        out_shape=out_shape,
        scratch_shapes=[
            pltpu.VMEM((tile, D_MODEL), BF16),
            pltpu.VMEM((tile, D_FF), BF16),
            pltpu.VMEM(tail.shape, F32),
        ],
        compiler_params=pltpu.CompilerParams(
            dimension_semantics=("arbitrary", "arbitrary"), vmem_limit_bytes=VMEM_LIMIT),
        name="conv_ffn",
    )(h, tail, *args)
    return res[0], (res[1] if state_at is not None else None)


def _mla_proj_kernel(x_ref, g_ref, win_ref, qn_ref, kvn_ref, wqt_ref, wk_ref, wvt_ref,
                     cosk_ref, sink_ref, cosq_ref, sinq_ref, qt_ref, k_ref, vt_ref):
    x = x_ref[0]
    hn = _rms(x, g_ref[...]).astype(BF16)
    u = _dot(hn, win_ref[...])
    cq = u[:, 0:Q_LORA]
    ckv = u[:, Q_LORA:Q_LORA + KV_LORA]
    kr = u[:, Q_LORA + KV_LORA:Q_LORA + KV_LORA + HEAD_PAD]
    kr_rot = u[:, Q_LORA + KV_LORA + HEAD_PAD:]
    cqn = _rms(cq, qn_ref[...]).astype(BF16)
    ckvn = _rms(ckv, kvn_ref[...]).astype(BF16)

    qt = _dot_nt(wqt_ref[...], cqn)
    kn = _dot(ckvn, wk_ref[...])
    vt = _dot_nt(wvt_ref[...], ckvn)
    k_rope = kr * cosk_ref[...] + kr_rot * sink_ref[...]
    cosq = cosq_ref[...]
    sinq = sinq_ref[...]
    scale = QK_HEAD ** -0.5 * math.log2(math.e)
    zeros = jnp.zeros((HEAD_PAD - QK_HEAD, qt.shape[1]), F32)
    ones = jnp.ones((V_ROWS - V_HEAD, qt.shape[1]), F32)
    for h in range(MLA_HEADS):
        qh = qt[h * HEAD_PAD:(h + 1) * HEAD_PAD, :]
        roped = qh[QK_NOPE:QK_HEAD, :] * cosq + qh[QK_HEAD:, :] * sinq
        q_out = jnp.concatenate(
            [qh[0:QK_NOPE, :] * scale, roped * scale, zeros], axis=0).astype(BF16)
        k_out = (kn[:, h * HEAD_PAD:(h + 1) * HEAD_PAD] + k_rope).astype(BF16)
        v_out = jnp.concatenate(
            [vt[h * V_HEAD:(h + 1) * V_HEAD, :], ones], axis=0).astype(BF16)
        for i in range(qt_ref.shape[2]):
            rows = slice(i * TILE_T, (i + 1) * TILE_T)
            qt_ref[0, h, i] = q_out[:, rows]
            k_ref[0, h, i] = k_out[rows, :]
            vt_ref[0, h, i] = v_out[:, rows]


def _rot_cols(w):
    half = QK_ROPE // 2
    return jnp.concatenate([-w[..., half:], w[..., :half]], axis=-1)


def _mla_weights(w_in, w_uq, w_ukv):
    w_kr = w_in[:, Q_LORA + KV_LORA:]
    padl = jnp.zeros((D_MODEL, QK_NOPE), F32)
    padr = jnp.zeros((D_MODEL, HEAD_PAD - QK_HEAD), F32)
    win_ext = jnp.concatenate(
        [w_in[:, :Q_LORA + KV_LORA], padl, w_kr, padr, padl, _rot_cols(w_kr), padr],
        axis=-1).astype(BF16)
    wq = w_uq.reshape(Q_LORA, MLA_HEADS, QK_HEAD)
    wq_ext = jnp.concatenate([wq, _rot_cols(wq[..., QK_NOPE:])], axis=-1)
    wqt = wq_ext.reshape(Q_LORA, MLA_HEADS * HEAD_PAD).T.astype(BF16)
    wkv = w_ukv.reshape(KV_LORA, MLA_HEADS, QK_NOPE + V_HEAD)
    wk = jnp.concatenate(
        [wkv[..., :QK_NOPE], jnp.zeros((KV_LORA, MLA_HEADS, HEAD_PAD - QK_NOPE), F32)],
        axis=-1).reshape(KV_LORA, MLA_HEADS * HEAD_PAD).astype(BF16)
    wvt = wkv[..., QK_NOPE:].reshape(KV_LORA, MLA_HEADS * V_HEAD).T.astype(BF16)
    return win_ext, wqt, wk, wvt


def _rope_tables(first_pos, t_len):
    pos = first_pos + jnp.arange(t_len, dtype=F32)
    inv_freq = ROPE_BASE ** (-jnp.arange(0, QK_ROPE, 2, dtype=F32) / QK_ROPE)
    ang = _to_strided(pos[:, None] * inv_freq[None, :], 0)
    cos2 = jnp.concatenate([jnp.cos(ang)] * 2, axis=-1)
    sin2 = jnp.concatenate([jnp.sin(ang)] * 2, axis=-1)
    lpad = ((0, 0), (QK_NOPE, HEAD_PAD - QK_HEAD))
    return jnp.pad(cos2, lpad), jnp.pad(sin2, lpad), cos2.T, sin2.T


def _mla_proj(h, tables, g, q_norm, kv_norm, weights):
    bsz, tlen, _ = h.shape
    tile = TILE_T
    nt = tlen // tile
    step = min(STEP_T, tlen)
    per = step // tile
    win_ext, wqt, wk, wvt = weights
    args = (g.reshape(1, D_MODEL), win_ext, q_norm.reshape(1, Q_LORA),
            kv_norm.reshape(1, KV_LORA), wqt, wk, wvt)
    row_spec = pl.BlockSpec((1, step, D_MODEL), lambda b, t: (b, t, 0))
    in_specs = [row_spec] + [_const_spec(a.shape) for a in args] + [
        pl.BlockSpec((step, HEAD_PAD), lambda b, t: (t, 0)),
        pl.BlockSpec((step, HEAD_PAD), lambda b, t: (t, 0)),
        pl.BlockSpec((QK_ROPE, step), lambda b, t: (0, t)),
        pl.BlockSpec((QK_ROPE, step), lambda b, t: (0, t)),
    ]
    out_shape = (
        jax.ShapeDtypeStruct((bsz, MLA_HEADS, nt, HEAD_PAD, tile), BF16),
        jax.ShapeDtypeStruct((bsz, MLA_HEADS, nt, tile, HEAD_PAD), BF16),
        jax.ShapeDtypeStruct((bsz, MLA_HEADS, nt, V_ROWS, tile), BF16),
    )
    out_specs = (
        pl.BlockSpec((1, MLA_HEADS, per, HEAD_PAD, tile), lambda b, t: (b, 0, t, 0, 0)),
        pl.BlockSpec((1, MLA_HEADS, per, tile, HEAD_PAD), lambda b, t: (b, 0, t, 0, 0)),
        pl.BlockSpec((1, MLA_HEADS, per, V_ROWS, tile), lambda b, t: (b, 0, t, 0, 0)),
    )
    return pl.pallas_call(
        _mla_proj_kernel,
        grid=(bsz, tlen // step),
        in_specs=in_specs,
        out_specs=out_specs,
        out_shape=out_shape,
        compiler_params=pltpu.CompilerParams(
            dimension_semantics=("arbitrary", "arbitrary"), vmem_limit_bytes=VMEM_LIMIT),
        name="mla_proj",
    )(h, *args, *tables)


def _strided_time(i):
    return (i & (SUBLANES - 1)) * GROUPS + (i >> 3)


def _attn_kernel(*refs, has_meta):
    if has_meta:
        qt_ref, k_ref, vt_ref, km_ref, vtm_ref, o_ref, s_ref = refs
    else:
        qt_ref, k_ref, vt_ref, o_ref, s_ref = refs
    tile = qt_ref.shape[4]
    qi = pl.program_id(2)
    neg = jnp.finfo(F32).min
    heads = range(HEADS_PER_STEP)

    def softmax_update(state, scores, vt, keep=None):
        def block():
            s = scores()
            return s if keep is None else jnp.where(keep, s, neg)
        m, acc = state
        m_new = jnp.maximum(m, jnp.max(block(), axis=0, keepdims=True))
        p = jnp.exp2(block() - m_new).astype(BF16)
        return m_new, jnp.exp2(m - m_new) * acc + _dot(vt, p)

    def produce(slot, j, hh):
        s_ref[slot, hh] = _dot(k_ref[0, hh, j], qt_ref[0, hh, 0])

    def consume(carry, slot, j, next_j, keep=None):
        if next_j is not None:
            for hh in range(SCORE_LEAD):
                produce(1 - slot, next_j, hh)
        out = []
        for hh in heads:
            out.append(softmax_update(
                carry[hh], lambda: s_ref[slot, hh], vt_ref[0, hh, j], keep))
            if next_j is not None and hh + SCORE_LEAD < HEADS_PER_STEP:
                produce(1 - slot, next_j, hh + SCORE_LEAD)
        return tuple(out)

    carry = tuple((jnp.full((1, tile), neg, F32), jnp.zeros((V_ROWS, tile), F32))
                  for _ in heads)
    if has_meta:
        meta_scores = [_dot(km_ref[hh], qt_ref[0, hh, 0]) for hh in heads]
    for hh in heads:
        produce(0, 0, hh)
    if has_meta:
        carry = tuple(softmax_update(carry[hh], lambda: meta_scores[hh], vtm_ref[hh])
                      for hh in heads)

    def pair(jj, c):
        c = consume(c, 0, 2 * jj, 2 * jj + 1)
        return consume(c, 1, 2 * jj + 1, 2 * jj + 2)

    carry = lax.fori_loop(0, qi // 2, pair, carry)
    kpos = _strided_time(lax.broadcasted_iota(jnp.int32, (tile, tile), 0))
    qpos = _strided_time(lax.broadcasted_iota(jnp.int32, (tile, tile), 1))
    keep = kpos <= qpos
    carry = lax.cond(
        qi % 2 == 1,
        lambda c: consume(consume(c, 0, qi - 1, qi), 1, qi, None, keep),
        lambda c: consume(c, 0, qi, None, keep), carry)
    outs = [acc[0:V_HEAD, :] / acc[V_HEAD:V_HEAD + 1, :] for _, acc in carry]
    o_ref[0, 0] = jnp.concatenate(outs, axis=0).T.astype(o_ref.dtype)


def _attention(qt, k5, vt5, meta_kv=None):
    bsz, nh, nt, tile, _ = k5.shape
    tlen = nt * tile
    hps = HEADS_PER_STEP
    in_specs = [
        pl.BlockSpec((1, hps, 1, HEAD_PAD, tile), lambda b, h, q: (b, h, q, 0, 0)),
        pl.BlockSpec((1, hps, nt, tile, HEAD_PAD), lambda b, h, q: (b, h, 0, 0, 0)),
        pl.BlockSpec((1, hps, nt, V_ROWS, tile), lambda b, h, q: (b, h, 0, 0, 0)),
    ]
    args = (qt, k5, vt5)
    if meta_kv is not None:
        in_specs += [pl.BlockSpec((hps,) + a.shape[1:], lambda b, h, q: (h, 0, 0))
                     for a in meta_kv]
        args += tuple(meta_kv)
    return pl.pallas_call(
        functools.partial(_attn_kernel, has_meta=meta_kv is not None),
        grid=(bsz, nh // hps, nt),
        in_specs=in_specs,
        out_specs=pl.BlockSpec((1, 1, tile, hps * V_HEAD), lambda b, h, q: (b, h, q, 0)),
        out_shape=jax.ShapeDtypeStruct((bsz, nh // hps, tlen, hps * V_HEAD), BF16),
        scratch_shapes=[pltpu.VMEM((2, hps, tile, tile), F32)],
        compiler_params=pltpu.CompilerParams(
            dimension_semantics=("arbitrary", "arbitrary", "arbitrary"),
            vmem_limit_bytes=VMEM_LIMIT),
        name="mla_attention",
    )(*args)


def _to_strided(a, axis):
    shp = a.shape
    nt = shp[axis] // TILE_T
    a = a.reshape(shp[:axis] + (nt, SUBLANES, GROUPS) + shp[axis + 1:])
    return jnp.swapaxes(a, axis + 1, axis + 2).reshape(shp)


def _from_strided(a, axis):
    shp = a.shape
    nt = shp[axis] // TILE_T
    a = a.reshape(shp[:axis] + (nt, GROUPS, SUBLANES) + shp[axis + 1:])
    return jnp.swapaxes(a, axis + 1, axis + 2).reshape(shp)


def _trunk(h, first_pos, states, p, state_at):
    depth = len(p["ffn"])
    tables = _rope_tables(first_pos, h.shape[1])
    new_states = []
    for layer in range(depth):
        st = states[layer]
        ffn = p["ffn"][layer]
        attn = None
        if layer % 2 == 0:
            h, mix_state = _even_layer(h, st["mix"], *p["even"][layer // 2], state_at=state_at)
        else:
            g, q_norm, kv_norm, weights, w_out = p["odd"][layer // 2]
            qt, k5, vt5 = _mla_proj(h, tables, g, q_norm, kv_norm, weights)
            attn = (_attention(qt, k5, vt5, st["mix"]), w_out)
            mix_state = None
            if state_at is not None:
                n = state_at + 1
                assert n <= GROUPS
                mix_state = (k5[0, :, 0, 0:n * SUBLANES:SUBLANES, :],
                             vt5[0, :, 0, :, 0:n * SUBLANES:SUBLANES])
        h, tail = _ffn_layer(h, st["ffn"], *ffn, attn=attn,
                             final_g=p["final"] if layer == depth - 1 else None,
                             state_at=state_at)
        new_states.append({"mix": mix_state, "ffn": tail})
    return h, new_states


def kernel(x, meta_tokens, ev_norm, ev_w_in, ev_conv_a, ev_conv_b, ev_conv_b_bias, ev_gate_r_w, ev_gate_r_b, ev_gate_i_w, ev_gate_i_b, ev_lru_lambda, ev_w_out, od_norm, od_w_in, od_q_norm, od_kv_norm, od_w_uq, od_w_ukv, od_w_out, ffn_norm, ffn_w_up, ffn_conv_w, ffn_conv_b, ffn_w_down, final_norm):
    bsz, seq, _ = x.shape
    depth = ffn_norm.shape[0]
    assert seq % TILE_T == 0 and N_META <= GROUPS
    params = {
        "even": [(ev_norm[j], ev_w_in[j].astype(BF16), ev_conv_a[j], ev_conv_b[j],
                  ev_conv_b_bias[j], _block_diag(ev_gate_r_w[j]).astype(BF16), ev_gate_r_b[j],
                  _block_diag(ev_gate_i_w[j]).astype(BF16), ev_gate_i_b[j], ev_lru_lambda[j],
                  ev_w_out[j].astype(BF16)) for j in range(ev_norm.shape[0])],
        "odd": [(od_norm[j], od_q_norm[j], od_kv_norm[j],
                 _mla_weights(od_w_in[j], od_w_uq[j], od_w_ukv[j]), od_w_out[j].astype(BF16))
                for j in range(od_norm.shape[0])],
        "ffn": [(ffn_norm[l], ffn_w_up[l].astype(BF16), ffn_conv_w[l], ffn_conv_b[l],
                 ffn_w_down[l].astype(BF16)) for l in range(depth)],
        "final": final_norm,
    }
    zero_states = []
    for layer in range(depth):
        mix = None
        if layer % 2 == 0:
            mix = (jnp.zeros(((ev_conv_a.shape[1] - 1) * SUBLANES, CONV_WIDTH), F32),
                   jnp.zeros(((ev_conv_b.shape[1] - 1) * SUBLANES, LRU_WIDTH), F32),
                   jnp.zeros((1, LRU_WIDTH), F32))
        zero_states.append(
            {"mix": mix, "ffn": jnp.zeros(((ffn_conv_w.shape[1] - 1) * SUBLANES, 2 * D_FF), F32)})

    meta_tile = jnp.concatenate(
        [meta_tokens.astype(x.dtype), jnp.zeros((TILE_T - N_META, D_MODEL), x.dtype)], axis=0)
    _, meta_states = _trunk(_to_strided(meta_tile[None], 1), 0.0, zero_states, params,
                            state_at=N_META - 1)
    h, _ = _trunk(_to_strided(x, 1), float(N_META), meta_states, params, state_at=None)
    return _from_strided(h, 1)
```

```python
import functools
import math

import jax
import jax.numpy as jnp
from jax import lax
from jax.experimental import pallas as pl
from jax.experimental.pallas import tpu as pltpu

D_MODEL = 1024
N_META = 16
EPS = 1e-6
CONV_WIDTH = 512
LRU_WIDTH = 512
LRU_C = 8.0
MLA_HEADS = 16
QK_NOPE = 64
QK_ROPE = 32
QK_HEAD = QK_NOPE + QK_ROPE
V_HEAD = 64
Q_LORA = 384
KV_LORA = 256
ROPE_BASE = 10000.0
D_FF = 2816

LANES = 128
SUBLANES = 8
HEAD_PAD = 128
V_ROWS = V_HEAD + 16
HEADS_PER_STEP = 8
SCORE_LEAD = 4
TILE_T = 256
GROUPS = TILE_T // SUBLANES
STEP_T = 1024
FF_CHUNK = 256
VMEM_LIMIT = 56 * 1024 * 1024

F32 = jnp.float32
BF16 = jnp.bfloat16


def _rms(x, g):
    ms = jnp.mean(x * x, axis=-1, keepdims=True)
    return x * lax.rsqrt(ms + EPS) * g


def _sigmoid(x):
    return 1.0 / (1.0 + jnp.exp(-x))


def _gelu_tanh(x):
    c = math.sqrt(2.0 / math.pi)
    return x * (0.5 * (1.0 + jnp.tanh(c * (x + 0.044715 * (x * x * x)))))


def _dot(a, b):
    return jnp.dot(a, b, preferred_element_type=F32)


def _dot_nt(a, b):
    return lax.dot_general(a, b, (((1,), (1,)), ((), ())), preferred_element_type=F32)


def _row_groups(x):
    return [x[v * SUBLANES:(v + 1) * SUBLANES, :] for v in range(x.shape[0] // SUBLANES)]


def _delays(u, prev, kmax):
    tile = u.shape[0]
    first = lax.broadcasted_iota(jnp.int32, (SUBLANES, u.shape[1]), 0) == 0
    wrapped = []
    for i in range(kmax):
        cur = u[tile - (kmax - i) * SUBLANES:tile - (kmax - i - 1) * SUBLANES, :]
        old = prev[i * SUBLANES:(i + 1) * SUBLANES, :]
        wrapped.append(jnp.where(first, pltpu.roll(old, 1, axis=0), pltpu.roll(cur, 1, axis=0)))
    return [jnp.concatenate(wrapped[kmax - k:] + [u[:tile - k * SUBLANES, :]], axis=0)
            for k in range(1, kmax + 1)]


def _tiles(x):
    return [x[i * TILE_T:(i + 1) * TILE_T, :] for i in range(x.shape[0] // TILE_T)]


def _stack(parts):
    return parts[0] if len(parts) == 1 else jnp.concatenate(parts, axis=0)


def _causal_conv(u, prev, taps, state_at=None):
    k = taps.shape[0]
    outs = []
    for ui in _tiles(u):
        delayed = _delays(ui, prev, k - 1)
        acc = delayed[k - 2] * taps[0:1, :]
        for j in range(1, k - 1):
            acc = acc + delayed[k - 2 - j] * taps[j:j + 1, :]
        outs.append(acc + ui * taps[k - 1:k, :])
        prev = _conv_tail(ui, k - 1, state_at)
    return _stack(outs), prev


def _conv_tail(u, kmax, state_at):
    tile = u.shape[0]
    if state_at is None:
        return u[tile - kmax * SUBLANES:, :]
    s, v = divmod(state_at, GROUPS)
    assert v >= kmax - 1
    rows = []
    for i in range(kmax):
        grp = u[(v - (kmax - 1 - i)) * SUBLANES:(v - (kmax - 2 - i)) * SUBLANES, :]
        rows.append(pltpu.roll(grp, SUBLANES - 1 - s, axis=0) if s != SUBLANES - 1 else grp)
    return jnp.concatenate(rows, axis=0)


def _lru_scan(a, b, carry, state_at=None):
    outs = []
    for ai, bi in zip(_tiles(a), _tiles(b)):
        hi, carry = _lru_scan_tile(ai, bi, carry, state_at)
        outs.append(hi)
    return _stack(outs), carry


def _lru_scan_tile(a, b, carry, state_at):
    a_rows, b_rows = _row_groups(a), _row_groups(b)
    prod, hzero = [a_rows[0]], [b_rows[0]]
    for v in range(1, len(a_rows)):
        prod.append(a_rows[v] * prod[-1])
        hzero.append(a_rows[v] * hzero[-1] + b_rows[v])
    pa, ph = prod[-1], hzero[-1]
    sub = lax.broadcasted_iota(jnp.int32, pa.shape, 0)
    for d in (1, 2, 4):
        ok = sub >= d
        ph = jnp.where(ok, pa * pltpu.roll(ph, d, axis=0) + ph, ph)
        pa = jnp.where(ok, pa * pltpu.roll(pa, d, axis=0), pa)
    ends = ph + pa * carry
    init = jnp.where(sub == 0, carry, pltpu.roll(ends, 1, axis=0))
    h = jnp.concatenate([hz + pr * init for hz, pr in zip(hzero, prod)], axis=0)
    if state_at is None:
        return h, ends[SUBLANES - 1:SUBLANES, :]
    s, v = divmod(state_at, GROUPS)
    row = v * SUBLANES + s
    return h, h[row:row + 1, :]


def _even_kernel(*refs, state_at):
    (x_ref, zin_ref, xbin_ref, hin_ref, g_ref, win_ref, ca_ref, cb_ref, cbb_ref, rw_ref, rb_ref,
     iw_ref, ib_ref, lam_ref, wout_ref, o_ref) = refs[:16]
    ztail, xbtail, hstate = refs[-3:]
    t = pl.program_id(1)

    @pl.when(t == 0)
    def _():
        ztail[...] = zin_ref[...]
        xbtail[...] = xbin_ref[...]
        hstate[...] = hin_ref[...]

    hn = _rms(x_ref[0], g_ref[...]).astype(BF16)
    u = _dot(hn, win_ref[...])
    cw = CONV_WIDTH
    gb = u[:, 0:cw]
    gc = u[:, cw:2 * cw]
    xa = u[:, 2 * cw:3 * cw]
    xb = u[:, 3 * cw:3 * cw + LRU_WIDTH]
    gate = u[:, 3 * cw + LRU_WIDTH:]

    conv_z, ztail[...] = _causal_conv(gc * xa, ztail[...], ca_ref[...], state_at)
    y_a = gb * conv_z

    conv_xb, xbtail[...] = _causal_conv(xb, xbtail[...], cb_ref[...], state_at)
    xc = conv_xb + cbb_ref[...]

    xcb = xc.astype(BF16)
    r = _sigmoid(_dot(xcb, rw_ref[...]) + rb_ref[...])
    i = _sigmoid(_dot(xcb, iw_ref[...]) + ib_ref[...])
    nlam = -lam_ref[...]
    softplus = jnp.maximum(nlam, 0.0) + jnp.log1p(jnp.exp(-jnp.abs(nlam)))
    log_a = -LRU_C * r * softplus
    a = jnp.exp(log_a)
    th = jnp.tanh(log_a)
    mult = jnp.sqrt(-2.0 * th / (1.0 - th))
    h, hstate[...] = _lru_scan(a, mult * (i * xc), hstate[...], state_at)

    y_b = _gelu_tanh(gate) * h
    y = jnp.concatenate([y_a, y_b], axis=-1).astype(BF16)
    o_ref[0] = x_ref[0] + _dot(y, wout_ref[...])
    if state_at is not None:
        zout_ref, xbout_ref, hout_ref = refs[16:19]
        zout_ref[...] = ztail[...]
        xbout_ref[...] = xbtail[...]
        hout_ref[...] = hstate[...]


def _const_spec(shape):
    nd = len(shape)
    return pl.BlockSpec(shape, lambda b, t: (0,) * nd, pipeline_mode=pl.Buffered(1))


def _shapes(arrays):
    return tuple(jax.ShapeDtypeStruct(a.shape, a.dtype) for a in arrays)


def _even_layer(h, state, g, w_in, conv_a, conv_b, conv_b_bias, rw, rb, iw, ib, lam, w_out,
                state_at=None):
    bsz, tlen, _ = h.shape
    tile = min(STEP_T, tlen)
    row_spec = pl.BlockSpec((1, tile, D_MODEL), lambda b, t: (b, t, 0))
    args = tuple(state) + (
        g.reshape(1, D_MODEL), w_in, conv_a, conv_b,
        conv_b_bias.reshape(1, LRU_WIDTH), rw, rb.reshape(1, LRU_WIDTH), iw,
        ib.reshape(1, LRU_WIDTH), lam.reshape(1, LRU_WIDTH), w_out)
    out_shape = [jax.ShapeDtypeStruct(h.shape, F32)]
    out_specs = [row_spec]
    if state_at is not None:
        assert bsz == 1 and tlen == tile
        out_shape += list(_shapes(state))
        out_specs += [_const_spec(s.shape) for s in state]
    res = pl.pallas_call(
        functools.partial(_even_kernel, state_at=state_at),
        grid=(bsz, tlen // tile),
        in_specs=[row_spec] + [_const_spec(a.shape) for a in args],
        out_specs=out_specs,
        out_shape=out_shape,
        scratch_shapes=[pltpu.VMEM(s.shape, F32) for s in state],
        compiler_params=pltpu.CompilerParams(
            dimension_semantics=("arbitrary", "arbitrary"), vmem_limit_bytes=VMEM_LIMIT),
        name="even_mixer",
    )(h, *args)
    return res[0], (tuple(res[1:]) if state_at is not None else None)


def _block_diag(w):
    nh, d, _ = w.shape
    eye = jnp.eye(nh, dtype=w.dtype)
    return jnp.einsum("hij,hg->higj", w, eye).reshape(nh * d, nh * d)


def _ffn_kernel(*refs, attn_input, final_norm, state_at):
    refs = list(refs)
    x_ref, tin_ref = refs.pop(0), refs.pop(0)
    attn_ref, wo_ref = (refs.pop(0), refs.pop(0)) if attn_input else (None, None)
    g_ref, wup_ref, cw_ref, cb_ref, wdn_ref = refs[:5]
    del refs[:5]
    fg_ref = refs.pop(0) if final_norm else None
    o_ref = refs.pop(0)
    tout_ref = refs.pop(0) if state_at is not None else None
    hn_ref, act_ref, tail = refs
    t = pl.program_id(1)

    @pl.when(t == 0)
    def _():
        tail[...] = tin_ref[...]

    if attn_input:
        wg = attn_ref.shape[3]
        y = x_ref[0]
        for grp in range(attn_ref.shape[1]):
            y = y + _dot(attn_ref[0, grp], wo_ref[grp * wg:(grp + 1) * wg, :])
        o_ref[0] = y
    else:
        o_ref[0] = x_ref[0]
    hn_ref[...] = _rms(o_ref[0], g_ref[...]).astype(BF16)
    for c in range(D_FF // FF_CHUNK):
        halves = []
        for part in range(2):
            col = part * D_FF + c * FF_CHUNK
            cs = slice(col, col + FF_CHUNK)
            u = _dot(hn_ref[...], wup_ref[:, cs])
            conv_u, tail[:, cs] = _causal_conv(u, tail[:, cs], cw_ref[:, cs], state_at)
            halves.append(conv_u + cb_ref[:, cs])
        a, gte = halves
        act_ref[:, c * FF_CHUNK:(c + 1) * FF_CHUNK] = (a * _sigmoid(a) * gte).astype(BF16)
    y = o_ref[0] + _dot(act_ref[...], wdn_ref[...])
    o_ref[0] = _rms(y, fg_ref[...]) if final_norm else y
    if state_at is not None:
        tout_ref[...] = tail[...]


def _ffn_layer(h, tail, g, w_up, conv_w, conv_b, w_down, attn=None, final_g=None, state_at=None):
    bsz, tlen, _ = h.shape
    tile = min(STEP_T, tlen)
    row_spec = pl.BlockSpec((1, tile, D_MODEL), lambda b, t: (b, t, 0))
    args = (g.reshape(1, D_MODEL), w_up, conv_w, conv_b.reshape(1, 2 * D_FF), w_down)
    if final_g is not None:
        args = args + (final_g.reshape(1, D_MODEL),)
    specs = [_const_spec(a.shape) for a in args]
    if attn is not None:
        args = attn + args
        attn_spec = pl.BlockSpec((1, attn[0].shape[1], tile, attn[0].shape[3]),
                                 lambda b, t: (b, 0, t, 0))
        specs = [attn_spec, _const_spec(attn[1].shape)] + specs
    out_shape = [jax.ShapeDtypeStruct(h.shape, F32)]
    out_specs = [row_spec]
    if state_at is not None:
        assert bsz == 1 and tlen == tile
        out_shape.append(jax.ShapeDtypeStruct(tail.shape, F32))
        out_specs.append(_const_spec(tail.shape))
    res = pl.pallas_call(
        functools.partial(_ffn_kernel, attn_input=attn is not None,
                          final_norm=final_g is not None, state_at=state_at),
        grid=(bsz, tlen // tile),
        in_specs=[row_spec, _const_spec(tail.shape)] + specs,
        out_specs=out_specs,
        out_shape=out_shape,
        scratch_shapes=[
            pltpu.VMEM((tile, D_MODEL), BF16),
            pltpu.VMEM((tile, D_FF), BF16),
            pltpu.VMEM(tail.shape, F32),
        ],
        compiler_params=pltpu.CompilerParams(
            dimension_semantics=("arbitrary", "arbitrary"), vmem_limit_bytes=VMEM_LIMIT),
        name="conv_ffn",
    )(h, tail, *args)
    return res[0], (res[1] if state_at is not None else None)


def _mla_proj_kernel(x_ref, g_ref, win_ref, qn_ref, kvn_ref, wqt_ref, wk_ref, wvt_ref,
                     cosk_ref, sink_ref, cosq_ref, sinq_ref, qt_ref, k_ref, vt_ref):
    x = x_ref[0]
    hn = _rms(x, g_ref[...]).astype(BF16)
    u = _dot(hn, win_ref[...])
    cq = u[:, 0:Q_LORA]
    ckv = u[:, Q_LORA:Q_LORA + KV_LORA]
    kr = u[:, Q_LORA + KV_LORA:Q_LORA + KV_LORA + HEAD_PAD]
    kr_rot = u[:, Q_LORA + KV_LORA + HEAD_PAD:]
    cqn = _rms(cq, qn_ref[...]).astype(BF16)
    ckvn = _rms(ckv, kvn_ref[...]).astype(BF16)

    qt = _dot_nt(wqt_ref[...], cqn)
    kn = _dot(ckvn, wk_ref[...])
    vt = _dot_nt(wvt_ref[...], ckvn)
    k_rope = kr * cosk_ref[...] + kr_rot * sink_ref[...]
    cosq = cosq_ref[...]
    sinq = sinq_ref[...]
    scale = QK_HEAD ** -0.5 * math.log2(math.e)
    zeros = jnp.zeros((HEAD_PAD - QK_HEAD, qt.shape[1]), F32)
    ones = jnp.ones((V_ROWS - V_HEAD, qt.shape[1]), F32)
    for h in range(MLA_HEADS):
        qh = qt[h * HEAD_PAD:(h + 1) * HEAD_PAD, :]
        roped = qh[QK_NOPE:QK_HEAD, :] * cosq + qh[QK_HEAD:, :] * sinq
        q_out = jnp.concatenate(
            [qh[0:QK_NOPE, :] * scale, roped * scale, zeros], axis=0).astype(BF16)
        k_out = (kn[:, h * HEAD_PAD:(h + 1) * HEAD_PAD] + k_rope).astype(BF16)
        v_out = jnp.concatenate(
            [vt[h * V_HEAD:(h + 1) * V_HEAD, :], ones], axis=0).astype(BF16)
        for i in range(qt_ref.shape[2]):
            rows = slice(i * TILE_T, (i + 1) * TILE_T)
            qt_ref[0, h, i] = q_out[:, rows]
            k_ref[0, h, i] = k_out[rows, :]
            vt_ref[0, h, i] = v_out[:, rows]


def _rot_cols(w):
    half = QK_ROPE // 2
    return jnp.concatenate([-w[..., half:], w[..., :half]], axis=-1)


def _mla_weights(w_in, w_uq, w_ukv):
    w_kr = w_in[:, Q_LORA + KV_LORA:]
    padl = jnp.zeros((D_MODEL, QK_NOPE), F32)
    padr = jnp.zeros((D_MODEL, HEAD_PAD - QK_HEAD), F32)
    win_ext = jnp.concatenate(
        [w_in[:, :Q_LORA + KV_LORA], padl, w_kr, padr, padl, _rot_cols(w_kr), padr],
        axis=-1).astype(BF16)
    wq = w_uq.reshape(Q_LORA, MLA_HEADS, QK_HEAD)
    wq_ext = jnp.concatenate([wq, _rot_cols(wq[..., QK_NOPE:])], axis=-1)
    wqt = wq_ext.reshape(Q_LORA, MLA_HEADS * HEAD_PAD).T.astype(BF16)
    wkv = w_ukv.reshape(KV_LORA, MLA_HEADS, QK_NOPE + V_HEAD)
    wk = jnp.concatenate(
        [wkv[..., :QK_NOPE], jnp.zeros((KV_LORA, MLA_HEADS, HEAD_PAD - QK_NOPE), F32)],
        axis=-1).reshape(KV_LORA, MLA_HEADS * HEAD_PAD).astype(BF16)
    wvt = wkv[..., QK_NOPE:].reshape(KV_LORA, MLA_HEADS * V_HEAD).T.astype(BF16)
    return win_ext, wqt, wk, wvt


def _rope_tables(first_pos, t_len):
    pos = first_pos + jnp.arange(t_len, dtype=F32)
    inv_freq = ROPE_BASE ** (-jnp.arange(0, QK_ROPE, 2, dtype=F32) / QK_ROPE)
    ang = _to_strided(pos[:, None] * inv_freq[None, :], 0)
    cos2 = jnp.concatenate([jnp.cos(ang)] * 2, axis=-1)
    sin2 = jnp.concatenate([jnp.sin(ang)] * 2, axis=-1)
    lpad = ((0, 0), (QK_NOPE, HEAD_PAD - QK_HEAD))
    return jnp.pad(cos2, lpad), jnp.pad(sin2, lpad), cos2.T, sin2.T


def _mla_proj(h, tables, g, q_norm, kv_norm, weights):
    bsz, tlen, _ = h.shape
    tile = TILE_T
    nt = tlen // tile
    step = min(STEP_T, tlen)
    per = step // tile
    win_ext, wqt, wk, wvt = weights
    args = (g.reshape(1, D_MODEL), win_ext, q_norm.reshape(1, Q_LORA),
            kv_norm.reshape(1, KV_LORA), wqt, wk, wvt)
    row_spec = pl.BlockSpec((1, step, D_MODEL), lambda b, t: (b, t, 0))
    in_specs = [row_spec] + [_const_spec(a.shape) for a in args] + [
        pl.BlockSpec((step, HEAD_PAD), lambda b, t: (t, 0)),
        pl.BlockSpec((step, HEAD_PAD), lambda b, t: (t, 0)),
        pl.BlockSpec((QK_ROPE, step), lambda b, t: (0, t)),
        pl.BlockSpec((QK_ROPE, step), lambda b, t: (0, t)),
    ]
    out_shape = (
        jax.ShapeDtypeStruct((bsz, MLA_HEADS, nt, HEAD_PAD, tile), BF16),
        jax.ShapeDtypeStruct((bsz, MLA_HEADS, nt, tile, HEAD_PAD), BF16),
        jax.ShapeDtypeStruct((bsz, MLA_HEADS, nt, V_ROWS, tile), BF16),
    )
    out_specs = (
        pl.BlockSpec((1, MLA_HEADS, per, HEAD_PAD, tile), lambda b, t: (b, 0, t, 0, 0)),
        pl.BlockSpec((1, MLA_HEADS, per, tile, HEAD_PAD), lambda b, t: (b, 0, t, 0, 0)),
        pl.BlockSpec((1, MLA_HEADS, per, V_ROWS, tile), lambda b, t: (b, 0, t, 0, 0)),
    )
    return pl.pallas_call(
        _mla_proj_kernel,
        grid=(bsz, tlen // step),
        in_specs=in_specs,
        out_specs=out_specs,
        out_shape=out_shape,
        compiler_params=pltpu.CompilerParams(
            dimension_semantics=("arbitrary", "arbitrary"), vmem_limit_bytes=VMEM_LIMIT),
        name="mla_proj",
    )(h, *args, *tables)


def _strided_time(i):
    return (i & (SUBLANES - 1)) * GROUPS + (i >> 3)


def _attn_kernel(*refs, has_meta):
    if has_meta:
        qt_ref, k_ref, vt_ref, km_ref, vtm_ref, o_ref, s_ref, smax_ref = refs
    else:
        qt_ref, k_ref, vt_ref, o_ref, s_ref, smax_ref = refs
    tile = qt_ref.shape[4]
    qi = pl.program_id(2)
    neg = jnp.finfo(F32).min
    heads = range(HEADS_PER_STEP)

    def softmax_update(state, scores, vt, keep=None, block_max=None):
        def block():
            s = scores()
            return s if keep is None else jnp.where(keep, s, neg)
        m, acc = state
        if block_max is None:
            block_max = jnp.max(block(), axis=0, keepdims=True)
        m_new = jnp.maximum(m, block_max)
        p = jnp.exp2(block() - m_new).astype(BF16)
        return m_new, jnp.exp2(m - m_new) * acc + _dot(vt, p)

    def produce(slot, j, hh):
        s = _dot(k_ref[0, hh, j], qt_ref[0, hh, 0])
        s_ref[slot, hh] = s
        smax_ref[slot, hh] = jnp.max(s, axis=0, keepdims=True)

    def consume(carry, slot, j, next_j, keep=None):
        if next_j is not None:
            for hh in range(SCORE_LEAD):
                produce(1 - slot, next_j, hh)
        out = []
        for hh in heads:
            out.append(softmax_update(
                carry[hh], lambda: s_ref[slot, hh], vt_ref[0, hh, j], keep,
                smax_ref[slot, hh] if keep is None else None))
            if next_j is not None and hh + SCORE_LEAD < HEADS_PER_STEP:
                produce(1 - slot, next_j, hh + SCORE_LEAD)
        return tuple(out)

    carry = tuple((jnp.full((1, tile), neg, F32), jnp.zeros((V_ROWS, tile), F32))
                  for _ in heads)
    if has_meta:
        meta_scores = [_dot(km_ref[hh], qt_ref[0, hh, 0]) for hh in heads]
    for hh in heads:
        produce(0, 0, hh)
    if has_meta:
        carry = tuple(softmax_update(carry[hh], lambda: meta_scores[hh], vtm_ref[hh])
                      for hh in heads)

    def pair(jj, c):
        c = consume(c, 0, 2 * jj, 2 * jj + 1)
        return consume(c, 1, 2 * jj + 1, 2 * jj + 2)

    carry = lax.fori_loop(0, qi // 2, pair, carry)
    kpos = _strided_time(lax.broadcasted_iota(jnp.int32, (tile, tile), 0))
    qpos = _strided_time(lax.broadcasted_iota(jnp.int32, (tile, tile), 1))
    keep = kpos <= qpos
    carry = lax.cond(
        qi % 2 == 1,
        lambda c: consume(consume(c, 0, qi - 1, qi), 1, qi, None, keep),
        lambda c: consume(c, 0, qi, None, keep), carry)
    outs = [acc[0:V_HEAD, :] / acc[V_HEAD:V_HEAD + 1, :] for _, acc in carry]
    o_ref[0, 0] = jnp.concatenate(outs, axis=0).T.astype(o_ref.dtype)


def _attention(qt, k5, vt5, meta_kv=None):
    bsz, nh, nt, tile, _ = k5.shape
    tlen = nt * tile
    hps = HEADS_PER_STEP
    in_specs = [
        pl.BlockSpec((1, hps, 1, HEAD_PAD, tile), lambda b, h, q: (b, h, q, 0, 0)),
        pl.BlockSpec((1, hps, nt, tile, HEAD_PAD), lambda b, h, q: (b, h, 0, 0, 0)),
        pl.BlockSpec((1, hps, nt, V_ROWS, tile), lambda b, h, q: (b, h, 0, 0, 0)),
    ]
    args = (qt, k5, vt5)
    if meta_kv is not None:
        in_specs += [pl.BlockSpec((hps,) + a.shape[1:], lambda b, h, q: (h, 0, 0))
                     for a in meta_kv]
        args += tuple(meta_kv)
    return pl.pallas_call(
        functools.partial(_attn_kernel, has_meta=meta_kv is not None),
        grid=(bsz, nh // hps, nt),
        in_specs=in_specs,
        out_specs=pl.BlockSpec((1, 1, tile, hps * V_HEAD), lambda b, h, q: (b, h, q, 0)),
        out_shape=jax.ShapeDtypeStruct((bsz, nh // hps, tlen, hps * V_HEAD), BF16),
        scratch_shapes=[pltpu.VMEM((2, hps, tile, tile), F32),
                        pltpu.VMEM((2, hps, 1, tile), F32)],
        compiler_params=pltpu.CompilerParams(
            dimension_semantics=("arbitrary", "arbitrary", "arbitrary"),
            vmem_limit_bytes=VMEM_LIMIT),
        name="mla_attention",
    )(*args)


def _to_strided(a, axis):
    shp = a.shape
    nt = shp[axis] // TILE_T
    a = a.reshape(shp[:axis] + (nt, SUBLANES, GROUPS) + shp[axis + 1:])
    return jnp.swapaxes(a, axis + 1, axis + 2).reshape(shp)


def _from_strided(a, axis):
    shp = a.shape
    nt = shp[axis] // TILE_T
    a = a.reshape(shp[:axis] + (nt, GROUPS, SUBLANES) + shp[axis + 1:])
    return jnp.swapaxes(a, axis + 1, axis + 2).reshape(shp)


def _trunk(h, first_pos, states, p, state_at):
    depth = len(p["ffn"])
    tables = _rope_tables(first_pos, h.shape[1])
    new_states = []
    for layer in range(depth):
        st = states[layer]
        ffn = p["ffn"][layer]
        attn = None
        if layer % 2 == 0:
            h, mix_state = _even_layer(h, st["mix"], *p["even"][layer // 2], state_at=state_at)
        else:
            g, q_norm, kv_norm, weights, w_out = p["odd"][layer // 2]
            qt, k5, vt5 = _mla_proj(h, tables, g, q_norm, kv_norm, weights)
            attn = (_attention(qt, k5, vt5, st["mix"]), w_out)
            mix_state = None
            if state_at is not None:
                n = state_at + 1
                assert n <= GROUPS
                mix_state = (k5[0, :, 0, 0:n * SUBLANES:SUBLANES, :],
                             vt5[0, :, 0, :, 0:n * SUBLANES:SUBLANES])
        h, tail = _ffn_layer(h, st["ffn"], *ffn, attn=attn,
                             final_g=p["final"] if layer == depth - 1 else None,
                             state_at=state_at)
        new_states.append({"mix": mix_state, "ffn": tail})
    return h, new_states


def kernel(x, meta_tokens, ev_norm, ev_w_in, ev_conv_a, ev_conv_b, ev_conv_b_bias, ev_gate_r_w, ev_gate_r_b, ev_gate_i_w, ev_gate_i_b, ev_lru_lambda, ev_w_out, od_norm, od_w_in, od_q_norm, od_kv_norm, od_w_uq, od_w_ukv, od_w_out, ffn_norm, ffn_w_up, ffn_conv_w, ffn_conv_b, ffn_w_down, final_norm):
    bsz, seq, _ = x.shape
    depth = ffn_norm.shape[0]
    assert seq % TILE_T == 0 and N_META <= GROUPS
    params = {
        "even": [(ev_norm[j], ev_w_in[j].astype(BF16), ev_conv_a[j], ev_conv_b[j],
                  ev_conv_b_bias[j], _block_diag(ev_gate_r_w[j]).astype(BF16), ev_gate_r_b[j],
                  _block_diag(ev_gate_i_w[j]).astype(BF16), ev_gate_i_b[j], ev_lru_lambda[j],
                  ev_w_out[j].astype(BF16)) for j in range(ev_norm.shape[0])],
        "odd": [(od_norm[j], od_q_norm[j], od_kv_norm[j],
                 _mla_weights(od_w_in[j], od_w_uq[j], od_w_ukv[j]), od_w_out[j].astype(BF16))
                for j in range(od_norm.shape[0])],
        "ffn": [(ffn_norm[l], ffn_w_up[l].astype(BF16), ffn_conv_w[l], ffn_conv_b[l],
                 ffn_w_down[l].astype(BF16)) for l in range(depth)],
        "final": final_norm,
    }
    zero_states = []
    for layer in range(depth):
        mix = None
        if layer % 2 == 0:
            mix = (jnp.zeros(((ev_conv_a.shape[1] - 1) * SUBLANES, CONV_WIDTH), F32),
                   jnp.zeros(((ev_conv_b.shape[1] - 1) * SUBLANES, LRU_WIDTH), F32),
                   jnp.zeros((1, LRU_WIDTH), F32))
        zero_states.append(
            {"mix": mix, "ffn": jnp.zeros(((ffn_conv_w.shape[1] - 1) * SUBLANES, 2 * D_FF), F32)})

    meta_tile = jnp.concatenate(
        [meta_tokens.astype(x.dtype), jnp.zeros((TILE_T - N_META, D_MODEL), x.dtype)], axis=0)
    _, meta_states = _trunk(_to_strided(meta_tile[None], 1), 0.0, zero_states, params,
                            state_at=N_META - 1)
    h, _ = _trunk(_to_strided(x, 1), float(N_META), meta_states, params, state_at=None)
    return _from_strided(h, 1)
```

```python
import functools
import math

import jax
import jax.numpy as jnp
from jax import lax
from jax.experimental import pallas as pl
from jax.experimental.pallas import tpu as pltpu

D_MODEL = 1024
N_META = 16
EPS = 1e-6
CONV_WIDTH = 512
LRU_WIDTH = 512
LRU_C = 8.0
MLA_HEADS = 16
QK_NOPE = 64
QK_ROPE = 32
QK_HEAD = QK_NOPE + QK_ROPE
V_HEAD = 64
Q_LORA = 384
KV_LORA = 256
ROPE_BASE = 10000.0
D_FF = 2816

LANES = 128
SUBLANES = 8
HEAD_PAD = 128
V_ROWS = V_HEAD + 16
HEADS_PER_STEP = 8
SCORE_LEAD = 1
TILE_T = 256
GROUPS = TILE_T // SUBLANES
STEP_T = 1024
FF_CHUNK = 256
VMEM_LIMIT = 56 * 1024 * 1024

F32 = jnp.float32
BF16 = jnp.bfloat16


def _rms(x, g):
    ms = jnp.mean(x * x, axis=-1, keepdims=True)
    return x * lax.rsqrt(ms + EPS) * g


def _sigmoid(x):
    return 1.0 / (1.0 + jnp.exp(-x))


def _gelu_tanh(x):
    c = math.sqrt(2.0 / math.pi)
    return x * (0.5 * (1.0 + jnp.tanh(c * (x + 0.044715 * (x * x * x)))))


def _dot(a, b):
    return jnp.dot(a, b, preferred_element_type=F32)


def _dot_nt(a, b):
    return lax.dot_general(a, b, (((1,), (1,)), ((), ())), preferred_element_type=F32)


def _row_groups(x):
    return [x[v * SUBLANES:(v + 1) * SUBLANES, :] for v in range(x.shape[0] // SUBLANES)]


def _delays(u, prev, kmax):
    tile = u.shape[0]
    first = lax.broadcasted_iota(jnp.int32, (SUBLANES, u.shape[1]), 0) == 0
    wrapped = []
    for i in range(kmax):
        cur = u[tile - (kmax - i) * SUBLANES:tile - (kmax - i - 1) * SUBLANES, :]
        old = prev[i * SUBLANES:(i + 1) * SUBLANES, :]
        wrapped.append(jnp.where(first, pltpu.roll(old, 1, axis=0), pltpu.roll(cur, 1, axis=0)))
    return [jnp.concatenate(wrapped[kmax - k:] + [u[:tile - k * SUBLANES, :]], axis=0)
            for k in range(1, kmax + 1)]


def _tiles(x):
    return [x[i * TILE_T:(i + 1) * TILE_T, :] for i in range(x.shape[0] // TILE_T)]


def _stack(parts):
    return parts[0] if len(parts) == 1 else jnp.concatenate(parts, axis=0)


def _causal_conv(u, prev, taps, state_at=None):
    k = taps.shape[0]
    outs = []
    for ui in _tiles(u):
        delayed = _delays(ui, prev, k - 1)
        acc = delayed[k - 2] * taps[0:1, :]
        for j in range(1, k - 1):
            acc = acc + delayed[k - 2 - j] * taps[j:j + 1, :]
        outs.append(acc + ui * taps[k - 1:k, :])
        prev = _conv_tail(ui, k - 1, state_at)
    return _stack(outs), prev


def _conv_tail(u, kmax, state_at):
    tile = u.shape[0]
    if state_at is None:
        return u[tile - kmax * SUBLANES:, :]
    s, v = divmod(state_at, GROUPS)
    assert v >= kmax - 1
    rows = []
    for i in range(kmax):
        grp = u[(v - (kmax - 1 - i)) * SUBLANES:(v - (kmax - 2 - i)) * SUBLANES, :]
        rows.append(pltpu.roll(grp, SUBLANES - 1 - s, axis=0) if s != SUBLANES - 1 else grp)
    return jnp.concatenate(rows, axis=0)


def _lru_scan(a, b, carry, state_at=None):
    outs = []
    for ai, bi in zip(_tiles(a), _tiles(b)):
        hi, carry = _lru_scan_tile(ai, bi, carry, state_at)
        outs.append(hi)
    return _stack(outs), carry


def _lru_scan_tile(a, b, carry, state_at):
    a_rows, b_rows = _row_groups(a), _row_groups(b)
    prod, hzero = [a_rows[0]], [b_rows[0]]
    for v in range(1, len(a_rows)):
        prod.append(a_rows[v] * prod[-1])
        hzero.append(a_rows[v] * hzero[-1] + b_rows[v])
    pa, ph = prod[-1], hzero[-1]
    sub = lax.broadcasted_iota(jnp.int32, pa.shape, 0)
    for d in (1, 2, 4):
        ok = sub >= d
        ph = jnp.where(ok, pa * pltpu.roll(ph, d, axis=0) + ph, ph)
        pa = jnp.where(ok, pa * pltpu.roll(pa, d, axis=0), pa)
    ends = ph + pa * carry
    init = jnp.where(sub == 0, carry, pltpu.roll(ends, 1, axis=0))
    h = jnp.concatenate([hz + pr * init for hz, pr in zip(hzero, prod)], axis=0)
    if state_at is None:
        return h, ends[SUBLANES - 1:SUBLANES, :]
    s, v = divmod(state_at, GROUPS)
    row = v * SUBLANES + s
    return h, h[row:row + 1, :]


def _even_kernel(*refs, state_at):
    (x_ref, zin_ref, xbin_ref, hin_ref, g_ref, win_ref, ca_ref, cb_ref, cbb_ref, rw_ref, rb_ref,
     iw_ref, ib_ref, lam_ref, wout_ref, o_ref) = refs[:16]
    ztail, xbtail, hstate = refs[-3:]
    t = pl.program_id(1)

    @pl.when(t == 0)
    def _():
        ztail[...] = zin_ref[...]
        xbtail[...] = xbin_ref[...]
        hstate[...] = hin_ref[...]

    hn = _rms(x_ref[0], g_ref[...]).astype(BF16)
    u = _dot(hn, win_ref[...])
    cw = CONV_WIDTH
    gb = u[:, 0:cw]
    gc = u[:, cw:2 * cw]
    xa = u[:, 2 * cw:3 * cw]
    xb = u[:, 3 * cw:3 * cw + LRU_WIDTH]
    gate = u[:, 3 * cw + LRU_WIDTH:]

    conv_z, ztail[...] = _causal_conv(gc * xa, ztail[...], ca_ref[...], state_at)
    y_a = gb * conv_z

    conv_xb, xbtail[...] = _causal_conv(xb, xbtail[...], cb_ref[...], state_at)
    xc = conv_xb + cbb_ref[...]

    xcb = xc.astype(BF16)
    r = _sigmoid(_dot(xcb, rw_ref[...]) + rb_ref[...])
    i = _sigmoid(_dot(xcb, iw_ref[...]) + ib_ref[...])
    nlam = -lam_ref[...]
    softplus = jnp.maximum(nlam, 0.0) + jnp.log1p(jnp.exp(-jnp.abs(nlam)))
    log_a = -LRU_C * r * softplus
    a = jnp.exp(log_a)
    th = jnp.tanh(log_a)
    mult = jnp.sqrt(-2.0 * th / (1.0 - th))
    h, hstate[...] = _lru_scan(a, mult * (i * xc), hstate[...], state_at)

    y_b = _gelu_tanh(gate) * h
    y = jnp.concatenate([y_a, y_b], axis=-1).astype(BF16)
    o_ref[0] = x_ref[0] + _dot(y, wout_ref[...])
    if state_at is not None:
        zout_ref, xbout_ref, hout_ref = refs[16:19]
        zout_ref[...] = ztail[...]
        xbout_ref[...] = xbtail[...]
        hout_ref[...] = hstate[...]


def _const_spec(shape):
    nd = len(shape)
    return pl.BlockSpec(shape, lambda b, t: (0,) * nd, pipeline_mode=pl.Buffered(1))


def _shapes(arrays):
    return tuple(jax.ShapeDtypeStruct(a.shape, a.dtype) for a in arrays)


def _even_layer(h, state, g, w_in, conv_a, conv_b, conv_b_bias, rw, rb, iw, ib, lam, w_out,
                state_at=None):
    bsz, tlen, _ = h.shape
    tile = min(STEP_T, tlen)
    row_spec = pl.BlockSpec((1, tile, D_MODEL), lambda b, t: (b, t, 0))
    args = tuple(state) + (
        g.reshape(1, D_MODEL), w_in, conv_a, conv_b,
        conv_b_bias.reshape(1, LRU_WIDTH), rw, rb.reshape(1, LRU_WIDTH), iw,
        ib.reshape(1, LRU_WIDTH), lam.reshape(1, LRU_WIDTH), w_out)
    out_shape = [jax.ShapeDtypeStruct(h.shape, F32)]
    out_specs = [row_spec]
    if state_at is not None:
        assert bsz == 1 and tlen == tile
        out_shape += list(_shapes(state))
        out_specs += [_const_spec(s.shape) for s in state]
    res = pl.pallas_call(
        functools.partial(_even_kernel, state_at=state_at),
        grid=(bsz, tlen // tile),
        in_specs=[row_spec] + [_const_spec(a.shape) for a in args],
        out_specs=out_specs,
        out_shape=out_shape,
        scratch_shapes=[pltpu.VMEM(s.shape, F32) for s in state],
        compiler_params=pltpu.CompilerParams(
            dimension_semantics=("arbitrary", "arbitrary"), vmem_limit_bytes=VMEM_LIMIT),
        name="even_mixer",
    )(h, *args)
    return res[0], (tuple(res[1:]) if state_at is not None else None)


def _block_diag(w):
    nh, d, _ = w.shape
    eye = jnp.eye(nh, dtype=w.dtype)
    return jnp.einsum("hij,hg->higj", w, eye).reshape(nh * d, nh * d)


def _ffn_kernel(*refs, attn_input, final_norm, state_at):
    refs = list(refs)
    x_ref, tin_ref = refs.pop(0), refs.pop(0)
    attn_ref, wo_ref = (refs.pop(0), refs.pop(0)) if attn_input else (None, None)
    g_ref, wup_ref, cw_ref, cb_ref, wdn_ref = refs[:5]
    del refs[:5]
    fg_ref = refs.pop(0) if final_norm else None
    o_ref = refs.pop(0)
    tout_ref = refs.pop(0) if state_at is not None else None
    slab_ref = refs.pop() if final_norm else None
    hn_ref, act_ref, tail = refs
    t = pl.program_id(1)

    @pl.when(t == 0)
    def _():
        tail[...] = tin_ref[...]

    if attn_input:
        wg = attn_ref.shape[3]
        y = x_ref[0]
        for grp in range(attn_ref.shape[1]):
            y = y + _dot(attn_ref[0, grp], wo_ref[grp * wg:(grp + 1) * wg, :])
        o_ref[0] = y
    else:
        o_ref[0] = x_ref[0]
    hn_ref[...] = _rms(o_ref[0], g_ref[...]).astype(BF16)
    for c in range(D_FF // FF_CHUNK):
        halves = []
        for part in range(2):
            col = part * D_FF + c * FF_CHUNK
            cs = slice(col, col + FF_CHUNK)
            u = _dot(hn_ref[...], wup_ref[:, cs])
            conv_u, tail[:, cs] = _causal_conv(u, tail[:, cs], cw_ref[:, cs], state_at)
            halves.append(conv_u + cb_ref[:, cs])
        a, gte = halves
        act_ref[:, c * FF_CHUNK:(c + 1) * FF_CHUNK] = (a * _sigmoid(a) * gte).astype(BF16)
    y = o_ref[0] + _dot(act_ref[...], wdn_ref[...])
    if final_norm:
        y = _rms(y, fg_ref[...])
        slabs = range(D_MODEL // LANES)
        for c in slabs:
            slab_ref[c] = y[:, c * LANES:(c + 1) * LANES]
        for n in range(y.shape[0] // TILE_T):
            for s in range(SUBLANES):
                src = pl.ds(n * TILE_T + s, GROUPS, stride=SUBLANES)
                dst = pl.ds(n * TILE_T + s * GROUPS, GROUPS)
                for c in slabs:
                    o_ref[0, dst, c * LANES:(c + 1) * LANES] = slab_ref[c, src, :]
    else:
        o_ref[0] = y
    if state_at is not None:
        tout_ref[...] = tail[...]


def _ffn_layer(h, tail, g, w_up, conv_w, conv_b, w_down, attn=None, final_g=None, state_at=None):
    bsz, tlen, _ = h.shape
    tile = min(STEP_T, tlen)
    row_spec = pl.BlockSpec((1, tile, D_MODEL), lambda b, t: (b, t, 0))
    args = (g.reshape(1, D_MODEL), w_up, conv_w, conv_b.reshape(1, 2 * D_FF), w_down)
    if final_g is not None:
        args = args + (final_g.reshape(1, D_MODEL),)
    specs = [_const_spec(a.shape) for a in args]
    if attn is not None:
        args = attn + args
        attn_spec = pl.BlockSpec((1, attn[0].shape[1], tile, attn[0].shape[3]),
                                 lambda b, t: (b, 0, t, 0))
        specs = [attn_spec, _const_spec(attn[1].shape)] + specs
    out_shape = [jax.ShapeDtypeStruct(h.shape, F32)]
    out_specs = [row_spec]
    if state_at is not None:
        assert bsz == 1 and tlen == tile
        out_shape.append(jax.ShapeDtypeStruct(tail.shape, F32))
        out_specs.append(_const_spec(tail.shape))
    res = pl.pallas_call(
        functools.partial(_ffn_kernel, attn_input=attn is not None,
                          final_norm=final_g is not None, state_at=state_at),
        grid=(bsz, tlen // tile),
        in_specs=[row_spec, _const_spec(tail.shape)] + specs,
        out_specs=out_specs,
        out_shape=out_shape,
        scratch_shapes=[
            pltpu.VMEM((tile, D_MODEL), BF16),
            pltpu.VMEM((tile, D_FF), BF16),
            pltpu.VMEM(tail.shape, F32),
        ] + ([pltpu.VMEM((D_MODEL // LANES, tile, LANES), F32)] if final_g is not None else []),
        compiler_params=pltpu.CompilerParams(
            dimension_semantics=("arbitrary", "arbitrary"), vmem_limit_bytes=VMEM_LIMIT),
        name="conv_ffn",
    )(h, tail, *args)
    return res[0], (res[1] if state_at is not None else None)


def _mla_proj_kernel(x_ref, g_ref, win_ref, qn_ref, kvn_ref, wqt_ref, wk_ref, wvt_ref,
                     cosk_ref, sink_ref, cosq_ref, sinq_ref, qt_ref, k_ref, vt_ref):
    x = x_ref[0]
    hn = _rms(x, g_ref[...]).astype(BF16)
    u = _dot(hn, win_ref[...])
    cq = u[:, 0:Q_LORA]
    ckv = u[:, Q_LORA:Q_LORA + KV_LORA]
    kr = u[:, Q_LORA + KV_LORA:Q_LORA + KV_LORA + HEAD_PAD]
    kr_rot = u[:, Q_LORA + KV_LORA + HEAD_PAD:]
    cqn = _rms(cq, qn_ref[...]).astype(BF16)
    ckvn = _rms(ckv, kvn_ref[...]).astype(BF16)

    qt = _dot_nt(wqt_ref[...], cqn)
    kn = _dot(ckvn, wk_ref[...])
    vt = _dot_nt(wvt_ref[...], ckvn)
    k_rope = kr * cosk_ref[...] + kr_rot * sink_ref[...]
    cosq = cosq_ref[...]
    sinq = sinq_ref[...]
    scale = QK_HEAD ** -0.5 * math.log2(math.e)
    zeros = jnp.zeros((HEAD_PAD - QK_HEAD, qt.shape[1]), F32)
    ones = jnp.ones((V_ROWS - V_HEAD, qt.shape[1]), F32)
    for h in range(MLA_HEADS):
        qh = qt[h * HEAD_PAD:(h + 1) * HEAD_PAD, :]
        roped = qh[QK_NOPE:QK_HEAD, :] * cosq + qh[QK_HEAD:, :] * sinq
        q_out = jnp.concatenate(
            [qh[0:QK_NOPE, :] * scale, roped * scale, zeros], axis=0).astype(BF16)
        k_out = (kn[:, h * HEAD_PAD:(h + 1) * HEAD_PAD] + k_rope).astype(BF16)
        v_out = jnp.concatenate(
            [vt[h * V_HEAD:(h + 1) * V_HEAD, :], ones], axis=0).astype(BF16)
        for i in range(qt_ref.shape[2]):
            rows = slice(i * TILE_T, (i + 1) * TILE_T)
            qt_ref[0, h, i] = q_out[:, rows]
            k_ref[0, h, i] = k_out[rows, :]
            vt_ref[0, h, i] = v_out[:, rows]


def _rot_cols(w):
    half = QK_ROPE // 2
    return jnp.concatenate([-w[..., half:], w[..., :half]], axis=-1)


def _mla_weights(w_in, w_uq, w_ukv):
    w_kr = w_in[:, Q_LORA + KV_LORA:]
    padl = jnp.zeros((D_MODEL, QK_NOPE), F32)
    padr = jnp.zeros((D_MODEL, HEAD_PAD - QK_HEAD), F32)
    win_ext = jnp.concatenate(
        [w_in[:, :Q_LORA + KV_LORA], padl, w_kr, padr, padl, _rot_cols(w_kr), padr],
        axis=-1).astype(BF16)
    wq = w_uq.reshape(Q_LORA, MLA_HEADS, QK_HEAD)
    wq_ext = jnp.concatenate([wq, _rot_cols(wq[..., QK_NOPE:])], axis=-1)
    wqt = wq_ext.reshape(Q_LORA, MLA_HEADS * HEAD_PAD).T.astype(BF16)
    wkv = w_ukv.reshape(KV_LORA, MLA_HEADS, QK_NOPE + V_HEAD)
    wk = jnp.concatenate(
        [wkv[..., :QK_NOPE], jnp.zeros((KV_LORA, MLA_HEADS, HEAD_PAD - QK_NOPE), F32)],
        axis=-1).reshape(KV_LORA, MLA_HEADS * HEAD_PAD).astype(BF16)
    wvt = wkv[..., QK_NOPE:].reshape(KV_LORA, MLA_HEADS * V_HEAD).T.astype(BF16)
    return win_ext, wqt, wk, wvt


def _rope_tables(first_pos, t_len):
    pos = first_pos + jnp.arange(t_len, dtype=F32)
    inv_freq = ROPE_BASE ** (-jnp.arange(0, QK_ROPE, 2, dtype=F32) / QK_ROPE)
    ang = _to_strided(pos[:, None] * inv_freq[None, :], 0)
    cos2 = jnp.concatenate([jnp.cos(ang)] * 2, axis=-1)
    sin2 = jnp.concatenate([jnp.sin(ang)] * 2, axis=-1)
    lpad = ((0, 0), (QK_NOPE, HEAD_PAD - QK_HEAD))
    return jnp.pad(cos2, lpad), jnp.pad(sin2, lpad), cos2.T, sin2.T


def _mla_proj(h, tables, g, q_norm, kv_norm, weights):
    bsz, tlen, _ = h.shape
    tile = TILE_T
    nt = tlen // tile
    step = min(STEP_T, tlen)
    per = step // tile
    win_ext, wqt, wk, wvt = weights
    args = (g.reshape(1, D_MODEL), win_ext, q_norm.reshape(1, Q_LORA),
            kv_norm.reshape(1, KV_LORA), wqt, wk, wvt)
    row_spec = pl.BlockSpec((1, step, D_MODEL), lambda b, t: (b, t, 0))
    in_specs = [row_spec] + [_const_spec(a.shape) for a in args] + [
        pl.BlockSpec((step, HEAD_PAD), lambda b, t: (t, 0)),
        pl.BlockSpec((step, HEAD_PAD), lambda b, t: (t, 0)),
        pl.BlockSpec((QK_ROPE, step), lambda b, t: (0, t)),
        pl.BlockSpec((QK_ROPE, step), lambda b, t: (0, t)),
    ]
    out_shape = (
        jax.ShapeDtypeStruct((bsz, MLA_HEADS, nt, HEAD_PAD, tile), BF16),
        jax.ShapeDtypeStruct((bsz, MLA_HEADS, nt, tile, HEAD_PAD), BF16),
        jax.ShapeDtypeStruct((bsz, MLA_HEADS, nt, V_ROWS, tile), BF16),
    )
    out_specs = (
        pl.BlockSpec((1, MLA_HEADS, per, HEAD_PAD, tile), lambda b, t: (b, 0, t, 0, 0)),
        pl.BlockSpec((1, MLA_HEADS, per, tile, HEAD_PAD), lambda b, t: (b, 0, t, 0, 0)),
        pl.BlockSpec((1, MLA_HEADS, per, V_ROWS, tile), lambda b, t: (b, 0, t, 0, 0)),
    )
    return pl.pallas_call(
        _mla_proj_kernel,
        grid=(bsz, tlen // step),
        in_specs=in_specs,
        out_specs=out_specs,
        out_shape=out_shape,
        compiler_params=pltpu.CompilerParams(
            dimension_semantics=("arbitrary", "arbitrary"), vmem_limit_bytes=VMEM_LIMIT),
        name="mla_proj",
    )(h, *args, *tables)


def _strided_time(i):
    return (i & (SUBLANES - 1)) * GROUPS + (i >> 3)


def _attn_kernel(*refs, has_meta):
    if has_meta:
        qt_ref, k_ref, vt_ref, km_ref, vtm_ref, o_ref, s_ref, smax_ref = refs
    else:
        qt_ref, k_ref, vt_ref, o_ref, s_ref, smax_ref = refs
    tile = qt_ref.shape[4]
    step = pl.program_id(2)
    neg = jnp.finfo(F32).min
    heads = range(HEADS_PER_STEP)

    def softmax_update(state, scores, vt, keep=None, block_max=None):
        def block():
            s = scores()
            return s if keep is None else jnp.where(keep, s, neg)
        m, acc = state
        if block_max is None:
            block_max = jnp.max(block(), axis=0, keepdims=True)
        m_new = jnp.maximum(m, block_max)
        p = jnp.exp2(block() - m_new).astype(BF16)
        return m_new, jnp.exp2(m - m_new) * acc + _dot(vt, p)

    def produce(sub, slot, j, hh):
        s = _dot(k_ref[0, hh, j], qt_ref[0, hh, sub])
        s_ref[slot, hh] = s
        smax_ref[slot, hh] = jnp.max(s, axis=0, keepdims=True)

    def consume(sub, carry, slot, j, next_j, keep=None):
        if next_j is not None:
            for hh in range(SCORE_LEAD):
                produce(sub, 1 - slot, next_j, hh)
        out = []
        for hh in heads:
            out.append(softmax_update(
                carry[hh], lambda: s_ref[slot, hh], vt_ref[0, hh, j], keep,
                smax_ref[slot, hh] if keep is None else None))
            if next_j is not None and hh + SCORE_LEAD < HEADS_PER_STEP:
                produce(sub, 1 - slot, next_j, hh + SCORE_LEAD)
        return tuple(out)

    kpos = _strided_time(lax.broadcasted_iota(jnp.int32, (tile, tile), 0))
    qpos = _strided_time(lax.broadcasted_iota(jnp.int32, (tile, tile), 1))
    keep = kpos <= qpos
    subs = qt_ref.shape[2]
    for sub in range(subs):
        qi = subs * step + sub
        carry = tuple((jnp.full((1, tile), neg, F32), jnp.zeros((V_ROWS, tile), F32))
                      for _ in heads)
        if has_meta:
            meta_scores = [_dot(km_ref[hh], qt_ref[0, hh, sub]) for hh in heads]
        for hh in heads:
            produce(sub, 0, 0, hh)
        if has_meta:
            carry = tuple(softmax_update(carry[hh], lambda: meta_scores[hh], vtm_ref[hh])
                          for hh in heads)

        def pair(jj, c, sub=sub):
            c = consume(sub, c, 0, 2 * jj, 2 * jj + 1)
            return consume(sub, c, 1, 2 * jj + 1, 2 * jj + 2)

        carry = lax.fori_loop(0, qi // 2, pair, carry)
        if (subs == 2 and sub == 1):
            carry = consume(sub, consume(sub, carry, 0, qi - 1, qi), 1, qi, None, keep)
        else:
            carry = consume(sub, carry, 0, qi, None, keep)
        outs = [acc[0:V_HEAD, :] / acc[V_HEAD:V_HEAD + 1, :] for _, acc in carry]
        o_ref[0, 0, sub * tile:(sub + 1) * tile, :] = (
            jnp.concatenate(outs, axis=0).T.astype(o_ref.dtype))


def _attention(qt, k5, vt5, meta_kv=None):
    bsz, nh, nt, tile, _ = k5.shape
    tlen = nt * tile
    hps = HEADS_PER_STEP
    subs = 2 if nt % 2 == 0 else 1
    assert subs == 2 or nt == 1
    in_specs = [
        pl.BlockSpec((1, hps, subs, HEAD_PAD, tile), lambda b, h, q: (b, h, q, 0, 0)),
        pl.BlockSpec((1, hps, nt, tile, HEAD_PAD), lambda b, h, q: (b, h, 0, 0, 0)),
        pl.BlockSpec((1, hps, nt, V_ROWS, tile), lambda b, h, q: (b, h, 0, 0, 0)),
    ]
    args = (qt, k5, vt5)
    if meta_kv is not None:
        in_specs += [pl.BlockSpec((hps,) + a.shape[1:], lambda b, h, q: (h, 0, 0))
                     for a in meta_kv]
        args += tuple(meta_kv)
    return pl.pallas_call(
        functools.partial(_attn_kernel, has_meta=meta_kv is not None),
        grid=(bsz, nh // hps, nt // subs),
        in_specs=in_specs,
        out_specs=pl.BlockSpec((1, 1, subs * tile, hps * V_HEAD), lambda b, h, q: (b, h, q, 0)),
        out_shape=jax.ShapeDtypeStruct((bsz, nh // hps, tlen, hps * V_HEAD), BF16),
        scratch_shapes=[pltpu.VMEM((2, hps, tile, tile), F32),
                        pltpu.VMEM((2, hps, 1, tile), F32)],
        compiler_params=pltpu.CompilerParams(
            dimension_semantics=("arbitrary", "arbitrary", "arbitrary"),
            vmem_limit_bytes=VMEM_LIMIT),
        name="mla_attention",
    )(*args)


def _to_strided(a, axis):
    shp = a.shape
    nt = shp[axis] // TILE_T
    a = a.reshape(shp[:axis] + (nt, SUBLANES, GROUPS) + shp[axis + 1:])
    return jnp.swapaxes(a, axis + 1, axis + 2).reshape(shp)


def _trunk(h, first_pos, states, p, state_at):
    depth = len(p["ffn"])
    tables = _rope_tables(first_pos, h.shape[1])
    new_states = []
    for layer in range(depth):
        st = states[layer]
        ffn = p["ffn"][layer]
        attn = None
        if layer % 2 == 0:
            h, mix_state = _even_layer(h, st["mix"], *p["even"][layer // 2], state_at=state_at)
        else:
            g, q_norm, kv_norm, weights, w_out = p["odd"][layer // 2]
            qt, k5, vt5 = _mla_proj(h, tables, g, q_norm, kv_norm, weights)
            attn = (_attention(qt, k5, vt5, st["mix"]), w_out)
            mix_state = None
            if state_at is not None:
                n = state_at + 1
                assert n <= GROUPS
                mix_state = (k5[0, :, 0, 0:n * SUBLANES:SUBLANES, :],
                             vt5[0, :, 0, :, 0:n * SUBLANES:SUBLANES])
        h, tail = _ffn_layer(h, st["ffn"], *ffn, attn=attn,
                             final_g=p["final"] if layer == depth - 1 else None,
                             state_at=state_at)
        new_states.append({"mix": mix_state, "ffn": tail})
    return h, new_states


def kernel(x, meta_tokens, ev_norm, ev_w_in, ev_conv_a, ev_conv_b, ev_conv_b_bias, ev_gate_r_w, ev_gate_r_b, ev_gate_i_w, ev_gate_i_b, ev_lru_lambda, ev_w_out, od_norm, od_w_in, od_q_norm, od_kv_norm, od_w_uq, od_w_ukv, od_w_out, ffn_norm, ffn_w_up, ffn_conv_w, ffn_conv_b, ffn_w_down, final_norm):
    bsz, seq, _ = x.shape
    depth = ffn_norm.shape[0]
    assert seq % TILE_T == 0 and N_META <= GROUPS
    params = {
        "even": [(ev_norm[j], ev_w_in[j].astype(BF16), ev_conv_a[j], ev_conv_b[j],
                  ev_conv_b_bias[j], _block_diag(ev_gate_r_w[j]).astype(BF16), ev_gate_r_b[j],
                  _block_diag(ev_gate_i_w[j]).astype(BF16), ev_gate_i_b[j], ev_lru_lambda[j],
                  ev_w_out[j].astype(BF16)) for j in range(ev_norm.shape[0])],
        "odd": [(od_norm[j], od_q_norm[j], od_kv_norm[j],
                 _mla_weights(od_w_in[j], od_w_uq[j], od_w_ukv[j]), od_w_out[j].astype(BF16))
                for j in range(od_norm.shape[0])],
        "ffn": [(ffn_norm[l], ffn_w_up[l].astype(BF16), ffn_conv_w[l], ffn_conv_b[l],
                 ffn_w_down[l].astype(BF16)) for l in range(depth)],
        "final": final_norm,
    }
    zero_states = []
    for layer in range(depth):
        mix = None
        if layer % 2 == 0:
            mix = (jnp.zeros(((ev_conv_a.shape[1] - 1) * SUBLANES, CONV_WIDTH), F32),
                   jnp.zeros(((ev_conv_b.shape[1] - 1) * SUBLANES, LRU_WIDTH), F32),
                   jnp.zeros((1, LRU_WIDTH), F32))
        zero_states.append(
            {"mix": mix, "ffn": jnp.zeros(((ffn_conv_w.shape[1] - 1) * SUBLANES, 2 * D_FF), F32)})

    meta_tile = jnp.concatenate(
        [meta_tokens.astype(x.dtype), jnp.zeros((TILE_T - N_META, D_MODEL), x.dtype)], axis=0)
    _, meta_states = _trunk(_to_strided(meta_tile[None], 1), 0.0, zero_states, params,
                            state_at=N_META - 1)
    h, _ = _trunk(_to_strided(x, 1), float(N_META), meta_states, params, state_at=None)
    return h
```

```python
import functools
import math

import jax
import jax.numpy as jnp
from jax import lax
from jax.experimental import pallas as pl
from jax.experimental.pallas import tpu as pltpu

D_MODEL = 1024
N_META = 16
EPS = 1e-6
CONV_WIDTH = 512
LRU_WIDTH = 512
LRU_C = 8.0
MLA_HEADS = 16
QK_NOPE = 64
QK_ROPE = 32
QK_HEAD = QK_NOPE + QK_ROPE
V_HEAD = 64
Q_LORA = 384
KV_LORA = 256
ROPE_BASE = 10000.0
D_FF = 2816

LANES = 128
SUBLANES = 8
HEAD_PAD = 128
V_ROWS = V_HEAD + 16
HEADS_PER_STEP = 8
ATTN_Q_TILES = 4
SCORE_LEAD = 1
TILE_T = 256
GROUPS = TILE_T // SUBLANES
STEP_T = 1024
FF_CHUNK = 256
VMEM_LIMIT = 56 * 1024 * 1024

F32 = jnp.float32
BF16 = jnp.bfloat16


def _rms(x, g):
    ms = jnp.mean(x * x, axis=-1, keepdims=True)
    return x * lax.rsqrt(ms + EPS) * g


def _sigmoid(x):
    return 1.0 / (1.0 + jnp.exp(-x))


def _gelu_tanh(x):
    c = math.sqrt(2.0 / math.pi)
    return x * (0.5 * (1.0 + jnp.tanh(c * (x + 0.044715 * (x * x * x)))))


def _dot(a, b):
    return jnp.dot(a, b, preferred_element_type=F32)


def _dot_nt(a, b):
    return lax.dot_general(a, b, (((1,), (1,)), ((), ())), preferred_element_type=F32)


def _row_groups(x):
    return [x[v * SUBLANES:(v + 1) * SUBLANES, :] for v in range(x.shape[0] // SUBLANES)]


def _delays(u, prev, kmax):
    tile = u.shape[0]
    first = lax.broadcasted_iota(jnp.int32, (SUBLANES, u.shape[1]), 0) == 0
    wrapped = []
    for i in range(kmax):
        cur = u[tile - (kmax - i) * SUBLANES:tile - (kmax - i - 1) * SUBLANES, :]
        old = prev[i * SUBLANES:(i + 1) * SUBLANES, :]
        wrapped.append(jnp.where(first, pltpu.roll(old, 1, axis=0), pltpu.roll(cur, 1, axis=0)))
    return [jnp.concatenate(wrapped[kmax - k:] + [u[:tile - k * SUBLANES, :]], axis=0)
            for k in range(1, kmax + 1)]


def _tiles(x):
    return [x[i * TILE_T:(i + 1) * TILE_T, :] for i in range(x.shape[0] // TILE_T)]


def _stack(parts):
    return parts[0] if len(parts) == 1 else jnp.concatenate(parts, axis=0)


def _causal_conv(u, prev, taps, state_at=None):
    k = taps.shape[0]
    outs = []
    for ui in _tiles(u):
        delayed = _delays(ui, prev, k - 1)
        acc = delayed[k - 2] * taps[0:1, :]
        for j in range(1, k - 1):
            acc = acc + delayed[k - 2 - j] * taps[j:j + 1, :]
        outs.append(acc + ui * taps[k - 1:k, :])
        prev = _conv_tail(ui, k - 1, state_at)
    return _stack(outs), prev


def _conv_tail(u, kmax, state_at):
    tile = u.shape[0]
    if state_at is None:
        return u[tile - kmax * SUBLANES:, :]
    s, v = divmod(state_at, GROUPS)
    assert v >= kmax - 1
    rows = []
    for i in range(kmax):
        grp = u[(v - (kmax - 1 - i)) * SUBLANES:(v - (kmax - 2 - i)) * SUBLANES, :]
        rows.append(pltpu.roll(grp, SUBLANES - 1 - s, axis=0) if s != SUBLANES - 1 else grp)
    return jnp.concatenate(rows, axis=0)


def _lru_scan(a, b, carry, state_at=None):
    outs = []
    for ai, bi in zip(_tiles(a), _tiles(b)):
        hi, carry = _lru_scan_tile(ai, bi, carry, state_at)
        outs.append(hi)
    return _stack(outs), carry


def _lru_scan_tile(a, b, carry, state_at):
    a_rows, b_rows = _row_groups(a), _row_groups(b)
    prod, hzero = [a_rows[0]], [b_rows[0]]
    for v in range(1, len(a_rows)):
        prod.append(a_rows[v] * prod[-1])
        hzero.append(a_rows[v] * hzero[-1] + b_rows[v])
    pa, ph = prod[-1], hzero[-1]
    sub = lax.broadcasted_iota(jnp.int32, pa.shape, 0)
    for d in (1, 2, 4):
        ok = sub >= d
        ph = jnp.where(ok, pa * pltpu.roll(ph, d, axis=0) + ph, ph)
        pa = jnp.where(ok, pa * pltpu.roll(pa, d, axis=0), pa)
    ends = ph + pa * carry
    init = jnp.where(sub == 0, carry, pltpu.roll(ends, 1, axis=0))
    h = jnp.concatenate([hz + pr * init for hz, pr in zip(hzero, prod)], axis=0)
    if state_at is None:
        return h, ends[SUBLANES - 1:SUBLANES, :]
    s, v = divmod(state_at, GROUPS)
    row = v * SUBLANES + s
    return h, h[row:row + 1, :]


def _even_kernel(*refs, state_at):
    (x_ref, zin_ref, xbin_ref, hin_ref, g_ref, win_ref, ca_ref, cb_ref, cbb_ref, rw_ref, rb_ref,
     iw_ref, ib_ref, lam_ref, wout_ref, o_ref) = refs[:16]
    ztail, xbtail, hstate = refs[-3:]
    t = pl.program_id(1)

    @pl.when(t == 0)
    def _():
        ztail[...] = zin_ref[...]
        xbtail[...] = xbin_ref[...]
        hstate[...] = hin_ref[...]

    hn = _rms(x_ref[0], g_ref[...]).astype(BF16)
    u = _dot(hn, win_ref[...])
    cw = CONV_WIDTH
    gb = u[:, 0:cw]
    gc = u[:, cw:2 * cw]
    xa = u[:, 2 * cw:3 * cw]
    xb = u[:, 3 * cw:3 * cw + LRU_WIDTH]
    gate = u[:, 3 * cw + LRU_WIDTH:]

    conv_z, ztail[...] = _causal_conv(gc * xa, ztail[...], ca_ref[...], state_at)
    y_a = gb * conv_z

    conv_xb, xbtail[...] = _causal_conv(xb, xbtail[...], cb_ref[...], state_at)
    xc = conv_xb + cbb_ref[...]

    xcb = xc.astype(BF16)
    r = _sigmoid(_dot(xcb, rw_ref[...]) + rb_ref[...])
    i = _sigmoid(_dot(xcb, iw_ref[...]) + ib_ref[...])
    nlam = -lam_ref[...]
    softplus = jnp.maximum(nlam, 0.0) + jnp.log1p(jnp.exp(-jnp.abs(nlam)))
    log_a = -LRU_C * r * softplus
    a = jnp.exp(log_a)
    th = jnp.tanh(log_a)
    mult = jnp.sqrt(-2.0 * th / (1.0 - th))
    h, hstate[...] = _lru_scan(a, mult * (i * xc), hstate[...], state_at)

    y_b = _gelu_tanh(gate) * h
    y = jnp.concatenate([y_a, y_b], axis=-1).astype(BF16)
    o_ref[0] = x_ref[0] + _dot(y, wout_ref[...])
    if state_at is not None:
        zout_ref, xbout_ref, hout_ref = refs[16:19]
        zout_ref[...] = ztail[...]
        xbout_ref[...] = xbtail[...]
        hout_ref[...] = hstate[...]


def _const_spec(shape):
    nd = len(shape)
    return pl.BlockSpec(shape, lambda b, t: (0,) * nd, pipeline_mode=pl.Buffered(1))


def _shapes(arrays):
    return tuple(jax.ShapeDtypeStruct(a.shape, a.dtype) for a in arrays)


def _even_layer(h, state, g, w_in, conv_a, conv_b, conv_b_bias, rw, rb, iw, ib, lam, w_out,
                state_at=None):
    bsz, tlen, _ = h.shape
    tile = min(STEP_T, tlen)
    row_spec = pl.BlockSpec((1, tile, D_MODEL), lambda b, t: (b, t, 0))
    args = tuple(state) + (
        g.reshape(1, D_MODEL), w_in, conv_a, conv_b,
        conv_b_bias.reshape(1, LRU_WIDTH), rw, rb.reshape(1, LRU_WIDTH), iw,
        ib.reshape(1, LRU_WIDTH), lam.reshape(1, LRU_WIDTH), w_out)
    out_shape = [jax.ShapeDtypeStruct(h.shape, F32)]
    out_specs = [row_spec]
    if state_at is not None:
        assert bsz == 1 and tlen == tile
        out_shape += list(_shapes(state))
        out_specs += [_const_spec(s.shape) for s in state]
    res = pl.pallas_call(
        functools.partial(_even_kernel, state_at=state_at),
        grid=(bsz, tlen // tile),
        in_specs=[row_spec] + [_const_spec(a.shape) for a in args],
        out_specs=out_specs,
        out_shape=out_shape,
        scratch_shapes=[pltpu.VMEM(s.shape, F32) for s in state],
        compiler_params=pltpu.CompilerParams(
            dimension_semantics=("arbitrary", "arbitrary"), vmem_limit_bytes=VMEM_LIMIT),
        name="even_mixer",
    )(h, *args)
    return res[0], (tuple(res[1:]) if state_at is not None else None)


def _block_diag(w):
    nh, d, _ = w.shape
    eye = jnp.eye(nh, dtype=w.dtype)
    return jnp.einsum("hij,hg->higj", w, eye).reshape(nh * d, nh * d)


def _ffn_kernel(*refs, attn_input, final_norm, state_at):
    refs = list(refs)
    x_ref, tin_ref = refs.pop(0), refs.pop(0)
    attn_ref, wo_ref = (refs.pop(0), refs.pop(0)) if attn_input else (None, None)
    g_ref, wup_ref, cw_ref, cb_ref, wdn_ref = refs[:5]
    del refs[:5]
    fg_ref = refs.pop(0) if final_norm else None
    o_ref = refs.pop(0)
    tout_ref = refs.pop(0) if state_at is not None else None
    slab_ref = refs.pop() if final_norm else None
    hn_ref, act_ref, tail = refs
    t = pl.program_id(1)

    @pl.when(t == 0)
    def _():
        tail[...] = tin_ref[...]

    if attn_input:
        wg = attn_ref.shape[3]
        y = x_ref[0]
        for grp in range(attn_ref.shape[1]):
            y = y + _dot(attn_ref[0, grp], wo_ref[grp * wg:(grp + 1) * wg, :])
        o_ref[0] = y
    else:
        o_ref[0] = x_ref[0]
    hn_ref[...] = _rms(o_ref[0], g_ref[...]).astype(BF16)
    for c in range(D_FF // FF_CHUNK):
        halves = []
        for part in range(2):
            col = part * D_FF + c * FF_CHUNK
            cs = slice(col, col + FF_CHUNK)
            u = _dot(hn_ref[...], wup_ref[:, cs])
            conv_u, tail[:, cs] = _causal_conv(u, tail[:, cs], cw_ref[:, cs], state_at)
            halves.append(conv_u + cb_ref[:, cs])
        a, gte = halves
        act_ref[:, c * FF_CHUNK:(c + 1) * FF_CHUNK] = (a * _sigmoid(a) * gte).astype(BF16)
    y = o_ref[0] + _dot(act_ref[...], wdn_ref[...])
    if final_norm:
        y = _rms(y, fg_ref[...])
        slabs = range(D_MODEL // LANES)
        for c in slabs:
            slab_ref[c] = y[:, c * LANES:(c + 1) * LANES]
        for n in range(y.shape[0] // TILE_T):
            for s in range(SUBLANES):
                src = pl.ds(n * TILE_T + s, GROUPS, stride=SUBLANES)
                dst = pl.ds(n * TILE_T + s * GROUPS, GROUPS)
                for c in slabs:
                    o_ref[0, dst, c * LANES:(c + 1) * LANES] = slab_ref[c, src, :]
    else:
        o_ref[0] = y
    if state_at is not None:
        tout_ref[...] = tail[...]


def _ffn_layer(h, tail, g, w_up, conv_w, conv_b, w_down, attn=None, final_g=None, state_at=None):
    bsz, tlen, _ = h.shape
    tile = min(STEP_T, tlen)
    row_spec = pl.BlockSpec((1, tile, D_MODEL), lambda b, t: (b, t, 0))
    args = (g.reshape(1, D_MODEL), w_up, conv_w, conv_b.reshape(1, 2 * D_FF), w_down)
    if final_g is not None:
        args = args + (final_g.reshape(1, D_MODEL),)
    specs = [_const_spec(a.shape) for a in args]
    if attn is not None:
        args = attn + args
        attn_spec = pl.BlockSpec((1, attn[0].shape[1], tile, attn[0].shape[3]),
                                 lambda b, t: (b, 0, t, 0))
        specs = [attn_spec, _const_spec(attn[1].shape)] + specs
    out_shape = [jax.ShapeDtypeStruct(h.shape, F32)]
    out_specs = [row_spec]
    if state_at is not None:
        assert bsz == 1 and tlen == tile
        out_shape.append(jax.ShapeDtypeStruct(tail.shape, F32))
        out_specs.append(_const_spec(tail.shape))
    res = pl.pallas_call(
        functools.partial(_ffn_kernel, attn_input=attn is not None,
                          final_norm=final_g is not None, state_at=state_at),
        grid=(bsz, tlen // tile),
        in_specs=[row_spec, _const_spec(tail.shape)] + specs,
        out_specs=out_specs,
        out_shape=out_shape,
        scratch_shapes=[
            pltpu.VMEM((tile, D_MODEL), BF16),
            pltpu.VMEM((tile, D_FF), BF16),
            pltpu.VMEM(tail.shape, F32),
        ] + ([pltpu.VMEM((D_MODEL // LANES, tile, LANES), F32)] if final_g is not None else []),
        compiler_params=pltpu.CompilerParams(
            dimension_semantics=("arbitrary", "arbitrary"), vmem_limit_bytes=VMEM_LIMIT),
        name="conv_ffn",
    )(h, tail, *args)
    return res[0], (res[1] if state_at is not None else None)


def _mla_proj_kernel(x_ref, g_ref, win_ref, qn_ref, kvn_ref, wqt_ref, wk_ref, wvt_ref,
                     cosk_ref, sink_ref, cosq_ref, sinq_ref, qt_ref, k_ref, vt_ref):
    x = x_ref[0]
    hn = _rms(x, g_ref[...]).astype(BF16)
    u = _dot(hn, win_ref[...])
    cq = u[:, 0:Q_LORA]
    ckv = u[:, Q_LORA:Q_LORA + KV_LORA]
    kr = u[:, Q_LORA + KV_LORA:Q_LORA + KV_LORA + HEAD_PAD]
    kr_rot = u[:, Q_LORA + KV_LORA + HEAD_PAD:]
    cqn = _rms(cq, qn_ref[...]).astype(BF16)
    ckvn = _rms(ckv, kvn_ref[...]).astype(BF16)

    qt = _dot_nt(wqt_ref[...], cqn)
    kn = _dot(ckvn, wk_ref[...])
    vt = _dot_nt(wvt_ref[...], ckvn)
    k_rope = kr * cosk_ref[...] + kr_rot * sink_ref[...]
    cosq = cosq_ref[...]
    sinq = sinq_ref[...]
    scale = QK_HEAD ** -0.5 * math.log2(math.e)
    zeros = jnp.zeros((HEAD_PAD - QK_HEAD, qt.shape[1]), F32)
    ones = jnp.ones((V_ROWS - V_HEAD, qt.shape[1]), F32)
    for h in range(MLA_HEADS):
        qh = qt[h * HEAD_PAD:(h + 1) * HEAD_PAD, :]
        roped = qh[QK_NOPE:QK_HEAD, :] * cosq + qh[QK_HEAD:, :] * sinq
        q_out = jnp.concatenate(
            [qh[0:QK_NOPE, :] * scale, roped * scale, zeros], axis=0).astype(BF16)
        k_out = (kn[:, h * HEAD_PAD:(h + 1) * HEAD_PAD] + k_rope).astype(BF16)
        v_out = jnp.concatenate(
            [vt[h * V_HEAD:(h + 1) * V_HEAD, :], ones], axis=0).astype(BF16)
        for i in range(qt_ref.shape[2]):
            rows = slice(i * TILE_T, (i + 1) * TILE_T)
            qt_ref[0, h, i] = q_out[:, rows]
            k_ref[0, h, i] = k_out[rows, :]
            vt_ref[0, h, i] = v_out[:, rows]


def _rot_cols(w):
    half = QK_ROPE // 2
    return jnp.concatenate([-w[..., half:], w[..., :half]], axis=-1)


def _mla_weights(w_in, w_uq, w_ukv):
    w_kr = w_in[:, Q_LORA + KV_LORA:]
    padl = jnp.zeros((D_MODEL, QK_NOPE), F32)
    padr = jnp.zeros((D_MODEL, HEAD_PAD - QK_HEAD), F32)
    win_ext = jnp.concatenate(
        [w_in[:, :Q_LORA + KV_LORA], padl, w_kr, padr, padl, _rot_cols(w_kr), padr],
        axis=-1).astype(BF16)
    wq = w_uq.reshape(Q_LORA, MLA_HEADS, QK_HEAD)
    wq_ext = jnp.concatenate([wq, _rot_cols(wq[..., QK_NOPE:])], axis=-1)
    wqt = wq_ext.reshape(Q_LORA, MLA_HEADS * HEAD_PAD).T.astype(BF16)
    wkv = w_ukv.reshape(KV_LORA, MLA_HEADS, QK_NOPE + V_HEAD)
    wk = jnp.concatenate(
        [wkv[..., :QK_NOPE], jnp.zeros((KV_LORA, MLA_HEADS, HEAD_PAD - QK_NOPE), F32)],
        axis=-1).reshape(KV_LORA, MLA_HEADS * HEAD_PAD).astype(BF16)
    wvt = wkv[..., QK_NOPE:].reshape(KV_LORA, MLA_HEADS * V_HEAD).T.astype(BF16)
    return win_ext, wqt, wk, wvt


def _rope_tables(first_pos, t_len):
    pos = first_pos + jnp.arange(t_len, dtype=F32)
    inv_freq = ROPE_BASE ** (-jnp.arange(0, QK_ROPE, 2, dtype=F32) / QK_ROPE)
    ang = _to_strided(pos[:, None] * inv_freq[None, :], 0)
    cos2 = jnp.concatenate([jnp.cos(ang)] * 2, axis=-1)
    sin2 = jnp.concatenate([jnp.sin(ang)] * 2, axis=-1)
    lpad = ((0, 0), (QK_NOPE, HEAD_PAD - QK_HEAD))
    return jnp.pad(cos2, lpad), jnp.pad(sin2, lpad), cos2.T, sin2.T


def _mla_proj(h, tables, g, q_norm, kv_norm, weights):
    bsz, tlen, _ = h.shape
    tile = TILE_T
    nt = tlen // tile
    step = min(STEP_T, tlen)
    per = step // tile
    win_ext, wqt, wk, wvt = weights
    args = (g.reshape(1, D_MODEL), win_ext, q_norm.reshape(1, Q_LORA),
            kv_norm.reshape(1, KV_LORA), wqt, wk, wvt)
    row_spec = pl.BlockSpec((1, step, D_MODEL), lambda b, t: (b, t, 0))
    in_specs = [row_spec] + [_const_spec(a.shape) for a in args] + [
        pl.BlockSpec((step, HEAD_PAD), lambda b, t: (t, 0)),
        pl.BlockSpec((step, HEAD_PAD), lambda b, t: (t, 0)),
        pl.BlockSpec((QK_ROPE, step), lambda b, t: (0, t)),
        pl.BlockSpec((QK_ROPE, step), lambda b, t: (0, t)),
    ]
    out_shape = (
        jax.ShapeDtypeStruct((bsz, MLA_HEADS, nt, HEAD_PAD, tile), BF16),
        jax.ShapeDtypeStruct((bsz, MLA_HEADS, nt, tile, HEAD_PAD), BF16),
        jax.ShapeDtypeStruct((bsz, MLA_HEADS, nt, V_ROWS, tile), BF16),
    )
    out_specs = (
        pl.BlockSpec((1, MLA_HEADS, per, HEAD_PAD, tile), lambda b, t: (b, 0, t, 0, 0)),
        pl.BlockSpec((1, MLA_HEADS, per, tile, HEAD_PAD), lambda b, t: (b, 0, t, 0, 0)),
        pl.BlockSpec((1, MLA_HEADS, per, V_ROWS, tile), lambda b, t: (b, 0, t, 0, 0)),
    )
    return pl.pallas_call(
        _mla_proj_kernel,
        grid=(bsz, tlen // step),
        in_specs=in_specs,
        out_specs=out_specs,
        out_shape=out_shape,
        compiler_params=pltpu.CompilerParams(
            dimension_semantics=("arbitrary", "arbitrary"), vmem_limit_bytes=VMEM_LIMIT),
        name="mla_proj",
    )(h, *args, *tables)


def _strided_time(i):
    return (i & (SUBLANES - 1)) * GROUPS + (i >> 3)


def _attn_kernel(*refs, has_meta):
    if has_meta:
        qt_ref, k_ref, vt_ref, km_ref, vtm_ref, o_ref, s_ref, smax_ref = refs
    else:
        qt_ref, k_ref, vt_ref, o_ref, s_ref, smax_ref = refs
    tile = qt_ref.shape[4]
    step = pl.program_id(2)
    neg = jnp.finfo(F32).min
    heads = range(HEADS_PER_STEP)

    def softmax_update(state, scores, vt, keep=None, block_max=None):
        def block():
            s = scores()
            return s if keep is None else jnp.where(keep, s, neg)
        m, acc = state
        if block_max is None:
            block_max = jnp.max(block(), axis=0, keepdims=True)
        m_new = jnp.maximum(m, block_max)
        p = jnp.exp2(block() - m_new).astype(BF16)
        return m_new, jnp.exp2(m - m_new) * acc + _dot(vt, p)

    def produce(sub, slot, j, hh):
        s = _dot(k_ref[0, hh, j], qt_ref[0, hh, sub])
        s_ref[slot, hh] = s
        smax_ref[slot, hh] = jnp.max(s, axis=0, keepdims=True)

    def consume(sub, carry, slot, j, next_j, keep=None):
        if next_j is not None:
            for hh in range(SCORE_LEAD):
                produce(sub, 1 - slot, next_j, hh)
        out = []
        for hh in heads:
            out.append(softmax_update(
                carry[hh], lambda: s_ref[slot, hh], vt_ref[0, hh, j], keep,
                smax_ref[slot, hh] if keep is None else None))
            if next_j is not None and hh + SCORE_LEAD < HEADS_PER_STEP:
                produce(sub, 1 - slot, next_j, hh + SCORE_LEAD)
        return tuple(out)

    kpos = _strided_time(lax.broadcasted_iota(jnp.int32, (tile, tile), 0))
    qpos = _strided_time(lax.broadcasted_iota(jnp.int32, (tile, tile), 1))
    keep = kpos <= qpos
    subs = qt_ref.shape[2]
    for sub in range(subs):
        qi = subs * step + sub
        carry = tuple((jnp.full((1, tile), neg, F32), jnp.zeros((V_ROWS, tile), F32))
                      for _ in heads)
        if has_meta:
            meta_scores = [_dot(km_ref[hh], qt_ref[0, hh, sub]) for hh in heads]
        for hh in heads:
            produce(sub, 0, 0, hh)
        if has_meta:
            carry = tuple(softmax_update(carry[hh], lambda: meta_scores[hh], vtm_ref[hh])
                          for hh in heads)

        def pair(jj, c, sub=sub):
            c = consume(sub, c, 0, 2 * jj, 2 * jj + 1)
            return consume(sub, c, 1, 2 * jj + 1, 2 * jj + 2)

        carry = lax.fori_loop(0, qi // 2, pair, carry)
        if sub % 2 == 1:
            carry = consume(sub, consume(sub, carry, 0, qi - 1, qi), 1, qi, None, keep)
        else:
            carry = consume(sub, carry, 0, qi, None, keep)
        outs = [acc[0:V_HEAD, :] / acc[V_HEAD:V_HEAD + 1, :] for _, acc in carry]
        o_ref[0, 0, sub * tile:(sub + 1) * tile, :] = (
            jnp.concatenate(outs, axis=0).T.astype(o_ref.dtype))


def _attention(qt, k5, vt5, meta_kv=None):
    bsz, nh, nt, tile, _ = k5.shape
    tlen = nt * tile
    hps = HEADS_PER_STEP
    subs = ATTN_Q_TILES if nt % ATTN_Q_TILES == 0 else 1
    assert subs % 2 == 0 or nt == 1
    in_specs = [
        pl.BlockSpec((1, hps, subs, HEAD_PAD, tile), lambda b, h, q: (b, h, q, 0, 0)),
        pl.BlockSpec((1, hps, nt, tile, HEAD_PAD), lambda b, h, q: (b, h, 0, 0, 0)),
        pl.BlockSpec((1, hps, nt, V_ROWS, tile), lambda b, h, q: (b, h, 0, 0, 0)),
    ]
    args = (qt, k5, vt5)
    if meta_kv is not None:
        in_specs += [pl.BlockSpec((hps,) + a.shape[1:], lambda b, h, q: (h, 0, 0))
                     for a in meta_kv]
        args += tuple(meta_kv)
    return pl.pallas_call(
        functools.partial(_attn_kernel, has_meta=meta_kv is not None),
        grid=(bsz, nh // hps, nt // subs),
        in_specs=in_specs,
        out_specs=pl.BlockSpec((1, 1, subs * tile, hps * V_HEAD), lambda b, h, q: (b, h, q, 0)),
        out_shape=jax.ShapeDtypeStruct((bsz, nh // hps, tlen, hps * V_HEAD), BF16),
        scratch_shapes=[pltpu.VMEM((2, hps, tile, tile), F32),
                        pltpu.VMEM((2, hps, 1, tile), F32)],
        compiler_params=pltpu.CompilerParams(
            dimension_semantics=("arbitrary", "arbitrary", "arbitrary"),
            vmem_limit_bytes=VMEM_LIMIT),
        name="mla_attention",
    )(*args)


def _to_strided(a, axis):
    shp = a.shape
    nt = shp[axis] // TILE_T
    a = a.reshape(shp[:axis] + (nt, SUBLANES, GROUPS) + shp[axis + 1:])
    return jnp.swapaxes(a, axis + 1, axis + 2).reshape(shp)


def _trunk(h, first_pos, states, p, state_at):
    depth = len(p["ffn"])
    tables = _rope_tables(first_pos, h.shape[1])
    new_states = []
    for layer in range(depth):
        st = states[layer]
        ffn = p["ffn"][layer]
        attn = None
        if layer % 2 == 0:
            h, mix_state = _even_layer(h, st["mix"], *p["even"][layer // 2], state_at=state_at)
        else:
            g, q_norm, kv_norm, weights, w_out = p["odd"][layer // 2]
            qt, k5, vt5 = _mla_proj(h, tables, g, q_norm, kv_norm, weights)
            attn = (_attention(qt, k5, vt5, st["mix"]), w_out)
            mix_state = None
            if state_at is not None:
                n = state_at + 1
                assert n <= GROUPS
                mix_state = (k5[0, :, 0, 0:n * SUBLANES:SUBLANES, :],
                             vt5[0, :, 0, :, 0:n * SUBLANES:SUBLANES])
        h, tail = _ffn_layer(h, st["ffn"], *ffn, attn=attn,
                             final_g=p["final"] if layer == depth - 1 else None,
                             state_at=state_at)
        new_states.append({"mix": mix_state, "ffn": tail})
    return h, new_states


def kernel(x, meta_tokens, ev_norm, ev_w_in, ev_conv_a, ev_conv_b, ev_conv_b_bias, ev_gate_r_w, ev_gate_r_b, ev_gate_i_w, ev_gate_i_b, ev_lru_lambda, ev_w_out, od_norm, od_w_in, od_q_norm, od_kv_norm, od_w_uq, od_w_ukv, od_w_out, ffn_norm, ffn_w_up, ffn_conv_w, ffn_conv_b, ffn_w_down, final_norm):
    bsz, seq, _ = x.shape
    depth = ffn_norm.shape[0]
    assert seq % TILE_T == 0 and N_META <= GROUPS
    params = {
        "even": [(ev_norm[j], ev_w_in[j].astype(BF16), ev_conv_a[j], ev_conv_b[j],
                  ev_conv_b_bias[j], _block_diag(ev_gate_r_w[j]).astype(BF16), ev_gate_r_b[j],
                  _block_diag(ev_gate_i_w[j]).astype(BF16), ev_gate_i_b[j], ev_lru_lambda[j],
                  ev_w_out[j].astype(BF16)) for j in range(ev_norm.shape[0])],
        "odd": [(od_norm[j], od_q_norm[j], od_kv_norm[j],
                 _mla_weights(od_w_in[j], od_w_uq[j], od_w_ukv[j]), od_w_out[j].astype(BF16))
                for j in range(od_norm.shape[0])],
        "ffn": [(ffn_norm[l], ffn_w_up[l].astype(BF16), ffn_conv_w[l], ffn_conv_b[l],
                 ffn_w_down[l].astype(BF16)) for l in range(depth)],
        "final": final_norm,
    }
    zero_states = []
    for layer in range(depth):
        mix = None
        if layer % 2 == 0:
            mix = (jnp.zeros(((ev_conv_a.shape[1] - 1) * SUBLANES, CONV_WIDTH), F32),
                   jnp.zeros(((ev_conv_b.shape[1] - 1) * SUBLANES, LRU_WIDTH), F32),
                   jnp.zeros((1, LRU_WIDTH), F32))
        zero_states.append(
            {"mix": mix, "ffn": jnp.zeros(((ffn_conv_w.shape[1] - 1) * SUBLANES, 2 * D_FF), F32)})

    meta_tile = jnp.concatenate(
        [meta_tokens.astype(x.dtype), jnp.zeros((TILE_T - N_META, D_MODEL), x.dtype)], axis=0)
    _, meta_states = _trunk(_to_strided(meta_tile[None], 1), 0.0, zero_states, params,
                            state_at=N_META - 1)
    h, _ = _trunk(_to_strided(x, 1), float(N_META), meta_states, params, state_at=None)
    return h
```

```python
import functools
import math

import jax
import jax.numpy as jnp
from jax import lax
from jax.experimental import pallas as pl
from jax.experimental.pallas import tpu as pltpu

D_MODEL = 1024
N_META = 16
EPS = 1e-6
CONV_WIDTH = 512
LRU_WIDTH = 512
LRU_C = 8.0
MLA_HEADS = 16
QK_NOPE = 64
QK_ROPE = 32
QK_HEAD = QK_NOPE + QK_ROPE
V_HEAD = 64
Q_LORA = 384
KV_LORA = 256
ROPE_BASE = 10000.0
D_FF = 2816

LANES = 128
SUBLANES = 8
HEAD_PAD = 128
V_ROWS = V_HEAD + 16
HEADS_PER_STEP = 8
ATTN_Q_TILES = 4
SCORE_LEAD = 1
TILE_T = 256
GROUPS = TILE_T // SUBLANES
STEP_T = 1024
FF_CHUNK = 256
VMEM_LIMIT = 56 * 1024 * 1024

F32 = jnp.float32
BF16 = jnp.bfloat16


def _rms(x, g):
    ms = jnp.mean(x * x, axis=-1, keepdims=True)
    return x * lax.rsqrt(ms + EPS) * g


def _sigmoid(x):
    return 1.0 / (1.0 + jnp.exp(-x))


def _gelu_tanh(x):
    c = math.sqrt(2.0 / math.pi)
    return x * (0.5 * (1.0 + jnp.tanh(c * (x + 0.044715 * (x * x * x)))))


def _dot(a, b):
    return jnp.dot(a, b, preferred_element_type=F32)


def _dot_nt(a, b):
    return lax.dot_general(a, b, (((1,), (1,)), ((), ())), preferred_element_type=F32)


def _row_groups(x):
    return [x[v * SUBLANES:(v + 1) * SUBLANES, :] for v in range(x.shape[0] // SUBLANES)]


def _delays(u, prev, kmax):
    tile = u.shape[0]
    first = lax.broadcasted_iota(jnp.int32, (SUBLANES, u.shape[1]), 0) == 0
    wrapped = []
    for i in range(kmax):
        cur = u[tile - (kmax - i) * SUBLANES:tile - (kmax - i - 1) * SUBLANES, :]
        old = prev[i * SUBLANES:(i + 1) * SUBLANES, :]
        wrapped.append(jnp.where(first, pltpu.roll(old, 1, axis=0), pltpu.roll(cur, 1, axis=0)))
    return [jnp.concatenate(wrapped[kmax - k:] + [u[:tile - k * SUBLANES, :]], axis=0)
            for k in range(1, kmax + 1)]


def _tiles(x):
    return [x[i * TILE_T:(i + 1) * TILE_T, :] for i in range(x.shape[0] // TILE_T)]


def _stack(parts):
    return parts[0] if len(parts) == 1 else jnp.concatenate(parts, axis=0)


def _causal_conv(u, prev, taps, state_at=None):
    k = taps.shape[0]
    outs = []
    for ui in _tiles(u):
        delayed = _delays(ui, prev, k - 1)
        acc = delayed[k - 2] * taps[0:1, :]
        for j in range(1, k - 1):
            acc = acc + delayed[k - 2 - j] * taps[j:j + 1, :]
        outs.append(acc + ui * taps[k - 1:k, :])
        prev = _conv_tail(ui, k - 1, state_at)
    return _stack(outs), prev


def _conv_tail(u, kmax, state_at):
    tile = u.shape[0]
    if state_at is None:
        return u[tile - kmax * SUBLANES:, :]
    s, v = divmod(state_at, GROUPS)
    assert v >= kmax - 1
    rows = []
    for i in range(kmax):
        grp = u[(v - (kmax - 1 - i)) * SUBLANES:(v - (kmax - 2 - i)) * SUBLANES, :]
        rows.append(pltpu.roll(grp, SUBLANES - 1 - s, axis=0) if s != SUBLANES - 1 else grp)
    return jnp.concatenate(rows, axis=0)


def _lru_scan(a, b, carry, state_at=None):
    outs = []
    for ai, bi in zip(_tiles(a), _tiles(b)):
        hi, carry = _lru_scan_tile(ai, bi, carry, state_at)
        outs.append(hi)
    return _stack(outs), carry


def _lru_scan_tile(a, b, carry, state_at):
    a_rows, b_rows = _row_groups(a), _row_groups(b)
    prod, hzero = [a_rows[0]], [b_rows[0]]
    for v in range(1, len(a_rows)):
        prod.append(a_rows[v] * prod[-1])
        hzero.append(a_rows[v] * hzero[-1] + b_rows[v])
    pa, ph = prod[-1], hzero[-1]
    sub = lax.broadcasted_iota(jnp.int32, pa.shape, 0)
    for d in (1, 2, 4):
        ok = sub >= d
        ph = jnp.where(ok, pa * pltpu.roll(ph, d, axis=0) + ph, ph)
        pa = jnp.where(ok, pa * pltpu.roll(pa, d, axis=0), pa)
    ends = ph + pa * carry
    init = jnp.where(sub == 0, carry, pltpu.roll(ends, 1, axis=0))
    h = jnp.concatenate([hz + pr * init for hz, pr in zip(hzero, prod)], axis=0)
    if state_at is None:
        return h, ends[SUBLANES - 1:SUBLANES, :]
    s, v = divmod(state_at, GROUPS)
    row = v * SUBLANES + s
    return h, h[row:row + 1, :]


def _even_kernel(*refs, state_at):
    (x_ref, zin_ref, xbin_ref, hin_ref, g_ref, win_ref, ca_ref, cb_ref, cbb_ref, rw_ref, rb_ref,
     iw_ref, ib_ref, lam_ref, wout_ref, o_ref) = refs[:16]
    ztail, xbtail, hstate = refs[-3:]
    t = pl.program_id(1)

    @pl.when(t == 0)
    def _():
        ztail[...] = zin_ref[...]
        xbtail[...] = xbin_ref[...]
        hstate[...] = hin_ref[...]

    hn = _rms(x_ref[0], g_ref[...]).astype(BF16)
    u = _dot(hn, win_ref[...])
    cw = CONV_WIDTH
    gb = u[:, 0:cw]
    gc = u[:, cw:2 * cw]
    xa = u[:, 2 * cw:3 * cw]
    xb = u[:, 3 * cw:3 * cw + LRU_WIDTH]
    gate = u[:, 3 * cw + LRU_WIDTH:]

    conv_z, ztail[...] = _causal_conv(gc * xa, ztail[...], ca_ref[...], state_at)
    y_a = gb * conv_z

    conv_xb, xbtail[...] = _causal_conv(xb, xbtail[...], cb_ref[...], state_at)
    xc = conv_xb + cbb_ref[...]

    xcb = xc.astype(BF16)
    r = _sigmoid(_dot(xcb, rw_ref[...]) + rb_ref[...])
    i = _sigmoid(_dot(xcb, iw_ref[...]) + ib_ref[...])
    nlam = -lam_ref[...]
    softplus = jnp.maximum(nlam, 0.0) + jnp.log1p(jnp.exp(-jnp.abs(nlam)))
    log_a = -LRU_C * r * softplus
    a = jnp.exp(log_a)
    th = jnp.tanh(log_a)
    mult = jnp.sqrt(-2.0 * th / (1.0 - th))
    h, hstate[...] = _lru_scan(a, mult * (i * xc), hstate[...], state_at)

    y_b = _gelu_tanh(gate) * h
    y = jnp.concatenate([y_a, y_b], axis=-1).astype(BF16)
    o_ref[0] = x_ref[0] + _dot(y, wout_ref[...])
    if state_at is not None:
        zout_ref, xbout_ref, hout_ref = refs[16:19]
        zout_ref[...] = ztail[...]
        xbout_ref[...] = xbtail[...]
        hout_ref[...] = hstate[...]


def _const_spec(shape):
    nd = len(shape)
    return pl.BlockSpec(shape, lambda b, t: (0,) * nd, pipeline_mode=pl.Buffered(1))


def _shapes(arrays):
    return tuple(jax.ShapeDtypeStruct(a.shape, a.dtype) for a in arrays)


def _even_layer(h, state, g, w_in, conv_a, conv_b, conv_b_bias, rw, rb, iw, ib, lam, w_out,
                state_at=None):
    bsz, tlen, _ = h.shape
    tile = min(STEP_T, tlen)
    row_spec = pl.BlockSpec((1, tile, D_MODEL), lambda b, t: (b, t, 0))
    args = tuple(state) + (
        g.reshape(1, D_MODEL), w_in, conv_a, conv_b,
        conv_b_bias.reshape(1, LRU_WIDTH), rw, rb.reshape(1, LRU_WIDTH), iw,
        ib.reshape(1, LRU_WIDTH), lam.reshape(1, LRU_WIDTH), w_out)
    out_shape = [jax.ShapeDtypeStruct(h.shape, F32)]
    out_specs = [row_spec]
    if state_at is not None:
        assert bsz == 1 and tlen == tile
        out_shape += list(_shapes(state))
        out_specs += [_const_spec(s.shape) for s in state]
    res = pl.pallas_call(
        functools.partial(_even_kernel, state_at=state_at),
        grid=(bsz, tlen // tile),
        in_specs=[row_spec] + [_const_spec(a.shape) for a in args],
        out_specs=out_specs,
        out_shape=out_shape,
        scratch_shapes=[pltpu.VMEM(s.shape, F32) for s in state],
        compiler_params=pltpu.CompilerParams(
            dimension_semantics=("arbitrary", "arbitrary"), vmem_limit_bytes=VMEM_LIMIT),
        name="even_mixer",
    )(h, *args)
    return res[0], (tuple(res[1:]) if state_at is not None else None)


def _block_diag(w):
    nh, d, _ = w.shape
    eye = jnp.eye(nh, dtype=w.dtype)
    return jnp.einsum("hij,hg->higj", w, eye).reshape(nh * d, nh * d)


def _ffn_kernel(*refs, attn_input, final_norm, state_at):
    refs = list(refs)
    x_ref, tin_ref = refs.pop(0), refs.pop(0)
    attn_ref, wo_ref = (refs.pop(0), refs.pop(0)) if attn_input else (None, None)
    g_ref, wup_ref, cw_ref, cb_ref, wdn_ref = refs[:5]
    del refs[:5]
    fg_ref = refs.pop(0) if final_norm else None
    o_ref = refs.pop(0)
    tout_ref = refs.pop(0) if state_at is not None else None
    slab_ref = refs.pop() if final_norm else None
    hn_ref, act_ref, tail = refs
    t = pl.program_id(1)

    @pl.when(t == 0)
    def _():
        tail[...] = tin_ref[...]

    if attn_input:
        wg = attn_ref.shape[3]
        y = x_ref[0]
        for grp in range(attn_ref.shape[1]):
            y = y + _dot(attn_ref[0, grp], wo_ref[grp * wg:(grp + 1) * wg, :])
        o_ref[0] = y
    else:
        o_ref[0] = x_ref[0]
    hn_ref[...] = _rms(o_ref[0], g_ref[...]).astype(BF16)
    for c in range(D_FF // FF_CHUNK):
        halves = []
        for part in range(2):
            col = part * D_FF + c * FF_CHUNK
            cs = slice(col, col + FF_CHUNK)
            u = _dot(hn_ref[...], wup_ref[:, cs])
            conv_u, tail[:, cs] = _causal_conv(u, tail[:, cs], cw_ref[:, cs], state_at)
            halves.append(conv_u + cb_ref[:, cs])
        a, gte = halves
        act_ref[:, c * FF_CHUNK:(c + 1) * FF_CHUNK] = (a * _sigmoid(a) * gte).astype(BF16)
    y = o_ref[0] + _dot(act_ref[...], wdn_ref[...])
    if final_norm:
        y = _rms(y, fg_ref[...])
        slabs = range(D_MODEL // LANES)
        for c in slabs:
            slab_ref[c] = y[:, c * LANES:(c + 1) * LANES]
        for n in range(y.shape[0] // TILE_T):
            for s in range(SUBLANES):
                src = pl.ds(n * TILE_T + s, GROUPS, stride=SUBLANES)
                dst = pl.ds(n * TILE_T + s * GROUPS, GROUPS)
                for c in slabs:
                    o_ref[0, dst, c * LANES:(c + 1) * LANES] = slab_ref[c, src, :]
    else:
        o_ref[0] = y
    if state_at is not None:
        tout_ref[...] = tail[...]


def _ffn_layer(h, tail, g, w_up, conv_w, conv_b, w_down, attn=None, final_g=None, state_at=None):
    bsz, tlen, _ = h.shape
    tile = min(STEP_T, tlen)
    row_spec = pl.BlockSpec((1, tile, D_MODEL), lambda b, t: (b, t, 0))
    args = (g.reshape(1, D_MODEL), w_up, conv_w, conv_b.reshape(1, 2 * D_FF), w_down)
    if final_g is not None:
        args = args + (final_g.reshape(1, D_MODEL),)
    specs = [_const_spec(a.shape) for a in args]
    if attn is not None:
        args = attn + args
        attn_spec = pl.BlockSpec((1, attn[0].shape[1], tile, attn[0].shape[3]),
                                 lambda b, t: (b, 0, t, 0))
        specs = [attn_spec, _const_spec(attn[1].shape)] + specs
    out_shape = [jax.ShapeDtypeStruct(h.shape, F32)]
    out_specs = [row_spec]
    if state_at is not None:
        assert bsz == 1 and tlen == tile
        out_shape.append(jax.ShapeDtypeStruct(tail.shape, F32))
        out_specs.append(_const_spec(tail.shape))
    res = pl.pallas_call(
        functools.partial(_ffn_kernel, attn_input=attn is not None,
                          final_norm=final_g is not None, state_at=state_at),
        grid=(bsz, tlen // tile),
        in_specs=[row_spec, _const_spec(tail.shape)] + specs,
        out_specs=out_specs,
        out_shape=out_shape,
        scratch_shapes=[
            pltpu.VMEM((tile, D_MODEL), BF16),
            pltpu.VMEM((tile, D_FF), BF16),
            pltpu.VMEM(tail.shape, F32),
        ] + ([pltpu.VMEM((D_MODEL // LANES, tile, LANES), F32)] if final_g is not None else []),
        compiler_params=pltpu.CompilerParams(
            dimension_semantics=("arbitrary", "arbitrary"), vmem_limit_bytes=VMEM_LIMIT),
        name="conv_ffn",
    )(h, tail, *args)
    return res[0], (res[1] if state_at is not None else None)


def _mla_proj_kernel(x_ref, g_ref, win_ref, qn_ref, kvn_ref, wqt_ref, wk_ref, wvt_ref,
                     cosk_ref, sink_ref, cosq_ref, sinq_ref, qt_ref, k_ref, vt_ref):
    x = x_ref[0]
    hn = _rms(x, g_ref[...]).astype(BF16)
    u = _dot(hn, win_ref[...])
    cq = u[:, 0:Q_LORA]
    ckv = u[:, Q_LORA:Q_LORA + KV_LORA]
    kr = u[:, Q_LORA + KV_LORA:Q_LORA + KV_LORA + HEAD_PAD]
    kr_rot = u[:, Q_LORA + KV_LORA + HEAD_PAD:]
    cqn = _rms(cq, qn_ref[...]).astype(BF16)
    ckvn = _rms(ckv, kvn_ref[...]).astype(BF16)

    qt = _dot_nt(wqt_ref[...], cqn)
    kn = _dot(ckvn, wk_ref[...])
    vt = _dot_nt(wvt_ref[...], ckvn)
    k_rope = kr * cosk_ref[...] + kr_rot * sink_ref[...]
    cosq = cosq_ref[...]
    sinq = sinq_ref[...]
    scale = QK_HEAD ** -0.5 * math.log2(math.e)
    zeros = jnp.zeros((HEAD_PAD - QK_HEAD, qt.shape[1]), F32)
    ones = jnp.ones((V_ROWS - V_HEAD, qt.shape[1]), F32)
    for h in range(MLA_HEADS):
        qh = qt[h * HEAD_PAD:(h + 1) * HEAD_PAD, :]
        roped = qh[QK_NOPE:QK_HEAD, :] * cosq + qh[QK_HEAD:, :] * sinq
        q_out = jnp.concatenate(
            [qh[0:QK_NOPE, :] * scale, roped * scale, zeros], axis=0).astype(BF16)
        k_out = (kn[:, h * HEAD_PAD:(h + 1) * HEAD_PAD] + k_rope).astype(BF16)
        v_out = jnp.concatenate(
            [vt[h * V_HEAD:(h + 1) * V_HEAD, :], ones], axis=0).astype(BF16)
        for i in range(qt_ref.shape[2]):
            rows = slice(i * TILE_T, (i + 1) * TILE_T)
            qt_ref[0, h, i] = q_out[:, rows]
            k_ref[0, h, i] = k_out[rows, :]
            vt_ref[0, h, i] = v_out[:, rows]


def _rot_cols(w):
    half = QK_ROPE // 2
    return jnp.concatenate([-w[..., half:], w[..., :half]], axis=-1)


def _mla_weights(w_in, w_uq, w_ukv):
    w_kr = w_in[:, Q_LORA + KV_LORA:]
    padl = jnp.zeros((D_MODEL, QK_NOPE), F32)
    padr = jnp.zeros((D_MODEL, HEAD_PAD - QK_HEAD), F32)
    win_ext = jnp.concatenate(
        [w_in[:, :Q_LORA + KV_LORA], padl, w_kr, padr, padl, _rot_cols(w_kr), padr],
        axis=-1).astype(BF16)
    wq = w_uq.reshape(Q_LORA, MLA_HEADS, QK_HEAD)
    wq_ext = jnp.concatenate([wq, _rot_cols(wq[..., QK_NOPE:])], axis=-1)
    wqt = wq_ext.reshape(Q_LORA, MLA_HEADS * HEAD_PAD).T.astype(BF16)
    wkv = w_ukv.reshape(KV_LORA, MLA_HEADS, QK_NOPE + V_HEAD)
    wk = jnp.concatenate(
        [wkv[..., :QK_NOPE], jnp.zeros((KV_LORA, MLA_HEADS, HEAD_PAD - QK_NOPE), F32)],
        axis=-1).reshape(KV_LORA, MLA_HEADS * HEAD_PAD).astype(BF16)
    wvt = wkv[..., QK_NOPE:].reshape(KV_LORA, MLA_HEADS * V_HEAD).T.astype(BF16)
    return win_ext, wqt, wk, wvt


def _rope_tables(first_pos, t_len):
    pos = first_pos + jnp.arange(t_len, dtype=F32)
    inv_freq = ROPE_BASE ** (-jnp.arange(0, QK_ROPE, 2, dtype=F32) / QK_ROPE)
    ang = _to_strided(pos[:, None] * inv_freq[None, :], 0)
    cos2 = jnp.concatenate([jnp.cos(ang)] * 2, axis=-1)
    sin2 = jnp.concatenate([jnp.sin(ang)] * 2, axis=-1)
    lpad = ((0, 0), (QK_NOPE, HEAD_PAD - QK_HEAD))
    return jnp.pad(cos2, lpad), jnp.pad(sin2, lpad), cos2.T, sin2.T


def _mla_proj(h, tables, g, q_norm, kv_norm, weights):
    bsz, tlen, _ = h.shape
    tile = TILE_T
    nt = tlen // tile
    step = min(STEP_T, tlen)
    per = step // tile
    win_ext, wqt, wk, wvt = weights
    args = (g.reshape(1, D_MODEL), win_ext, q_norm.reshape(1, Q_LORA),
            kv_norm.reshape(1, KV_LORA), wqt, wk, wvt)
    row_spec = pl.BlockSpec((1, step, D_MODEL), lambda b, t: (b, t, 0))
    in_specs = [row_spec] + [_const_spec(a.shape) for a in args] + [
        pl.BlockSpec((step, HEAD_PAD), lambda b, t: (t, 0)),
        pl.BlockSpec((step, HEAD_PAD), lambda b, t: (t, 0)),
        pl.BlockSpec((QK_ROPE, step), lambda b, t: (0, t)),
        pl.BlockSpec((QK_ROPE, step), lambda b, t: (0, t)),
    ]
    out_shape = (
        jax.ShapeDtypeStruct((bsz, MLA_HEADS, nt, HEAD_PAD, tile), BF16),
        jax.ShapeDtypeStruct((bsz, MLA_HEADS, nt, tile, HEAD_PAD), BF16),
        jax.ShapeDtypeStruct((bsz, MLA_HEADS, nt, V_ROWS, tile), BF16),
    )
    out_specs = (
        pl.BlockSpec((1, MLA_HEADS, per, HEAD_PAD, tile), lambda b, t: (b, 0, t, 0, 0)),
        pl.BlockSpec((1, MLA_HEADS, per, tile, HEAD_PAD), lambda b, t: (b, 0, t, 0, 0)),
        pl.BlockSpec((1, MLA_HEADS, per, V_ROWS, tile), lambda b, t: (b, 0, t, 0, 0)),
    )
    return pl.pallas_call(
        _mla_proj_kernel,
        grid=(bsz, tlen // step),
        in_specs=in_specs,
        out_specs=out_specs,
        out_shape=out_shape,
        compiler_params=pltpu.CompilerParams(
            dimension_semantics=("arbitrary", "arbitrary"), vmem_limit_bytes=VMEM_LIMIT),
        name="mla_proj",
    )(h, *args, *tables)


def _strided_time(i):
    return (i & (SUBLANES - 1)) * GROUPS + (i >> 3)


def _attn_kernel(*refs, has_meta):
    if has_meta:
        qt_ref, k_ref, vt_ref, km_ref, vtm_ref, o_ref, s_ref, smax_ref = refs
    else:
        qt_ref, k_ref, vt_ref, o_ref, s_ref, smax_ref = refs
    tile = qt_ref.shape[4]
    step = pl.program_id(2)
    neg = jnp.finfo(F32).min
    heads = range(HEADS_PER_STEP)

    def softmax_update(state, scores, vt, keep=None, block_max=None, extra=None):
        def block():
            s = scores()
            return s if keep is None else jnp.where(keep, s, neg)
        m, acc = state
        if block_max is None:
            block_max = jnp.max(block(), axis=0, keepdims=True)
        if extra is not None:
            block_max = jnp.maximum(block_max, jnp.max(extra[0], axis=0, keepdims=True))
        m_new = jnp.maximum(m, block_max)
        p = jnp.exp2(block() - m_new).astype(BF16)
        if extra is not None:
            p = jnp.concatenate([p, jnp.exp2(extra[0] - m_new).astype(BF16)], axis=0)
            vt = jnp.concatenate([vt, extra[1]], axis=1)
        return m_new, jnp.exp2(m - m_new) * acc + _dot(vt, p)

    def produce(sub, slot, j, hh):
        s = _dot(k_ref[0, hh, j], qt_ref[0, hh, sub])
        s_ref[slot, hh] = s
        smax_ref[slot, hh] = jnp.max(s, axis=0, keepdims=True)

    def consume(sub, carry, slot, j, next_j, keep=None, with_meta=False):
        if with_meta:
            meta = [(_dot(km_ref[hh], qt_ref[0, hh, sub]), vtm_ref[hh]) for hh in heads]
        if next_j is not None:
            for hh in range(SCORE_LEAD):
                produce(sub, 1 - slot, next_j, hh)
        out = []
        for hh in heads:
            out.append(softmax_update(
                carry[hh], lambda: s_ref[slot, hh], vt_ref[0, hh, j], keep,
                smax_ref[slot, hh] if keep is None else None,
                meta[hh] if with_meta else None))
            if next_j is not None and hh + SCORE_LEAD < HEADS_PER_STEP:
                produce(sub, 1 - slot, next_j, hh + SCORE_LEAD)
        return tuple(out)

    kpos = _strided_time(lax.broadcasted_iota(jnp.int32, (tile, tile), 0))
    qpos = _strided_time(lax.broadcasted_iota(jnp.int32, (tile, tile), 1))
    keep = kpos <= qpos
    subs = qt_ref.shape[2]
    for sub in range(subs):
        qi = subs * step + sub
        carry = tuple((jnp.full((1, tile), neg, F32), jnp.zeros((V_ROWS, tile), F32))
                      for _ in heads)
        for hh in heads:
            produce(sub, 0, 0, hh)

        def group(jj, c, sub=sub):
            for d in range(subs):
                c = consume(sub, c, d % 2, subs * jj + d, subs * jj + d + 1)
            return c

        carry = lax.fori_loop(0, step, group, carry)
        for d in range(sub):
            carry = consume(sub, carry, d % 2, subs * step + d, subs * step + d + 1)
        carry = consume(sub, carry, sub % 2, qi, None, keep, has_meta)
        outs = [acc[0:V_HEAD, :] / acc[V_HEAD:V_HEAD + 1, :] for _, acc in carry]
        o_ref[0, 0, sub * tile:(sub + 1) * tile, :] = (
            jnp.concatenate(outs, axis=0).T.astype(o_ref.dtype))


def _attention(qt, k5, vt5, meta_kv=None):
    bsz, nh, nt, tile, _ = k5.shape
    tlen = nt * tile
    hps = HEADS_PER_STEP
    subs = ATTN_Q_TILES if nt % ATTN_Q_TILES == 0 else 1
    assert subs % 2 == 0 or nt == 1
    in_specs = [
        pl.BlockSpec((1, hps, subs, HEAD_PAD, tile), lambda b, h, q: (b, h, q, 0, 0)),
        pl.BlockSpec((1, hps, nt, tile, HEAD_PAD), lambda b, h, q: (b, h, 0, 0, 0)),
        pl.BlockSpec((1, hps, nt, V_ROWS, tile), lambda b, h, q: (b, h, 0, 0, 0)),
    ]
    args = (qt, k5, vt5)
    if meta_kv is not None:
        in_specs += [pl.BlockSpec((hps,) + a.shape[1:], lambda b, h, q: (h, 0, 0))
                     for a in meta_kv]
        args += tuple(meta_kv)
    return pl.pallas_call(
        functools.partial(_attn_kernel, has_meta=meta_kv is not None),
        grid=(bsz, nh // hps, nt // subs),
        in_specs=in_specs,
        out_specs=pl.BlockSpec((1, 1, subs * tile, hps * V_HEAD), lambda b, h, q: (b, h, q, 0)),
        out_shape=jax.ShapeDtypeStruct((bsz, nh // hps, tlen, hps * V_HEAD), BF16),
        scratch_shapes=[pltpu.VMEM((2, hps, tile, tile), F32),
                        pltpu.VMEM((2, hps, 1, tile), F32)],
        compiler_params=pltpu.CompilerParams(
            dimension_semantics=("arbitrary", "arbitrary", "arbitrary"),
            vmem_limit_bytes=VMEM_LIMIT),
        name="mla_attention",
    )(*args)


def _to_strided(a, axis):
    shp = a.shape
    nt = shp[axis] // TILE_T
    a = a.reshape(shp[:axis] + (nt, SUBLANES, GROUPS) + shp[axis + 1:])
    return jnp.swapaxes(a, axis + 1, axis + 2).reshape(shp)


def _trunk(h, first_pos, states, p, state_at):
    depth = len(p["ffn"])
    tables = _rope_tables(first_pos, h.shape[1])
    new_states = []
    for layer in range(depth):
        st = states[layer]
        ffn = p["ffn"][layer]
        attn = None
        if layer % 2 == 0:
            h, mix_state = _even_layer(h, st["mix"], *p["even"][layer // 2], state_at=state_at)
        else:
            g, q_norm, kv_norm, weights, w_out = p["odd"][layer // 2]
            qt, k5, vt5 = _mla_proj(h, tables, g, q_norm, kv_norm, weights)
            attn = (_attention(qt, k5, vt5, st["mix"]), w_out)
            mix_state = None
            if state_at is not None:
                n = state_at + 1
                assert n <= GROUPS
                mix_state = (k5[0, :, 0, 0:n * SUBLANES:SUBLANES, :],
                             vt5[0, :, 0, :, 0:n * SUBLANES:SUBLANES])
        h, tail = _ffn_layer(h, st["ffn"], *ffn, attn=attn,
                             final_g=p["final"] if layer == depth - 1 else None,
                             state_at=state_at)
        new_states.append({"mix": mix_state, "ffn": tail})
    return h, new_states


def kernel(x, meta_tokens, ev_norm, ev_w_in, ev_conv_a, ev_conv_b, ev_conv_b_bias, ev_gate_r_w, ev_gate_r_b, ev_gate_i_w, ev_gate_i_b, ev_lru_lambda, ev_w_out, od_norm, od_w_in, od_q_norm, od_kv_norm, od_w_uq, od_w_ukv, od_w_out, ffn_norm, ffn_w_up, ffn_conv_w, ffn_conv_b, ffn_w_down, final_norm):
    bsz, seq, _ = x.shape
    depth = ffn_norm.shape[0]
    assert seq % TILE_T == 0 and N_META <= GROUPS
    params = {
        "even": [(ev_norm[j], ev_w_in[j].astype(BF16), ev_conv_a[j], ev_conv_b[j],
                  ev_conv_b_bias[j], _block_diag(ev_gate_r_w[j]).astype(BF16), ev_gate_r_b[j],
                  _block_diag(ev_gate_i_w[j]).astype(BF16), ev_gate_i_b[j], ev_lru_lambda[j],
                  ev_w_out[j].astype(BF16)) for j in range(ev_norm.shape[0])],
        "odd": [(od_norm[j], od_q_norm[j], od_kv_norm[j],
                 _mla_weights(od_w_in[j], od_w_uq[j], od_w_ukv[j]), od_w_out[j].astype(BF16))
                for j in range(od_norm.shape[0])],
        "ffn": [(ffn_norm[l], ffn_w_up[l].astype(BF16), ffn_conv_w[l], ffn_conv_b[l],
                 ffn_w_down[l].astype(BF16)) for l in range(depth)],
        "final": final_norm,
    }
    zero_states = []
    for layer in range(depth):
        mix = None
        if layer % 2 == 0:
            mix = (jnp.zeros(((ev_conv_a.shape[1] - 1) * SUBLANES, CONV_WIDTH), F32),
                   jnp.zeros(((ev_conv_b.shape[1] - 1) * SUBLANES, LRU_WIDTH), F32),
                   jnp.zeros((1, LRU_WIDTH), F32))
        zero_states.append(
            {"mix": mix, "ffn": jnp.zeros(((ffn_conv_w.shape[1] - 1) * SUBLANES, 2 * D_FF), F32)})

    meta_tile = jnp.concatenate(
        [meta_tokens.astype(x.dtype), jnp.zeros((TILE_T - N_META, D_MODEL), x.dtype)], axis=0)
    _, meta_states = _trunk(_to_strided(meta_tile[None], 1), 0.0, zero_states, params,
                            state_at=N_META - 1)
    h, _ = _trunk(_to_strided(x, 1), float(N_META), meta_states, params, state_at=None)
    return h
```

```python
import functools
import math
from typing import NamedTuple

import jax
import jax.numpy as jnp
from jax import lax
from jax.experimental import pallas as pl
from jax.experimental.pallas import tpu as pltpu

D_MODEL = 1024
N_META = 16
EPS = 1e-6
CONV_WIDTH = 512
LRU_WIDTH = 512
LRU_C = 8.0
MLA_HEADS = 16
QK_NOPE = 64
QK_ROPE = 32
QK_HEAD = QK_NOPE + QK_ROPE
V_HEAD = 64
Q_LORA = 384
KV_LORA = 256
ROPE_BASE = 10000.0
D_FF = 2816

LANES = 128
SUBLANES = 8
HEAD_PAD = 128
V_ROWS = V_HEAD + 16
HEADS_PER_STEP = 8
ATTN_Q_TILES = 4
SCORE_LEAD = 1
TILE_T = 256
GROUPS = TILE_T // SUBLANES
STEP_T = 1024
FF_CHUNK = 256
VMEM_LIMIT = 56 * 1024 * 1024

F32 = jnp.float32
BF16 = jnp.bfloat16


def _rms(x, g):
    ms = jnp.mean(x * x, axis=-1, keepdims=True)
    return x * lax.rsqrt(ms + EPS) * g


def _sigmoid(x):
    return 1.0 / (1.0 + jnp.exp(-x))


def _gelu_tanh(x):
    c = math.sqrt(2.0 / math.pi)
    return x * (0.5 * (1.0 + jnp.tanh(c * (x + 0.044715 * (x * x * x)))))


def _dot(a, b):
    return jnp.dot(a, b, preferred_element_type=F32)


def _dot_nt(a, b):
    return lax.dot_general(a, b, (((1,), (1,)), ((), ())), preferred_element_type=F32)


def _row_groups(x):
    return [x[v * SUBLANES:(v + 1) * SUBLANES, :] for v in range(x.shape[0] // SUBLANES)]


def _delays(u, prev, kmax):
    tile = u.shape[0]
    first = lax.broadcasted_iota(jnp.int32, (SUBLANES, u.shape[1]), 0) == 0
    wrapped = []
    for i in range(kmax):
        cur = u[tile - (kmax - i) * SUBLANES:tile - (kmax - i - 1) * SUBLANES, :]
        old = prev[i * SUBLANES:(i + 1) * SUBLANES, :]
        wrapped.append(jnp.where(first, pltpu.roll(old, 1, axis=0), pltpu.roll(cur, 1, axis=0)))
    return [jnp.concatenate(wrapped[kmax - k:] + [u[:tile - k * SUBLANES, :]], axis=0)
            for k in range(1, kmax + 1)]


def _tiles(x):
    return [x[i * TILE_T:(i + 1) * TILE_T, :] for i in range(x.shape[0] // TILE_T)]


def _stack(parts):
    return parts[0] if len(parts) == 1 else jnp.concatenate(parts, axis=0)


def _causal_conv(u, prev, taps, state_at=None):
    k = taps.shape[0]
    outs = []
    for ui in _tiles(u):
        delayed = _delays(ui, prev, k - 1)
        acc = delayed[k - 2] * taps[0:1, :]
        for j in range(1, k - 1):
            acc = acc + delayed[k - 2 - j] * taps[j:j + 1, :]
        outs.append(acc + ui * taps[k - 1:k, :])
        prev = _conv_tail(ui, k - 1, state_at)
    return _stack(outs), prev


def _conv_tail(u, kmax, state_at):
    tile = u.shape[0]
    if state_at is None:
        return u[tile - kmax * SUBLANES:, :]
    s, v = divmod(state_at, GROUPS)
    assert v >= kmax - 1
    rows = []
    for i in range(kmax):
        grp = u[(v - (kmax - 1 - i)) * SUBLANES:(v - (kmax - 2 - i)) * SUBLANES, :]
        rows.append(pltpu.roll(grp, SUBLANES - 1 - s, axis=0) if s != SUBLANES - 1 else grp)
    return jnp.concatenate(rows, axis=0)


def _lru_scan(a, b, carry, state_at=None):
    outs = []
    for ai, bi in zip(_tiles(a), _tiles(b)):
        hi, carry = _lru_scan_tile(ai, bi, carry, state_at)
        outs.append(hi)
    return _stack(outs), carry


def _lru_scan_tile(a, b, carry, state_at):
    a_rows, b_rows = _row_groups(a), _row_groups(b)
    prod, hzero = [a_rows[0]], [b_rows[0]]
    for v in range(1, len(a_rows)):
        prod.append(a_rows[v] * prod[-1])
        hzero.append(a_rows[v] * hzero[-1] + b_rows[v])
    pa, ph = prod[-1], hzero[-1]
    sub = lax.broadcasted_iota(jnp.int32, pa.shape, 0)
    for d in (1, 2, 4):
        ok = sub >= d
        ph = jnp.where(ok, pa * pltpu.roll(ph, d, axis=0) + ph, ph)
        pa = jnp.where(ok, pa * pltpu.roll(pa, d, axis=0), pa)
    ends = ph + pa * carry
    init = jnp.where(sub == 0, carry, pltpu.roll(ends, 1, axis=0))
    h = jnp.concatenate([hz + pr * init for hz, pr in zip(hzero, prod)], axis=0)
    if state_at is None:
        return h, ends[SUBLANES - 1:SUBLANES, :]
    s, v = divmod(state_at, GROUPS)
    row = v * SUBLANES + s
    return h, h[row:row + 1, :]


def _even_kernel(*refs, state_at):
    (x_ref, zin_ref, xbin_ref, hin_ref, g_ref, win_ref, ca_ref, cb_ref, cbb_ref, rw_ref, rb_ref,
     iw_ref, ib_ref, lam_ref, wout_ref, o_ref) = refs[:16]
    ztail, xbtail, hstate = refs[-3:]
    t = pl.program_id(1)

    @pl.when(t == 0)
    def _():
        ztail[...] = zin_ref[...]
        xbtail[...] = xbin_ref[...]
        hstate[...] = hin_ref[...]

    hn = _rms(x_ref[0], g_ref[...]).astype(BF16)
    u = _dot(hn, win_ref[0])
    cw = CONV_WIDTH
    gb = u[:, 0:cw]
    gc = u[:, cw:2 * cw]
    xa = u[:, 2 * cw:3 * cw]
    xb = u[:, 3 * cw:3 * cw + LRU_WIDTH]
    gate = u[:, 3 * cw + LRU_WIDTH:]

    conv_z, ztail[...] = _causal_conv(gc * xa, ztail[...], ca_ref[...], state_at)
    y_a = gb * conv_z

    conv_xb, xbtail[...] = _causal_conv(xb, xbtail[...], cb_ref[...], state_at)
    xc = conv_xb + cbb_ref[...]

    xcb = xc.astype(BF16)
    r = _sigmoid(_dot(xcb, rw_ref[...]) + rb_ref[...])
    i = _sigmoid(_dot(xcb, iw_ref[...]) + ib_ref[...])
    nlam = -lam_ref[...]
    softplus = jnp.maximum(nlam, 0.0) + jnp.log1p(jnp.exp(-jnp.abs(nlam)))
    log_a = -LRU_C * r * softplus
    a = jnp.exp(log_a)
    th = jnp.tanh(log_a)
    mult = jnp.sqrt(-2.0 * th / (1.0 - th))
    h, hstate[...] = _lru_scan(a, mult * (i * xc), hstate[...], state_at)

    y_b = _gelu_tanh(gate) * h
    y = jnp.concatenate([y_a, y_b], axis=-1).astype(BF16)
    o_ref[0] = x_ref[0] + _dot(y, wout_ref[0])
    if state_at is not None:
        zout_ref, xbout_ref, hout_ref = refs[16:19]
        zout_ref[...] = ztail[...]
        xbout_ref[...] = xbtail[...]
        hout_ref[...] = hstate[...]


def _const_spec(shape):
    nd = len(shape)
    return pl.BlockSpec(shape, lambda b, t: (0,) * nd, pipeline_mode=pl.Buffered(1))


class _Stacked(NamedTuple):
    array: jax.Array
    layer: int


def _weight_spec(a):
    if not isinstance(a, _Stacked):
        return _const_spec(a.shape)
    shape = (1,) + a.array.shape[1:]
    rest = (0,) * (len(shape) - 1)
    return pl.BlockSpec(shape, lambda b, t: (a.layer,) + rest, pipeline_mode=pl.Buffered(1))


def _operand(a):
    return a.array if isinstance(a, _Stacked) else a


def _shapes(arrays):
    return tuple(jax.ShapeDtypeStruct(a.shape, a.dtype) for a in arrays)


def _even_layer(h, state, g, w_in, conv_a, conv_b, conv_b_bias, rw, rb, iw, ib, lam, w_out,
                state_at=None):
    bsz, tlen, _ = h.shape
    tile = min(STEP_T, tlen)
    row_spec = pl.BlockSpec((1, tile, D_MODEL), lambda b, t: (b, t, 0))
    args = tuple(state) + (
        g.reshape(1, D_MODEL), w_in, conv_a, conv_b,
        conv_b_bias.reshape(1, LRU_WIDTH), rw, rb.reshape(1, LRU_WIDTH), iw,
        ib.reshape(1, LRU_WIDTH), lam.reshape(1, LRU_WIDTH), w_out)
    out_shape = [jax.ShapeDtypeStruct(h.shape, F32)]
    out_specs = [row_spec]
    if state_at is not None:
        assert bsz == 1 and tlen == tile
        out_shape += list(_shapes(state))
        out_specs += [_const_spec(s.shape) for s in state]
    res = pl.pallas_call(
        functools.partial(_even_kernel, state_at=state_at),
        grid=(bsz, tlen // tile),
        in_specs=[row_spec] + [_weight_spec(a) for a in args],
        out_specs=out_specs,
        out_shape=out_shape,
        scratch_shapes=[pltpu.VMEM(s.shape, F32) for s in state],
        compiler_params=pltpu.CompilerParams(
            dimension_semantics=("arbitrary", "arbitrary"), vmem_limit_bytes=VMEM_LIMIT),
        name="even_mixer",
    )(h, *map(_operand, args))
    return res[0], (tuple(res[1:]) if state_at is not None else None)


def _block_diag(w):
    nh, d, _ = w.shape
    eye = jnp.eye(nh, dtype=w.dtype)
    return jnp.einsum("hij,hg->higj", w, eye).reshape(nh * d, nh * d)


def _ffn_kernel(*refs, attn_input, final_norm, state_at):
    refs = list(refs)
    x_ref, tin_ref = refs.pop(0), refs.pop(0)
    attn_ref, wo_ref = (refs.pop(0), refs.pop(0)) if attn_input else (None, None)
    g_ref, wup_ref, cw_ref, cb_ref, wdn_ref = refs[:5]
    del refs[:5]
    fg_ref = refs.pop(0) if final_norm else None
    o_ref = refs.pop(0)
    tout_ref = refs.pop(0) if state_at is not None else None
    slab_ref = refs.pop() if final_norm else None
    hn_ref, act_ref, tail = refs
    t = pl.program_id(1)

    @pl.when(t == 0)
    def _():
        tail[...] = tin_ref[...]

    if attn_input:
        wg = attn_ref.shape[3]
        y = x_ref[0]
        for grp in range(attn_ref.shape[1]):
            y = y + _dot(attn_ref[0, grp], wo_ref[grp * wg:(grp + 1) * wg, :])
        o_ref[0] = y
    else:
        o_ref[0] = x_ref[0]
    hn_ref[...] = _rms(o_ref[0], g_ref[...]).astype(BF16)
    for c in range(D_FF // FF_CHUNK):
        halves = []
        for part in range(2):
            col = part * D_FF + c * FF_CHUNK
            cs = slice(col, col + FF_CHUNK)
            u = _dot(hn_ref[...], wup_ref[0, :, cs])
            conv_u, tail[:, cs] = _causal_conv(u, tail[:, cs], cw_ref[:, cs], state_at)
            halves.append(conv_u + cb_ref[:, cs])
        a, gte = halves
        act_ref[:, c * FF_CHUNK:(c + 1) * FF_CHUNK] = (a * _sigmoid(a) * gte).astype(BF16)
    y = o_ref[0] + _dot(act_ref[...], wdn_ref[0])
    if final_norm:
        y = _rms(y, fg_ref[...])
        slabs = range(D_MODEL // LANES)
        for c in slabs:
            slab_ref[c] = y[:, c * LANES:(c + 1) * LANES]
        for n in range(y.shape[0] // TILE_T):
            for s in range(SUBLANES):
                src = pl.ds(n * TILE_T + s, GROUPS, stride=SUBLANES)
                dst = pl.ds(n * TILE_T + s * GROUPS, GROUPS)
                for c in slabs:
                    o_ref[0, dst, c * LANES:(c + 1) * LANES] = slab_ref[c, src, :]
    else:
        o_ref[0] = y
    if state_at is not None:
        tout_ref[...] = tail[...]


def _ffn_layer(h, tail, g, w_up, conv_w, conv_b, w_down, attn=None, final_g=None, state_at=None):
    bsz, tlen, _ = h.shape
    tile = min(STEP_T, tlen)
    row_spec = pl.BlockSpec((1, tile, D_MODEL), lambda b, t: (b, t, 0))
    args = (g.reshape(1, D_MODEL), w_up, conv_w, conv_b.reshape(1, 2 * D_FF), w_down)
    if final_g is not None:
        args = args + (final_g.reshape(1, D_MODEL),)
    specs = [_weight_spec(a) for a in args]
    if attn is not None:
        args = attn + args
        attn_spec = pl.BlockSpec((1, attn[0].shape[1], tile, attn[0].shape[3]),
                                 lambda b, t: (b, 0, t, 0))
        specs = [attn_spec, _const_spec(attn[1].shape)] + specs
    out_shape = [jax.ShapeDtypeStruct(h.shape, F32)]
    out_specs = [row_spec]
    if state_at is not None:
        assert bsz == 1 and tlen == tile
        out_shape.append(jax.ShapeDtypeStruct(tail.shape, F32))
        out_specs.append(_const_spec(tail.shape))
    res = pl.pallas_call(
        functools.partial(_ffn_kernel, attn_input=attn is not None,
                          final_norm=final_g is not None, state_at=state_at),
        grid=(bsz, tlen // tile),
        in_specs=[row_spec, _const_spec(tail.shape)] + specs,
        out_specs=out_specs,
        out_shape=out_shape,
        scratch_shapes=[
            pltpu.VMEM((tile, D_MODEL), BF16),
            pltpu.VMEM((tile, D_FF), BF16),
            pltpu.VMEM(tail.shape, F32),
        ] + ([pltpu.VMEM((D_MODEL // LANES, tile, LANES), F32)] if final_g is not None else []),
        compiler_params=pltpu.CompilerParams(
            dimension_semantics=("arbitrary", "arbitrary"), vmem_limit_bytes=VMEM_LIMIT),
        name="conv_ffn",
    )(h, tail, *map(_operand, args))
    return res[0], (res[1] if state_at is not None else None)


def _mla_proj_kernel(x_ref, g_ref, win_ref, qn_ref, kvn_ref, wqt_ref, wk_ref, wvt_ref,
                     cosk_ref, sink_ref, cosq_ref, sinq_ref, qt_ref, k_ref, vt_ref):
    x = x_ref[0]
    hn = _rms(x, g_ref[...]).astype(BF16)
    u = _dot(hn, win_ref[...])
    cq = u[:, 0:Q_LORA]
    ckv = u[:, Q_LORA:Q_LORA + KV_LORA]
    kr = u[:, Q_LORA + KV_LORA:Q_LORA + KV_LORA + HEAD_PAD]
    kr_rot = u[:, Q_LORA + KV_LORA + HEAD_PAD:]
    cqn = _rms(cq, qn_ref[...]).astype(BF16)
    ckvn = _rms(ckv, kvn_ref[...]).astype(BF16)

    qt = _dot_nt(wqt_ref[...], cqn)
    kn = _dot(ckvn, wk_ref[...])
    vt = _dot_nt(wvt_ref[...], ckvn)
    k_rope = kr * cosk_ref[...] + kr_rot * sink_ref[...]
    cosq = cosq_ref[...]
    sinq = sinq_ref[...]
    scale = QK_HEAD ** -0.5 * math.log2(math.e)
    zeros = jnp.zeros((HEAD_PAD - QK_HEAD, qt.shape[1]), F32)
    ones = jnp.ones((V_ROWS - V_HEAD, qt.shape[1]), F32)
    for h in range(MLA_HEADS):
        qh = qt[h * HEAD_PAD:(h + 1) * HEAD_PAD, :]
        roped = qh[QK_NOPE:QK_HEAD, :] * cosq + qh[QK_HEAD:, :] * sinq
        q_out = jnp.concatenate(
            [qh[0:QK_NOPE, :] * scale, roped * scale, zeros], axis=0).astype(BF16)
        k_out = (kn[:, h * HEAD_PAD:(h + 1) * HEAD_PAD] + k_rope).astype(BF16)
        v_out = jnp.concatenate(
            [vt[h * V_HEAD:(h + 1) * V_HEAD, :], ones], axis=0).astype(BF16)
        for i in range(qt_ref.shape[2]):
            rows = slice(i * TILE_T, (i + 1) * TILE_T)
            qt_ref[0, h, i] = q_out[:, rows]
            k_ref[0, h, i] = k_out[rows, :]
            vt_ref[0, h, i] = v_out[:, rows]


def _rot_cols(w):
    half = QK_ROPE // 2
    return jnp.concatenate([-w[..., half:], w[..., :half]], axis=-1)


def _mla_weights(w_in, w_uq, w_ukv):
    w_kr = w_in[:, Q_LORA + KV_LORA:]
    padl = jnp.zeros((D_MODEL, QK_NOPE), F32)
    padr = jnp.zeros((D_MODEL, HEAD_PAD - QK_HEAD), F32)
    win_ext = jnp.concatenate(
        [w_in[:, :Q_LORA + KV_LORA], padl, w_kr, padr, padl, _rot_cols(w_kr), padr],
        axis=-1).astype(BF16)
    wq = w_uq.reshape(Q_LORA, MLA_HEADS, QK_HEAD)
    wq_ext = jnp.concatenate([wq, _rot_cols(wq[..., QK_NOPE:])], axis=-1)
    wqt = wq_ext.reshape(Q_LORA, MLA_HEADS * HEAD_PAD).T.astype(BF16)
    wkv = w_ukv.reshape(KV_LORA, MLA_HEADS, QK_NOPE + V_HEAD)
    wk = jnp.concatenate(
        [wkv[..., :QK_NOPE], jnp.zeros((KV_LORA, MLA_HEADS, HEAD_PAD - QK_NOPE), F32)],
        axis=-1).reshape(KV_LORA, MLA_HEADS * HEAD_PAD).astype(BF16)
    wvt = wkv[..., QK_NOPE:].reshape(KV_LORA, MLA_HEADS * V_HEAD).T.astype(BF16)
    return win_ext, wqt, wk, wvt


def _rope_tables(first_pos, t_len):
    pos = first_pos + jnp.arange(t_len, dtype=F32)
    inv_freq = ROPE_BASE ** (-jnp.arange(0, QK_ROPE, 2, dtype=F32) / QK_ROPE)
    ang = _to_strided(pos[:, None] * inv_freq[None, :], 0)
    cos2 = jnp.concatenate([jnp.cos(ang)] * 2, axis=-1)
    sin2 = jnp.concatenate([jnp.sin(ang)] * 2, axis=-1)
    lpad = ((0, 0), (QK_NOPE, HEAD_PAD - QK_HEAD))
    return jnp.pad(cos2, lpad), jnp.pad(sin2, lpad), cos2.T, sin2.T


def _mla_proj(h, tables, g, q_norm, kv_norm, weights):
    bsz, tlen, _ = h.shape
    tile = TILE_T
    nt = tlen // tile
    step = min(STEP_T, tlen)
    per = step // tile
    win_ext, wqt, wk, wvt = weights
    args = (g.reshape(1, D_MODEL), win_ext, q_norm.reshape(1, Q_LORA),
            kv_norm.reshape(1, KV_LORA), wqt, wk, wvt)
    row_spec = pl.BlockSpec((1, step, D_MODEL), lambda b, t: (b, t, 0))
    in_specs = [row_spec] + [_const_spec(a.shape) for a in args] + [
        pl.BlockSpec((step, HEAD_PAD), lambda b, t: (t, 0)),
        pl.BlockSpec((step, HEAD_PAD), lambda b, t: (t, 0)),
        pl.BlockSpec((QK_ROPE, step), lambda b, t: (0, t)),
        pl.BlockSpec((QK_ROPE, step), lambda b, t: (0, t)),
    ]
    out_shape = (
        jax.ShapeDtypeStruct((bsz, MLA_HEADS, nt, HEAD_PAD, tile), BF16),
        jax.ShapeDtypeStruct((bsz, MLA_HEADS, nt, tile, HEAD_PAD), BF16),
        jax.ShapeDtypeStruct((bsz, MLA_HEADS, nt, V_ROWS, tile), BF16),
    )
    out_specs = (
        pl.BlockSpec((1, MLA_HEADS, per, HEAD_PAD, tile), lambda b, t: (b, 0, t, 0, 0)),
        pl.BlockSpec((1, MLA_HEADS, per, tile, HEAD_PAD), lambda b, t: (b, 0, t, 0, 0)),
        pl.BlockSpec((1, MLA_HEADS, per, V_ROWS, tile), lambda b, t: (b, 0, t, 0, 0)),
    )
    return pl.pallas_call(
        _mla_proj_kernel,
        grid=(bsz, tlen // step),
        in_specs=in_specs,
        out_specs=out_specs,
        out_shape=out_shape,
        compiler_params=pltpu.CompilerParams(
            dimension_semantics=("arbitrary", "arbitrary"), vmem_limit_bytes=VMEM_LIMIT),
        name="mla_proj",
    )(h, *args, *tables)


def _strided_time(i):
    return (i & (SUBLANES - 1)) * GROUPS + (i >> (SUBLANES.bit_length() - 1))


def _attn_kernel(*refs, has_meta):
    if has_meta:
        qt_ref, k_ref, vt_ref, km_ref, vtm_ref, o_ref, s_ref, smax_ref = refs
    else:
        qt_ref, k_ref, vt_ref, o_ref, s_ref, smax_ref = refs
    tile = qt_ref.shape[4]
    step = pl.program_id(2)
    neg = jnp.finfo(F32).min
    heads = range(HEADS_PER_STEP)

    def softmax_update(state, scores, vt, keep=None, block_max=None, extra=None):
        def block():
            s = scores()
            return s if keep is None else jnp.where(keep, s, neg)
        m, acc = state
        if block_max is None:
            block_max = jnp.max(block(), axis=0, keepdims=True)
        if extra is not None:
            block_max = jnp.maximum(block_max, jnp.max(extra[0], axis=0, keepdims=True))
        m_new = jnp.maximum(m, block_max)
        p = jnp.exp2(block() - m_new).astype(BF16)
        if extra is not None:
            p = jnp.concatenate([p, jnp.exp2(extra[0] - m_new).astype(BF16)], axis=0)
            vt = jnp.concatenate([vt, extra[1]], axis=1)
        return m_new, jnp.exp2(m - m_new) * acc + _dot(vt, p)

    def produce(sub, slot, j, hh):
        s = _dot(k_ref[0, hh, j], qt_ref[0, hh, sub])
        s_ref[slot, hh] = s
        smax_ref[slot, hh] = jnp.max(s, axis=0, keepdims=True)

    def consume(sub, carry, slot, j, next_j, keep=None, with_meta=False):
        if with_meta:
            meta = [(_dot(km_ref[hh], qt_ref[0, hh, sub]), vtm_ref[hh]) for hh in heads]
        if next_j is not None:
            for hh in range(SCORE_LEAD):
                produce(sub, 1 - slot, next_j, hh)
        out = []
        for hh in heads:
            out.append(softmax_update(
                carry[hh], lambda: s_ref[slot, hh], vt_ref[0, hh, j], keep,
                smax_ref[slot, hh] if keep is None else None,
                meta[hh] if with_meta else None))
            if next_j is not None and hh + SCORE_LEAD < HEADS_PER_STEP:
                produce(sub, 1 - slot, next_j, hh + SCORE_LEAD)
        return tuple(out)

    kpos = _strided_time(lax.broadcasted_iota(jnp.int32, (tile, tile), 0))
    qpos = _strided_time(lax.broadcasted_iota(jnp.int32, (tile, tile), 1))
    keep = kpos <= qpos
    subs = qt_ref.shape[2]
    for sub in range(subs):
        qi = subs * step + sub
        carry = tuple((jnp.full((1, tile), neg, F32), jnp.zeros((V_ROWS, tile), F32))
                      for _ in heads)
        for hh in heads:
            produce(sub, 0, 0, hh)

        def group(jj, c, sub=sub):
            for d in range(subs):
                c = consume(sub, c, d % 2, subs * jj + d, subs * jj + d + 1)
            return c

        carry = lax.fori_loop(0, step, group, carry)
        for d in range(sub):
            carry = consume(sub, carry, d % 2, subs * step + d, subs * step + d + 1)
        carry = consume(sub, carry, sub % 2, qi, None, keep, has_meta)
        outs = [acc[0:V_HEAD, :] / acc[V_HEAD:V_HEAD + 1, :] for _, acc in carry]
        o_ref[0, 0, sub * tile:(sub + 1) * tile, :] = (
            jnp.concatenate(outs, axis=0).T.astype(o_ref.dtype))


def _attention(qt, k5, vt5, meta_kv=None):
    bsz, nh, nt, tile, _ = k5.shape
    tlen = nt * tile
    hps = HEADS_PER_STEP
    subs = ATTN_Q_TILES if nt % ATTN_Q_TILES == 0 else 1
    assert subs % 2 == 0 or nt == 1
    in_specs = [
        pl.BlockSpec((1, hps, subs, HEAD_PAD, tile), lambda b, h, q: (b, h, q, 0, 0)),
        pl.BlockSpec((1, hps, nt, tile, HEAD_PAD), lambda b, h, q: (b, h, 0, 0, 0)),
        pl.BlockSpec((1, hps, nt, V_ROWS, tile), lambda b, h, q: (b, h, 0, 0, 0)),
    ]
    args = (qt, k5, vt5)
    if meta_kv is not None:
        in_specs += [pl.BlockSpec((hps,) + a.shape[1:], lambda b, h, q: (h, 0, 0))
                     for a in meta_kv]
        args += tuple(meta_kv)
    return pl.pallas_call(
        functools.partial(_attn_kernel, has_meta=meta_kv is not None),
        grid=(bsz, nh // hps, nt // subs),
        in_specs=in_specs,
        out_specs=pl.BlockSpec((1, 1, subs * tile, hps * V_HEAD), lambda b, h, q: (b, h, q, 0)),
        out_shape=jax.ShapeDtypeStruct((bsz, nh // hps, tlen, hps * V_HEAD), BF16),
        scratch_shapes=[pltpu.VMEM((2, hps, tile, tile), F32),
                        pltpu.VMEM((2, hps, 1, tile), F32)],
        compiler_params=pltpu.CompilerParams(
            dimension_semantics=("arbitrary", "arbitrary", "arbitrary"),
            vmem_limit_bytes=VMEM_LIMIT),
        name="mla_attention",
    )(*args)


def _to_strided(a, axis):
    shp = a.shape
    nt = shp[axis] // TILE_T
    a = a.reshape(shp[:axis] + (nt, SUBLANES, GROUPS) + shp[axis + 1:])
    return jnp.swapaxes(a, axis + 1, axis + 2).reshape(shp)


def _trunk(h, first_pos, states, p, state_at):
    depth = len(p["ffn"])
    tables = _rope_tables(first_pos, h.shape[1])
    new_states = []
    for layer in range(depth):
        st = states[layer]
        ffn = p["ffn"][layer]
        attn = None
        if layer % 2 == 0:
            h, mix_state = _even_layer(h, st["mix"], *p["even"][layer // 2], state_at=state_at)
        else:
            g, q_norm, kv_norm, weights, w_out = p["odd"][layer // 2]
            qt, k5, vt5 = _mla_proj(h, tables, g, q_norm, kv_norm, weights)
            attn = (_attention(qt, k5, vt5, st["mix"]), w_out)
            mix_state = None
            if state_at is not None:
                n = state_at + 1
                assert n <= GROUPS
                mix_state = (k5[0, :, 0, 0:n * SUBLANES:SUBLANES, :],
                             vt5[0, :, 0, :, 0:n * SUBLANES:SUBLANES])
        h, tail = _ffn_layer(h, st["ffn"], *ffn, attn=attn,
                             final_g=p["final"] if layer == depth - 1 else None,
                             state_at=state_at)
        new_states.append({"mix": mix_state, "ffn": tail})
    return h, new_states


def kernel(x, meta_tokens, ev_norm, ev_w_in, ev_conv_a, ev_conv_b, ev_conv_b_bias, ev_gate_r_w, ev_gate_r_b, ev_gate_i_w, ev_gate_i_b, ev_lru_lambda, ev_w_out, od_norm, od_w_in, od_q_norm, od_kv_norm, od_w_uq, od_w_ukv, od_w_out, ffn_norm, ffn_w_up, ffn_conv_w, ffn_conv_b, ffn_w_down, final_norm):
    bsz, seq, _ = x.shape
    depth = ffn_norm.shape[0]
    assert seq % TILE_T == 0 and N_META <= GROUPS
    ev_in, ev_out = ev_w_in.astype(BF16), ev_w_out.astype(BF16)
    w_up, w_down = ffn_w_up.astype(BF16), ffn_w_down.astype(BF16)
    params = {
        "even": [(ev_norm[j], _Stacked(ev_in, j), ev_conv_a[j], ev_conv_b[j],
                  ev_conv_b_bias[j], _block_diag(ev_gate_r_w[j]).astype(BF16), ev_gate_r_b[j],
                  _block_diag(ev_gate_i_w[j]).astype(BF16), ev_gate_i_b[j], ev_lru_lambda[j],
                  _Stacked(ev_out, j)) for j in range(ev_norm.shape[0])],
        "odd": [(od_norm[j], od_q_norm[j], od_kv_norm[j],
                 _mla_weights(od_w_in[j], od_w_uq[j], od_w_ukv[j]), od_w_out[j].astype(BF16))
                for j in range(od_norm.shape[0])],
        "ffn": [(ffn_norm[l], _Stacked(w_up, l), ffn_conv_w[l], ffn_conv_b[l],
                 _Stacked(w_down, l)) for l in range(depth)],
        "final": final_norm,
    }
    zero_states = []
    for layer in range(depth):
        mix = None
        if layer % 2 == 0:
            mix = (jnp.zeros(((ev_conv_a.shape[1] - 1) * SUBLANES, CONV_WIDTH), F32),
                   jnp.zeros(((ev_conv_b.shape[1] - 1) * SUBLANES, LRU_WIDTH), F32),
                   jnp.zeros((1, LRU_WIDTH), F32))
        zero_states.append(
            {"mix": mix, "ffn": jnp.zeros(((ffn_conv_w.shape[1] - 1) * SUBLANES, 2 * D_FF), F32)})

    meta_tile = jnp.concatenate(
        [meta_tokens.astype(x.dtype), jnp.zeros((TILE_T - N_META, D_MODEL), x.dtype)], axis=0)
    _, meta_states = _trunk(_to_strided(meta_tile[None], 1), 0.0, zero_states, params,
                            state_at=N_META - 1)
    h, _ = _trunk(_to_strided(x, 1), float(N_META), meta_states, params, state_at=None)
    return h
```

```python
import functools
import math
from typing import NamedTuple

import jax
import jax.numpy as jnp
from jax import lax
from jax.experimental import pallas as pl
from jax.experimental.pallas import tpu as pltpu

D_MODEL = 1024
N_META = 16
EPS = 1e-6
CONV_WIDTH = 512
LRU_WIDTH = 512
LRU_C = 8.0
MLA_HEADS = 16
QK_NOPE = 64
QK_ROPE = 32
QK_HEAD = QK_NOPE + QK_ROPE
V_HEAD = 64
Q_LORA = 384
KV_LORA = 256
ROPE_BASE = 10000.0
D_FF = 2816

LANES = 128
SUBLANES = 8
HEAD_PAD = 128
V_ROWS = V_HEAD + 16
HEADS_PER_STEP = 8
ATTN_Q_TILES = 8
SCORE_LEAD = 1
TILE_T = 256
GROUPS = TILE_T // SUBLANES
STEP_T = 1024
FF_CHUNK = 256
VMEM_LIMIT = 56 * 1024 * 1024

F32 = jnp.float32
BF16 = jnp.bfloat16


def _rms(x, g):
    ms = jnp.mean(x * x, axis=-1, keepdims=True)
    return x * lax.rsqrt(ms + EPS) * g


def _sigmoid(x):
    return 1.0 / (1.0 + jnp.exp(-x))


def _gelu_tanh(x):
    c = math.sqrt(2.0 / math.pi)
    return x * (0.5 * (1.0 + jnp.tanh(c * (x + 0.044715 * (x * x * x)))))


def _dot(a, b):
    return jnp.dot(a, b, preferred_element_type=F32)


def _dot_nt(a, b):
    return lax.dot_general(a, b, (((1,), (1,)), ((), ())), preferred_element_type=F32)


def _row_groups(x):
    return [x[v * SUBLANES:(v + 1) * SUBLANES, :] for v in range(x.shape[0] // SUBLANES)]


def _delays(u, prev, kmax):
    tile = u.shape[0]
    first = lax.broadcasted_iota(jnp.int32, (SUBLANES, u.shape[1]), 0) == 0
    wrapped = []
    for i in range(kmax):
        cur = u[tile - (kmax - i) * SUBLANES:tile - (kmax - i - 1) * SUBLANES, :]
        old = prev[i * SUBLANES:(i + 1) * SUBLANES, :]
        wrapped.append(jnp.where(first, pltpu.roll(old, 1, axis=0), pltpu.roll(cur, 1, axis=0)))
    return [jnp.concatenate(wrapped[kmax - k:] + [u[:tile - k * SUBLANES, :]], axis=0)
            for k in range(1, kmax + 1)]


def _tiles(x):
    return [x[i * TILE_T:(i + 1) * TILE_T, :] for i in range(x.shape[0] // TILE_T)]


def _stack(parts):
    return parts[0] if len(parts) == 1 else jnp.concatenate(parts, axis=0)


def _causal_conv(u, prev, taps, state_at=None):
    k = taps.shape[0]
    outs = []
    for ui in _tiles(u):
        delayed = _delays(ui, prev, k - 1)
        acc = delayed[k - 2] * taps[0:1, :]
        for j in range(1, k - 1):
            acc = acc + delayed[k - 2 - j] * taps[j:j + 1, :]
        outs.append(acc + ui * taps[k - 1:k, :])
        prev = _conv_tail(ui, k - 1, state_at)
    return _stack(outs), prev


def _conv_tail(u, kmax, state_at):
    tile = u.shape[0]
    if state_at is None:
        return u[tile - kmax * SUBLANES:, :]
    s, v = divmod(state_at, GROUPS)
    assert v >= kmax - 1
    rows = []
    for i in range(kmax):
        grp = u[(v - (kmax - 1 - i)) * SUBLANES:(v - (kmax - 2 - i)) * SUBLANES, :]
        rows.append(pltpu.roll(grp, SUBLANES - 1 - s, axis=0) if s != SUBLANES - 1 else grp)
    return jnp.concatenate(rows, axis=0)


def _lru_scan(a, b, carry, state_at=None):
    outs = []
    for ai, bi in zip(_tiles(a), _tiles(b)):
        hi, carry = _lru_scan_tile(ai, bi, carry, state_at)
        outs.append(hi)
    return _stack(outs), carry


def _lru_scan_tile(a, b, carry, state_at):
    a_rows, b_rows = _row_groups(a), _row_groups(b)
    prod, hzero = [a_rows[0]], [b_rows[0]]
    for v in range(1, len(a_rows)):
        prod.append(a_rows[v] * prod[-1])
        hzero.append(a_rows[v] * hzero[-1] + b_rows[v])
    pa, ph = prod[-1], hzero[-1]
    sub = lax.broadcasted_iota(jnp.int32, pa.shape, 0)
    for d in (1, 2, 4):
        ok = sub >= d
        ph = jnp.where(ok, pa * pltpu.roll(ph, d, axis=0) + ph, ph)
        pa = jnp.where(ok, pa * pltpu.roll(pa, d, axis=0), pa)
    ends = ph + pa * carry
    init = jnp.where(sub == 0, carry, pltpu.roll(ends, 1, axis=0))
    h = jnp.concatenate([hz + pr * init for hz, pr in zip(hzero, prod)], axis=0)
    if state_at is None:
        return h, ends[SUBLANES - 1:SUBLANES, :]
    s, v = divmod(state_at, GROUPS)
    row = v * SUBLANES + s
    return h, h[row:row + 1, :]


def _even_kernel(*refs, state_at):
    (x_ref, zin_ref, xbin_ref, hin_ref, g_ref, win_ref, ca_ref, cb_ref, cbb_ref, rw_ref, rb_ref,
     iw_ref, ib_ref, lam_ref, wout_ref, o_ref) = refs[:16]
    ztail, xbtail, hstate = refs[-3:]
    t = pl.program_id(1)

    @pl.when(t == 0)
    def _():
        ztail[...] = zin_ref[...]
        xbtail[...] = xbin_ref[...]
        hstate[...] = hin_ref[...]

    hn = _rms(x_ref[0], g_ref[...]).astype(BF16)
    u = _dot(hn, win_ref[0])
    cw = CONV_WIDTH
    gb = u[:, 0:cw]
    gc = u[:, cw:2 * cw]
    xa = u[:, 2 * cw:3 * cw]
    xb = u[:, 3 * cw:3 * cw + LRU_WIDTH]
    gate = u[:, 3 * cw + LRU_WIDTH:]

    conv_z, ztail[...] = _causal_conv(gc * xa, ztail[...], ca_ref[...], state_at)
    y_a = gb * conv_z

    conv_xb, xbtail[...] = _causal_conv(xb, xbtail[...], cb_ref[...], state_at)
    xc = conv_xb + cbb_ref[...]

    xcb = xc.astype(BF16)
    r = _sigmoid(_dot(xcb, rw_ref[...]) + rb_ref[...])
    i = _sigmoid(_dot(xcb, iw_ref[...]) + ib_ref[...])
    nlam = -lam_ref[...]
    softplus = jnp.maximum(nlam, 0.0) + jnp.log1p(jnp.exp(-jnp.abs(nlam)))
    log_a = -LRU_C * r * softplus
    a = jnp.exp(log_a)
    th = jnp.tanh(log_a)
    mult = jnp.sqrt(-2.0 * th / (1.0 - th))
    h, hstate[...] = _lru_scan(a, mult * (i * xc), hstate[...], state_at)

    y_b = _gelu_tanh(gate) * h
    y = jnp.concatenate([y_a, y_b], axis=-1).astype(BF16)
    o_ref[0] = x_ref[0] + _dot(y, wout_ref[0])
    if state_at is not None:
        zout_ref, xbout_ref, hout_ref = refs[16:19]
        zout_ref[...] = ztail[...]
        xbout_ref[...] = xbtail[...]
        hout_ref[...] = hstate[...]


def _const_spec(shape):
    nd = len(shape)
    return pl.BlockSpec(shape, lambda b, t: (0,) * nd, pipeline_mode=pl.Buffered(1))


class _Stacked(NamedTuple):
    array: jax.Array
    layer: int


def _weight_spec(a):
    if not isinstance(a, _Stacked):
        return _const_spec(a.shape)
    shape = (1,) + a.array.shape[1:]
    rest = (0,) * (len(shape) - 1)
    return pl.BlockSpec(shape, lambda b, t: (a.layer,) + rest, pipeline_mode=pl.Buffered(1))


def _operand(a):
    return a.array if isinstance(a, _Stacked) else a


def _shapes(arrays):
    return tuple(jax.ShapeDtypeStruct(a.shape, a.dtype) for a in arrays)


def _even_layer(h, state, g, w_in, conv_a, conv_b, conv_b_bias, rw, rb, iw, ib, lam, w_out,
                state_at=None):
    bsz, tlen, _ = h.shape
    tile = min(STEP_T, tlen)
    row_spec = pl.BlockSpec((1, tile, D_MODEL), lambda b, t: (b, t, 0))
    args = tuple(state) + (
        g.reshape(1, D_MODEL), w_in, conv_a, conv_b,
        conv_b_bias.reshape(1, LRU_WIDTH), rw, rb.reshape(1, LRU_WIDTH), iw,
        ib.reshape(1, LRU_WIDTH), lam.reshape(1, LRU_WIDTH), w_out)
    out_shape = [jax.ShapeDtypeStruct(h.shape, F32)]
    out_specs = [row_spec]
    if state_at is not None:
        assert bsz == 1 and tlen == tile
        out_shape += list(_shapes(state))
        out_specs += [_const_spec(s.shape) for s in state]
    res = pl.pallas_call(
        functools.partial(_even_kernel, state_at=state_at),
        grid=(bsz, tlen // tile),
        in_specs=[row_spec] + [_weight_spec(a) for a in args],
        out_specs=out_specs,
        out_shape=out_shape,
        scratch_shapes=[pltpu.VMEM(s.shape, F32) for s in state],
        compiler_params=pltpu.CompilerParams(
            dimension_semantics=("arbitrary", "arbitrary"), vmem_limit_bytes=VMEM_LIMIT),
        name="even_mixer",
    )(h, *map(_operand, args))
    return res[0], (tuple(res[1:]) if state_at is not None else None)


def _block_diag(w):
    nh, d, _ = w.shape
    eye = jnp.eye(nh, dtype=w.dtype)
    return jnp.einsum("hij,hg->higj", w, eye).reshape(nh * d, nh * d)


def _ffn_kernel(*refs, attn_input, final_norm, state_at):
    refs = list(refs)
    x_ref, tin_ref = refs.pop(0), refs.pop(0)
    attn_ref, wo_ref = (refs.pop(0), refs.pop(0)) if attn_input else (None, None)
    g_ref, wup_ref, cw_ref, cb_ref, wdn_ref = refs[:5]
    del refs[:5]
    fg_ref = refs.pop(0) if final_norm else None
    o_ref = refs.pop(0)
    tout_ref = refs.pop(0) if state_at is not None else None
    slab_ref = refs.pop() if final_norm else None
    hn_ref, act_ref, tail = refs
    t = pl.program_id(1)

    @pl.when(t == 0)
    def _():
        tail[...] = tin_ref[...]

    if attn_input:
        wg = attn_ref.shape[3]
        y = x_ref[0]
        for grp in range(attn_ref.shape[1]):
            y = y + _dot(attn_ref[0, grp], wo_ref[grp * wg:(grp + 1) * wg, :])
        o_ref[0] = y
    else:
        o_ref[0] = x_ref[0]
    hn_ref[...] = _rms(o_ref[0], g_ref[...]).astype(BF16)
    for c in range(D_FF // FF_CHUNK):
        halves = []
        for part in range(2):
            col = part * D_FF + c * FF_CHUNK
            cs = slice(col, col + FF_CHUNK)
            u = _dot(hn_ref[...], wup_ref[0, :, cs])
            conv_u, tail[:, cs] = _causal_conv(u, tail[:, cs], cw_ref[:, cs], state_at)
            halves.append(conv_u + cb_ref[:, cs])
        a, gte = halves
        act_ref[:, c * FF_CHUNK:(c + 1) * FF_CHUNK] = (a * _sigmoid(a) * gte).astype(BF16)
    y = o_ref[0] + _dot(act_ref[...], wdn_ref[0])
    if final_norm:
        y = _rms(y, fg_ref[...])
        slabs = range(D_MODEL // LANES)
        for c in slabs:
            slab_ref[c] = y[:, c * LANES:(c + 1) * LANES]
        for n in range(y.shape[0] // TILE_T):
            for s in range(SUBLANES):
                src = pl.ds(n * TILE_T + s, GROUPS, stride=SUBLANES)
                dst = pl.ds(n * TILE_T + s * GROUPS, GROUPS)
                for c in slabs:
                    o_ref[0, dst, c * LANES:(c + 1) * LANES] = slab_ref[c, src, :]
    else:
        o_ref[0] = y
    if state_at is not None:
        tout_ref[...] = tail[...]


def _ffn_layer(h, tail, g, w_up, conv_w, conv_b, w_down, attn=None, final_g=None, state_at=None):
    bsz, tlen, _ = h.shape
    tile = min(STEP_T, tlen)
    row_spec = pl.BlockSpec((1, tile, D_MODEL), lambda b, t: (b, t, 0))
    args = (g.reshape(1, D_MODEL), w_up, conv_w, conv_b.reshape(1, 2 * D_FF), w_down)
    if final_g is not None:
        args = args + (final_g.reshape(1, D_MODEL),)
    specs = [_weight_spec(a) for a in args]
    if attn is not None:
        args = attn + args
        attn_spec = pl.BlockSpec((1, attn[0].shape[1], tile, attn[0].shape[3]),
                                 lambda b, t: (b, 0, t, 0))
        specs = [attn_spec, _const_spec(attn[1].shape)] + specs
    out_shape = [jax.ShapeDtypeStruct(h.shape, F32)]
    out_specs = [row_spec]
    if state_at is not None:
        assert bsz == 1 and tlen == tile
        out_shape.append(jax.ShapeDtypeStruct(tail.shape, F32))
        out_specs.append(_const_spec(tail.shape))
    res = pl.pallas_call(
        functools.partial(_ffn_kernel, attn_input=attn is not None,
                          final_norm=final_g is not None, state_at=state_at),
        grid=(bsz, tlen // tile),
        in_specs=[row_spec, _const_spec(tail.shape)] + specs,
        out_specs=out_specs,
        out_shape=out_shape,
        scratch_shapes=[
            pltpu.VMEM((tile, D_MODEL), BF16),
            pltpu.VMEM((tile, D_FF), BF16),
            pltpu.VMEM(tail.shape, F32),
        ] + ([pltpu.VMEM((D_MODEL // LANES, tile, LANES), F32)] if final_g is not None else []),
        compiler_params=pltpu.CompilerParams(
            dimension_semantics=("arbitrary", "arbitrary"), vmem_limit_bytes=VMEM_LIMIT),
        name="conv_ffn",
    )(h, tail, *map(_operand, args))
    return res[0], (res[1] if state_at is not None else None)


def _mla_proj_kernel(x_ref, g_ref, win_ref, qn_ref, kvn_ref, wqt_ref, wk_ref, wvt_ref,
                     cosk_ref, sink_ref, cosq_ref, sinq_ref, qt_ref, k_ref, vt_ref):
    x = x_ref[0]
    hn = _rms(x, g_ref[...]).astype(BF16)
    u = _dot(hn, win_ref[...])
    cq = u[:, 0:Q_LORA]
    ckv = u[:, Q_LORA:Q_LORA + KV_LORA]
    kr = u[:, Q_LORA + KV_LORA:Q_LORA + KV_LORA + HEAD_PAD]
    kr_rot = u[:, Q_LORA + KV_LORA + HEAD_PAD:]
    cqn = _rms(cq, qn_ref[...]).astype(BF16)
    ckvn = _rms(ckv, kvn_ref[...]).astype(BF16)

    qt = _dot_nt(wqt_ref[...], cqn)
    kn = _dot(ckvn, wk_ref[...])
    vt = _dot_nt(wvt_ref[...], ckvn)
    k_rope = kr * cosk_ref[...] + kr_rot * sink_ref[...]
    cosq = cosq_ref[...]
    sinq = sinq_ref[...]
    scale = QK_HEAD ** -0.5 * math.log2(math.e)
    zeros = jnp.zeros((HEAD_PAD - QK_HEAD, qt.shape[1]), F32)
    ones = jnp.ones((V_ROWS - V_HEAD, qt.shape[1]), F32)
    for h in range(MLA_HEADS):
        qh = qt[h * HEAD_PAD:(h + 1) * HEAD_PAD, :]
        roped = qh[QK_NOPE:QK_HEAD, :] * cosq + qh[QK_HEAD:, :] * sinq
        q_out = jnp.concatenate(
            [qh[0:QK_NOPE, :] * scale, roped * scale, zeros], axis=0).astype(BF16)
        k_out = (kn[:, h * HEAD_PAD:(h + 1) * HEAD_PAD] + k_rope).astype(BF16)
        v_out = jnp.concatenate(
            [vt[h * V_HEAD:(h + 1) * V_HEAD, :], ones], axis=0).astype(BF16)
        for i in range(qt_ref.shape[2]):
            rows = slice(i * TILE_T, (i + 1) * TILE_T)
            qt_ref[0, h, i] = q_out[:, rows]
            k_ref[0, h, i] = k_out[rows, :]
            vt_ref[0, h, i] = v_out[:, rows]


def _rot_cols(w):
    half = QK_ROPE // 2
    return jnp.concatenate([-w[..., half:], w[..., :half]], axis=-1)


def _mla_weights(w_in, w_uq, w_ukv):
    w_kr = w_in[:, Q_LORA + KV_LORA:]
    padl = jnp.zeros((D_MODEL, QK_NOPE), F32)
    padr = jnp.zeros((D_MODEL, HEAD_PAD - QK_HEAD), F32)
    win_ext = jnp.concatenate(
        [w_in[:, :Q_LORA + KV_LORA], padl, w_kr, padr, padl, _rot_cols(w_kr), padr],
        axis=-1).astype(BF16)
    wq = w_uq.reshape(Q_LORA, MLA_HEADS, QK_HEAD)
    wq_ext = jnp.concatenate([wq, _rot_cols(wq[..., QK_NOPE:])], axis=-1)
    wqt = wq_ext.reshape(Q_LORA, MLA_HEADS * HEAD_PAD).T.astype(BF16)
    wkv = w_ukv.reshape(KV_LORA, MLA_HEADS, QK_NOPE + V_HEAD)
    wk = jnp.concatenate(
        [wkv[..., :QK_NOPE], jnp.zeros((KV_LORA, MLA_HEADS, HEAD_PAD - QK_NOPE), F32)],
        axis=-1).reshape(KV_LORA, MLA_HEADS * HEAD_PAD).astype(BF16)
    wvt = wkv[..., QK_NOPE:].reshape(KV_LORA, MLA_HEADS * V_HEAD).T.astype(BF16)
    return win_ext, wqt, wk, wvt


def _rope_tables(first_pos, t_len):
    pos = first_pos + jnp.arange(t_len, dtype=F32)
    inv_freq = ROPE_BASE ** (-jnp.arange(0, QK_ROPE, 2, dtype=F32) / QK_ROPE)
    ang = _to_strided(pos[:, None] * inv_freq[None, :], 0)
    cos2 = jnp.concatenate([jnp.cos(ang)] * 2, axis=-1)
    sin2 = jnp.concatenate([jnp.sin(ang)] * 2, axis=-1)
    lpad = ((0, 0), (QK_NOPE, HEAD_PAD - QK_HEAD))
    return jnp.pad(cos2, lpad), jnp.pad(sin2, lpad), cos2.T, sin2.T


def _mla_proj(h, tables, g, q_norm, kv_norm, weights):
    bsz, tlen, _ = h.shape
    tile = TILE_T
    nt = tlen // tile
    step = min(STEP_T, tlen)
    per = step // tile
    win_ext, wqt, wk, wvt = weights
    args = (g.reshape(1, D_MODEL), win_ext, q_norm.reshape(1, Q_LORA),
            kv_norm.reshape(1, KV_LORA), wqt, wk, wvt)
    row_spec = pl.BlockSpec((1, step, D_MODEL), lambda b, t: (b, t, 0))
    in_specs = [row_spec] + [_const_spec(a.shape) for a in args] + [
        pl.BlockSpec((step, HEAD_PAD), lambda b, t: (t, 0)),
        pl.BlockSpec((step, HEAD_PAD), lambda b, t: (t, 0)),
        pl.BlockSpec((QK_ROPE, step), lambda b, t: (0, t)),
        pl.BlockSpec((QK_ROPE, step), lambda b, t: (0, t)),
    ]
    out_shape = (
        jax.ShapeDtypeStruct((bsz, MLA_HEADS, nt, HEAD_PAD, tile), BF16),
        jax.ShapeDtypeStruct((bsz, MLA_HEADS, nt, tile, HEAD_PAD), BF16),
        jax.ShapeDtypeStruct((bsz, MLA_HEADS, nt, V_ROWS, tile), BF16),
    )
    out_specs = (
        pl.BlockSpec((1, MLA_HEADS, per, HEAD_PAD, tile), lambda b, t: (b, 0, t, 0, 0)),
        pl.BlockSpec((1, MLA_HEADS, per, tile, HEAD_PAD), lambda b, t: (b, 0, t, 0, 0)),
        pl.BlockSpec((1, MLA_HEADS, per, V_ROWS, tile), lambda b, t: (b, 0, t, 0, 0)),
    )
    return pl.pallas_call(
        _mla_proj_kernel,
        grid=(bsz, tlen // step),
        in_specs=in_specs,
        out_specs=out_specs,
        out_shape=out_shape,
        compiler_params=pltpu.CompilerParams(
            dimension_semantics=("arbitrary", "arbitrary"), vmem_limit_bytes=VMEM_LIMIT),
        name="mla_proj",
    )(h, *args, *tables)


def _strided_time(i):
    return (i & (SUBLANES - 1)) * GROUPS + (i >> (SUBLANES.bit_length() - 1))


def _attn_kernel(*refs, has_meta):
    if has_meta:
        qt_ref, k_ref, vt_ref, km_ref, vtm_ref, o_ref, s_ref, smax_ref = refs
    else:
        qt_ref, k_ref, vt_ref, o_ref, s_ref, smax_ref = refs
    tile = qt_ref.shape[4]
    step = pl.program_id(2)
    neg = jnp.finfo(F32).min
    heads = range(HEADS_PER_STEP)

    def softmax_update(state, scores, vt, keep=None, block_max=None, extra=None):
        def block():
            s = scores()
            return s if keep is None else jnp.where(keep, s, neg)
        m, acc = state
        if block_max is None:
            block_max = jnp.max(block(), axis=0, keepdims=True)
        if extra is not None:
            block_max = jnp.maximum(block_max, jnp.max(extra[0], axis=0, keepdims=True))
        m_new = jnp.maximum(m, block_max)
        p = jnp.exp2(block() - m_new).astype(BF16)
        if extra is not None:
            p = jnp.concatenate([p, jnp.exp2(extra[0] - m_new).astype(BF16)], axis=0)
            vt = jnp.concatenate([vt, extra[1]], axis=1)
        return m_new, jnp.exp2(m - m_new) * acc + _dot(vt, p)

    def produce(sub, slot, j, hh):
        s = _dot(k_ref[0, hh, j], qt_ref[0, hh, sub])
        s_ref[slot, hh] = s
        smax_ref[slot, hh] = jnp.max(s, axis=0, keepdims=True)

    def consume(sub, carry, slot, j, next_j, keep=None, with_meta=False):
        if with_meta:
            meta = [(_dot(km_ref[hh], qt_ref[0, hh, sub]), vtm_ref[hh]) for hh in heads]
        if next_j is not None:
            for hh in range(SCORE_LEAD):
                produce(sub, 1 - slot, next_j, hh)
        out = []
        for hh in heads:
            out.append(softmax_update(
                carry[hh], lambda: s_ref[slot, hh], vt_ref[0, hh, j], keep,
                smax_ref[slot, hh] if keep is None else None,
                meta[hh] if with_meta else None))
            if next_j is not None and hh + SCORE_LEAD < HEADS_PER_STEP:
                produce(sub, 1 - slot, next_j, hh + SCORE_LEAD)
        return tuple(out)

    kpos = _strided_time(lax.broadcasted_iota(jnp.int32, (tile, tile), 0))
    qpos = _strided_time(lax.broadcasted_iota(jnp.int32, (tile, tile), 1))
    keep = kpos <= qpos
    subs = qt_ref.shape[2]
    for sub in range(subs):
        qi = subs * step + sub
        carry = tuple((jnp.full((1, tile), neg, F32), jnp.zeros((V_ROWS, tile), F32))
                      for _ in heads)
        for hh in heads:
            produce(sub, 0, 0, hh)

        def group(jj, c, sub=sub):
            for d in range(subs):
                c = consume(sub, c, d % 2, subs * jj + d, subs * jj + d + 1)
            return c

        carry = lax.fori_loop(0, step, group, carry)
        for d in range(sub):
            carry = consume(sub, carry, d % 2, subs * step + d, subs * step + d + 1)
        carry = consume(sub, carry, sub % 2, qi, None, keep, has_meta)
        outs = [acc[0:V_HEAD, :] / acc[V_HEAD:V_HEAD + 1, :] for _, acc in carry]
        o_ref[0, 0, sub * tile:(sub + 1) * tile, :] = (
            jnp.concatenate(outs, axis=0).T.astype(o_ref.dtype))


def _attention(qt, k5, vt5, meta_kv=None):
    bsz, nh, nt, tile, _ = k5.shape
    tlen = nt * tile
    hps = HEADS_PER_STEP
    subs = ATTN_Q_TILES if nt % ATTN_Q_TILES == 0 else 1
    assert subs % 2 == 0 or nt == 1
    in_specs = [
        pl.BlockSpec((1, hps, subs, HEAD_PAD, tile), lambda b, h, q: (b, h, q, 0, 0)),
        pl.BlockSpec((1, hps, nt, tile, HEAD_PAD), lambda b, h, q: (b, h, 0, 0, 0)),
        pl.BlockSpec((1, hps, nt, V_ROWS, tile), lambda b, h, q: (b, h, 0, 0, 0)),
    ]
    args = (qt, k5, vt5)
    if meta_kv is not None:
        in_specs += [pl.BlockSpec((hps,) + a.shape[1:], lambda b, h, q: (h, 0, 0))
                     for a in meta_kv]
        args += tuple(meta_kv)
    return pl.pallas_call(
        functools.partial(_attn_kernel, has_meta=meta_kv is not None),
        grid=(bsz, nh // hps, nt // subs),
        in_specs=in_specs,
        out_specs=pl.BlockSpec((1, 1, subs * tile, hps * V_HEAD), lambda b, h, q: (b, h, q, 0)),
        out_shape=jax.ShapeDtypeStruct((bsz, nh // hps, tlen, hps * V_HEAD), BF16),
        scratch_shapes=[pltpu.VMEM((2, hps, tile, tile), F32),
                        pltpu.VMEM((2, hps, 1, tile), F32)],
        compiler_params=pltpu.CompilerParams(
            dimension_semantics=("arbitrary", "arbitrary", "arbitrary"),
            vmem_limit_bytes=VMEM_LIMIT),
        name="mla_attention",
    )(*args)


def _to_strided(a, axis):
    shp = a.shape
    nt = shp[axis] // TILE_T
    a = a.reshape(shp[:axis] + (nt, SUBLANES, GROUPS) + shp[axis + 1:])
    return jnp.swapaxes(a, axis + 1, axis + 2).reshape(shp)


def _trunk(h, first_pos, states, p, state_at):
    depth = len(p["ffn"])
    tables = _rope_tables(first_pos, h.shape[1])
    new_states = []
    for layer in range(depth):
        st = states[layer]
        ffn = p["ffn"][layer]
        attn = None
        if layer % 2 == 0:
            h, mix_state = _even_layer(h, st["mix"], *p["even"][layer // 2], state_at=state_at)
        else:
            g, q_norm, kv_norm, weights, w_out = p["odd"][layer // 2]
            qt, k5, vt5 = _mla_proj(h, tables, g, q_norm, kv_norm, weights)
            attn = (_attention(qt, k5, vt5, st["mix"]), w_out)
            mix_state = None
            if state_at is not None:
                n = state_at + 1
                assert n <= GROUPS
                mix_state = (k5[0, :, 0, 0:n * SUBLANES:SUBLANES, :],
                             vt5[0, :, 0, :, 0:n * SUBLANES:SUBLANES])
        h, tail = _ffn_layer(h, st["ffn"], *ffn, attn=attn,
                             final_g=p["final"] if layer == depth - 1 else None,
                             state_at=state_at)
        new_states.append({"mix": mix_state, "ffn": tail})
    return h, new_states


def kernel(x, meta_tokens, ev_norm, ev_w_in, ev_conv_a, ev_conv_b, ev_conv_b_bias, ev_gate_r_w, ev_gate_r_b, ev_gate_i_w, ev_gate_i_b, ev_lru_lambda, ev_w_out, od_norm, od_w_in, od_q_norm, od_kv_norm, od_w_uq, od_w_ukv, od_w_out, ffn_norm, ffn_w_up, ffn_conv_w, ffn_conv_b, ffn_w_down, final_norm):
    bsz, seq, _ = x.shape
    depth = ffn_norm.shape[0]
    assert seq % TILE_T == 0 and N_META <= GROUPS
    ev_in, ev_out = ev_w_in.astype(BF16), ev_w_out.astype(BF16)
    w_up, w_down = ffn_w_up.astype(BF16), ffn_w_down.astype(BF16)
    params = {
        "even": [(ev_norm[j], _Stacked(ev_in, j), ev_conv_a[j], ev_conv_b[j],
                  ev_conv_b_bias[j], _block_diag(ev_gate_r_w[j]).astype(BF16), ev_gate_r_b[j],
                  _block_diag(ev_gate_i_w[j]).astype(BF16), ev_gate_i_b[j], ev_lru_lambda[j],
                  _Stacked(ev_out, j)) for j in range(ev_norm.shape[0])],
        "odd": [(od_norm[j], od_q_norm[j], od_kv_norm[j],
                 _mla_weights(od_w_in[j], od_w_uq[j], od_w_ukv[j]), od_w_out[j].astype(BF16))
                for j in range(od_norm.shape[0])],
        "ffn": [(ffn_norm[l], _Stacked(w_up, l), ffn_conv_w[l], ffn_conv_b[l],
                 _Stacked(w_down, l)) for l in range(depth)],
        "final": final_norm,
    }
    zero_states = []
    for layer in range(depth):
        mix = None
        if layer % 2 == 0:
            mix = (jnp.zeros(((ev_conv_a.shape[1] - 1) * SUBLANES, CONV_WIDTH), F32),
                   jnp.zeros(((ev_conv_b.shape[1] - 1) * SUBLANES, LRU_WIDTH), F32),
                   jnp.zeros((1, LRU_WIDTH), F32))
        zero_states.append(
            {"mix": mix, "ffn": jnp.zeros(((ffn_conv_w.shape[1] - 1) * SUBLANES, 2 * D_FF), F32)})

    meta_tile = jnp.concatenate(
        [meta_tokens.astype(x.dtype), jnp.zeros((TILE_T - N_META, D_MODEL), x.dtype)], axis=0)
    _, meta_states = _trunk(_to_strided(meta_tile[None], 1), 0.0, zero_states, params,
                            state_at=N_META - 1)
    h, _ = _trunk(_to_strided(x, 1), float(N_META), meta_states, params, state_at=None)
    return h
```

```python
import functools
import math
from typing import NamedTuple

import jax
import jax.numpy as jnp
from jax import lax
from jax.experimental import pallas as pl
from jax.experimental.pallas import tpu as pltpu

D_MODEL = 1024
N_META = 16
EPS = 1e-6
CONV_WIDTH = 512
LRU_WIDTH = 512
LRU_C = 8.0
MLA_HEADS = 16
QK_NOPE = 64
QK_ROPE = 32
QK_HEAD = QK_NOPE + QK_ROPE
V_HEAD = 64
Q_LORA = 384
KV_LORA = 256
ROPE_BASE = 10000.0
D_FF = 2816

LANES = 128
SUBLANES = 8
HEAD_PAD = 128
V_ROWS = V_HEAD + 16
HEADS_PER_STEP = 8
ATTN_Q_TILES = 2
LOOP_TILES = 4
SCORE_LEAD = 1
TILE_T = 256
GROUPS = TILE_T // SUBLANES
STEP_T = 1024
FF_CHUNK = 256
VMEM_LIMIT = 56 * 1024 * 1024

F32 = jnp.float32
BF16 = jnp.bfloat16


def _rms(x, g):
    ms = jnp.mean(x * x, axis=-1, keepdims=True)
    return x * lax.rsqrt(ms + EPS) * g


def _sigmoid(x):
    return 1.0 / (1.0 + jnp.exp(-x))


def _gelu_tanh(x):
    c = math.sqrt(2.0 / math.pi)
    return x * (0.5 * (1.0 + jnp.tanh(c * (x + 0.044715 * (x * x * x)))))


def _dot(a, b):
    return jnp.dot(a, b, preferred_element_type=F32)


def _dot_nt(a, b):
    return lax.dot_general(a, b, (((1,), (1,)), ((), ())), preferred_element_type=F32)


def _row_groups(x):
    return [x[v * SUBLANES:(v + 1) * SUBLANES, :] for v in range(x.shape[0] // SUBLANES)]


def _delays(u, prev, kmax):
    tile = u.shape[0]
    first = lax.broadcasted_iota(jnp.int32, (SUBLANES, u.shape[1]), 0) == 0
    wrapped = []
    for i in range(kmax):
        cur = u[tile - (kmax - i) * SUBLANES:tile - (kmax - i - 1) * SUBLANES, :]
        old = prev[i * SUBLANES:(i + 1) * SUBLANES, :]
        wrapped.append(jnp.where(first, pltpu.roll(old, 1, axis=0), pltpu.roll(cur, 1, axis=0)))
    return [jnp.concatenate(wrapped[kmax - k:] + [u[:tile - k * SUBLANES, :]], axis=0)
            for k in range(1, kmax + 1)]


def _tiles(x):
    return [x[i * TILE_T:(i + 1) * TILE_T, :] for i in range(x.shape[0] // TILE_T)]


def _stack(parts):
    return parts[0] if len(parts) == 1 else jnp.concatenate(parts, axis=0)


def _causal_conv(u, prev, taps, state_at=None):
    k = taps.shape[0]
    outs = []
    for ui in _tiles(u):
        delayed = _delays(ui, prev, k - 1)
        acc = delayed[k - 2] * taps[0:1, :]
        for j in range(1, k - 1):
            acc = acc + delayed[k - 2 - j] * taps[j:j + 1, :]
        outs.append(acc + ui * taps[k - 1:k, :])
        prev = _conv_tail(ui, k - 1, state_at)
    return _stack(outs), prev


def _conv_tail(u, kmax, state_at):
    tile = u.shape[0]
    if state_at is None:
        return u[tile - kmax * SUBLANES:, :]
    s, v = divmod(state_at, GROUPS)
    assert v >= kmax - 1
    rows = []
    for i in range(kmax):
        grp = u[(v - (kmax - 1 - i)) * SUBLANES:(v - (kmax - 2 - i)) * SUBLANES, :]
        rows.append(pltpu.roll(grp, SUBLANES - 1 - s, axis=0) if s != SUBLANES - 1 else grp)
    return jnp.concatenate(rows, axis=0)


def _lru_scan(a, b, carry, state_at=None):
    outs = []
    for ai, bi in zip(_tiles(a), _tiles(b)):
        hi, carry = _lru_scan_tile(ai, bi, carry, state_at)
        outs.append(hi)
    return _stack(outs), carry


def _lru_scan_tile(a, b, carry, state_at):
    a_rows, b_rows = _row_groups(a), _row_groups(b)
    prod, hzero = [a_rows[0]], [b_rows[0]]
    for v in range(1, len(a_rows)):
        prod.append(a_rows[v] * prod[-1])
        hzero.append(a_rows[v] * hzero[-1] + b_rows[v])
    pa, ph = prod[-1], hzero[-1]
    sub = lax.broadcasted_iota(jnp.int32, pa.shape, 0)
    for d in (1, 2, 4):
        ok = sub >= d
        ph = jnp.where(ok, pa * pltpu.roll(ph, d, axis=0) + ph, ph)
        pa = jnp.where(ok, pa * pltpu.roll(pa, d, axis=0), pa)
    ends = ph + pa * carry
    init = jnp.where(sub == 0, carry, pltpu.roll(ends, 1, axis=0))
    h = jnp.concatenate([hz + pr * init for hz, pr in zip(hzero, prod)], axis=0)
    if state_at is None:
        return h, ends[SUBLANES - 1:SUBLANES, :]
    s, v = divmod(state_at, GROUPS)
    row = v * SUBLANES + s
    return h, h[row:row + 1, :]


def _even_kernel(*refs, state_at):
    (x_ref, zin_ref, xbin_ref, hin_ref, g_ref, win_ref, ca_ref, cb_ref, cbb_ref, rw_ref, rb_ref,
     iw_ref, ib_ref, lam_ref, wout_ref, o_ref) = refs[:16]
    ztail, xbtail, hstate = refs[-3:]
    t = pl.program_id(1)

    @pl.when(t == 0)
    def _():
        ztail[...] = zin_ref[...]
        xbtail[...] = xbin_ref[...]
        hstate[...] = hin_ref[...]

    hn = _rms(x_ref[0], g_ref[...]).astype(BF16)
    u = _dot(hn, win_ref[0])
    cw = CONV_WIDTH
    gb = u[:, 0:cw]
    gc = u[:, cw:2 * cw]
    xa = u[:, 2 * cw:3 * cw]
    xb = u[:, 3 * cw:3 * cw + LRU_WIDTH]
    gate = u[:, 3 * cw + LRU_WIDTH:]

    conv_z, ztail[...] = _causal_conv(gc * xa, ztail[...], ca_ref[...], state_at)
    y_a = gb * conv_z

    conv_xb, xbtail[...] = _causal_conv(xb, xbtail[...], cb_ref[...], state_at)
    xc = conv_xb + cbb_ref[...]

    xcb = xc.astype(BF16)
    r = _sigmoid(_dot(xcb, rw_ref[...]) + rb_ref[...])
    i = _sigmoid(_dot(xcb, iw_ref[...]) + ib_ref[...])
    nlam = -lam_ref[...]
    softplus = jnp.maximum(nlam, 0.0) + jnp.log1p(jnp.exp(-jnp.abs(nlam)))
    log_a = -LRU_C * r * softplus
    a = jnp.exp(log_a)
    th = jnp.tanh(log_a)
    mult = jnp.sqrt(-2.0 * th / (1.0 - th))
    h, hstate[...] = _lru_scan(a, mult * (i * xc), hstate[...], state_at)

    y_b = _gelu_tanh(gate) * h
    y = jnp.concatenate([y_a, y_b], axis=-1).astype(BF16)
    o_ref[0] = x_ref[0] + _dot(y, wout_ref[0])
    if state_at is not None:
        zout_ref, xbout_ref, hout_ref = refs[16:19]
        zout_ref[...] = ztail[...]
        xbout_ref[...] = xbtail[...]
        hout_ref[...] = hstate[...]


def _const_spec(shape):
    nd = len(shape)
    return pl.BlockSpec(shape, lambda b, t: (0,) * nd, pipeline_mode=pl.Buffered(1))


class _Stacked(NamedTuple):
    array: jax.Array
    layer: int


def _weight_spec(a):
    if not isinstance(a, _Stacked):
        return _const_spec(a.shape)
    shape = (1,) + a.array.shape[1:]
    rest = (0,) * (len(shape) - 1)
    return pl.BlockSpec(shape, lambda b, t: (a.layer,) + rest, pipeline_mode=pl.Buffered(1))


def _operand(a):
    return a.array if isinstance(a, _Stacked) else a


def _shapes(arrays):
    return tuple(jax.ShapeDtypeStruct(a.shape, a.dtype) for a in arrays)


def _even_layer(h, state, g, w_in, conv_a, conv_b, conv_b_bias, rw, rb, iw, ib, lam, w_out,
                state_at=None):
    bsz, tlen, _ = h.shape
    tile = min(STEP_T, tlen)
    row_spec = pl.BlockSpec((1, tile, D_MODEL), lambda b, t: (b, t, 0))
    args = tuple(state) + (
        g.reshape(1, D_MODEL), w_in, conv_a, conv_b,
        conv_b_bias.reshape(1, LRU_WIDTH), rw, rb.reshape(1, LRU_WIDTH), iw,
        ib.reshape(1, LRU_WIDTH), lam.reshape(1, LRU_WIDTH), w_out)
    out_shape = [jax.ShapeDtypeStruct(h.shape, F32)]
    out_specs = [row_spec]
    if state_at is not None:
        assert bsz == 1 and tlen == tile
        out_shape += list(_shapes(state))
        out_specs += [_const_spec(s.shape) for s in state]
    res = pl.pallas_call(
        functools.partial(_even_kernel, state_at=state_at),
        grid=(bsz, tlen // tile),
        in_specs=[row_spec] + [_weight_spec(a) for a in args],
        out_specs=out_specs,
        out_shape=out_shape,
        scratch_shapes=[pltpu.VMEM(s.shape, F32) for s in state],
        compiler_params=pltpu.CompilerParams(
            dimension_semantics=("arbitrary", "arbitrary"), vmem_limit_bytes=VMEM_LIMIT),
        name="even_mixer",
    )(h, *map(_operand, args))
    return res[0], (tuple(res[1:]) if state_at is not None else None)


def _block_diag(w):
    nh, d, _ = w.shape
    eye = jnp.eye(nh, dtype=w.dtype)
    return jnp.einsum("hij,hg->higj", w, eye).reshape(nh * d, nh * d)


def _ffn_kernel(*refs, attn_input, final_norm, state_at):
    refs = list(refs)
    x_ref, tin_ref = refs.pop(0), refs.pop(0)
    attn_ref, wo_ref = (refs.pop(0), refs.pop(0)) if attn_input else (None, None)
    g_ref, wup_ref, cw_ref, cb_ref, wdn_ref = refs[:5]
    del refs[:5]
    fg_ref = refs.pop(0) if final_norm else None
    o_ref = refs.pop(0)
    tout_ref = refs.pop(0) if state_at is not None else None
    slab_ref = refs.pop() if final_norm else None
    hn_ref, act_ref, tail = refs
    t = pl.program_id(1)

    @pl.when(t == 0)
    def _():
        tail[...] = tin_ref[...]

    if attn_input:
        wg = attn_ref.shape[3]
        y = x_ref[0]
        for grp in range(attn_ref.shape[1]):
            y = y + _dot(attn_ref[0, grp], wo_ref[grp * wg:(grp + 1) * wg, :])
        o_ref[0] = y
    else:
        o_ref[0] = x_ref[0]
    hn_ref[...] = _rms(o_ref[0], g_ref[...]).astype(BF16)
    for c in range(D_FF // FF_CHUNK):
        halves = []
        for part in range(2):
            col = part * D_FF + c * FF_CHUNK
            cs = slice(col, col + FF_CHUNK)
            u = _dot(hn_ref[...], wup_ref[0, :, cs])
            conv_u, tail[:, cs] = _causal_conv(u, tail[:, cs], cw_ref[:, cs], state_at)
            halves.append(conv_u + cb_ref[:, cs])
        a, gte = halves
        act_ref[:, c * FF_CHUNK:(c + 1) * FF_CHUNK] = (a * _sigmoid(a) * gte).astype(BF16)
    y = o_ref[0] + _dot(act_ref[...], wdn_ref[0])
    if final_norm:
        y = _rms(y, fg_ref[...])
        slabs = range(D_MODEL // LANES)
        for c in slabs:
            slab_ref[c] = y[:, c * LANES:(c + 1) * LANES]
        for n in range(y.shape[0] // TILE_T):
            for s in range(SUBLANES):
                src = pl.ds(n * TILE_T + s, GROUPS, stride=SUBLANES)
                dst = pl.ds(n * TILE_T + s * GROUPS, GROUPS)
                for c in slabs:
                    o_ref[0, dst, c * LANES:(c + 1) * LANES] = slab_ref[c, src, :]
    else:
        o_ref[0] = y
    if state_at is not None:
        tout_ref[...] = tail[...]


def _ffn_layer(h, tail, g, w_up, conv_w, conv_b, w_down, attn=None, final_g=None, state_at=None):
    bsz, tlen, _ = h.shape
    tile = min(STEP_T, tlen)
    row_spec = pl.BlockSpec((1, tile, D_MODEL), lambda b, t: (b, t, 0))
    args = (g.reshape(1, D_MODEL), w_up, conv_w, conv_b.reshape(1, 2 * D_FF), w_down)
    if final_g is not None:
        args = args + (final_g.reshape(1, D_MODEL),)
    specs = [_weight_spec(a) for a in args]
    if attn is not None:
        args = attn + args
        attn_spec = pl.BlockSpec((1, attn[0].shape[1], tile, attn[0].shape[3]),
                                 lambda b, t: (b, 0, t, 0))
        specs = [attn_spec, _const_spec(attn[1].shape)] + specs
    out_shape = [jax.ShapeDtypeStruct(h.shape, F32)]
    out_specs = [row_spec]
    if state_at is not None:
        assert bsz == 1 and tlen == tile
        out_shape.append(jax.ShapeDtypeStruct(tail.shape, F32))
        out_specs.append(_const_spec(tail.shape))
    res = pl.pallas_call(
        functools.partial(_ffn_kernel, attn_input=attn is not None,
                          final_norm=final_g is not None, state_at=state_at),
        grid=(bsz, tlen // tile),
        in_specs=[row_spec, _const_spec(tail.shape)] + specs,
        out_specs=out_specs,
        out_shape=out_shape,
        scratch_shapes=[
            pltpu.VMEM((tile, D_MODEL), BF16),
            pltpu.VMEM((tile, D_FF), BF16),
            pltpu.VMEM(tail.shape, F32),
        ] + ([pltpu.VMEM((D_MODEL // LANES, tile, LANES), F32)] if final_g is not None else []),
        compiler_params=pltpu.CompilerParams(
            dimension_semantics=("arbitrary", "arbitrary"), vmem_limit_bytes=VMEM_LIMIT),
        name="conv_ffn",
    )(h, tail, *map(_operand, args))
    return res[0], (res[1] if state_at is not None else None)


def _mla_proj_kernel(x_ref, g_ref, win_ref, qn_ref, kvn_ref, wqt_ref, wk_ref, wvt_ref,
                     cosk_ref, sink_ref, cosq_ref, sinq_ref, qt_ref, k_ref, vt_ref):
    x = x_ref[0]
    hn = _rms(x, g_ref[...]).astype(BF16)
    u = _dot(hn, win_ref[...])
    cq = u[:, 0:Q_LORA]
    ckv = u[:, Q_LORA:Q_LORA + KV_LORA]
    kr = u[:, Q_LORA + KV_LORA:Q_LORA + KV_LORA + HEAD_PAD]
    kr_rot = u[:, Q_LORA + KV_LORA + HEAD_PAD:]
    cqn = _rms(cq, qn_ref[...]).astype(BF16)
    ckvn = _rms(ckv, kvn_ref[...]).astype(BF16)

    qt = _dot_nt(wqt_ref[...], cqn)
    kn = _dot(ckvn, wk_ref[...])
    vt = _dot_nt(wvt_ref[...], ckvn)
    k_rope = kr * cosk_ref[...] + kr_rot * sink_ref[...]
    cosq = cosq_ref[...]
    sinq = sinq_ref[...]
    scale = QK_HEAD ** -0.5 * math.log2(math.e)
    zeros = jnp.zeros((HEAD_PAD - QK_HEAD, qt.shape[1]), F32)
    ones = jnp.ones((V_ROWS - V_HEAD, qt.shape[1]), F32)
    for h in range(MLA_HEADS):
        qh = qt[h * HEAD_PAD:(h + 1) * HEAD_PAD, :]
        roped = qh[QK_NOPE:QK_HEAD, :] * cosq + qh[QK_HEAD:, :] * sinq
        q_out = jnp.concatenate(
            [qh[0:QK_NOPE, :] * scale, roped * scale, zeros], axis=0).astype(BF16)
        k_out = (kn[:, h * HEAD_PAD:(h + 1) * HEAD_PAD] + k_rope).astype(BF16)
        v_out = jnp.concatenate(
            [vt[h * V_HEAD:(h + 1) * V_HEAD, :], ones], axis=0).astype(BF16)
        for i in range(qt_ref.shape[2]):
            rows = slice(i * TILE_T, (i + 1) * TILE_T)
            qt_ref[0, h, i] = q_out[:, rows]
            k_ref[0, h, i] = k_out[rows, :]
            vt_ref[0, h, i] = v_out[:, rows]


def _rot_cols(w):
    half = QK_ROPE // 2
    return jnp.concatenate([-w[..., half:], w[..., :half]], axis=-1)


def _mla_weights(w_in, w_uq, w_ukv):
    w_kr = w_in[:, Q_LORA + KV_LORA:]
    padl = jnp.zeros((D_MODEL, QK_NOPE), F32)
    padr = jnp.zeros((D_MODEL, HEAD_PAD - QK_HEAD), F32)
    win_ext = jnp.concatenate(
        [w_in[:, :Q_LORA + KV_LORA], padl, w_kr, padr, padl, _rot_cols(w_kr), padr],
        axis=-1).astype(BF16)
    wq = w_uq.reshape(Q_LORA, MLA_HEADS, QK_HEAD)
    wq_ext = jnp.concatenate([wq, _rot_cols(wq[..., QK_NOPE:])], axis=-1)
    wqt = wq_ext.reshape(Q_LORA, MLA_HEADS * HEAD_PAD).T.astype(BF16)
    wkv = w_ukv.reshape(KV_LORA, MLA_HEADS, QK_NOPE + V_HEAD)
    wk = jnp.concatenate(
        [wkv[..., :QK_NOPE], jnp.zeros((KV_LORA, MLA_HEADS, HEAD_PAD - QK_NOPE), F32)],
        axis=-1).reshape(KV_LORA, MLA_HEADS * HEAD_PAD).astype(BF16)
    wvt = wkv[..., QK_NOPE:].reshape(KV_LORA, MLA_HEADS * V_HEAD).T.astype(BF16)
    return win_ext, wqt, wk, wvt


def _rope_tables(first_pos, t_len):
    pos = first_pos + jnp.arange(t_len, dtype=F32)
    inv_freq = ROPE_BASE ** (-jnp.arange(0, QK_ROPE, 2, dtype=F32) / QK_ROPE)
    ang = _to_strided(pos[:, None] * inv_freq[None, :], 0)
    cos2 = jnp.concatenate([jnp.cos(ang)] * 2, axis=-1)
    sin2 = jnp.concatenate([jnp.sin(ang)] * 2, axis=-1)
    lpad = ((0, 0), (QK_NOPE, HEAD_PAD - QK_HEAD))
    return jnp.pad(cos2, lpad), jnp.pad(sin2, lpad), cos2.T, sin2.T


def _mla_proj(h, tables, g, q_norm, kv_norm, weights):
    bsz, tlen, _ = h.shape
    tile = TILE_T
    nt = tlen // tile
    step = min(STEP_T, tlen)
    per = step // tile
    win_ext, wqt, wk, wvt = weights
    args = (g.reshape(1, D_MODEL), win_ext, q_norm.reshape(1, Q_LORA),
            kv_norm.reshape(1, KV_LORA), wqt, wk, wvt)
    row_spec = pl.BlockSpec((1, step, D_MODEL), lambda b, t: (b, t, 0))
    in_specs = [row_spec] + [_const_spec(a.shape) for a in args] + [
        pl.BlockSpec((step, HEAD_PAD), lambda b, t: (t, 0)),
        pl.BlockSpec((step, HEAD_PAD), lambda b, t: (t, 0)),
        pl.BlockSpec((QK_ROPE, step), lambda b, t: (0, t)),
        pl.BlockSpec((QK_ROPE, step), lambda b, t: (0, t)),
    ]
    out_shape = (
        jax.ShapeDtypeStruct((bsz, MLA_HEADS, nt, HEAD_PAD, tile), BF16),
        jax.ShapeDtypeStruct((bsz, MLA_HEADS, nt, tile, HEAD_PAD), BF16),
        jax.ShapeDtypeStruct((bsz, MLA_HEADS, nt, V_ROWS, tile), BF16),
    )
    out_specs = (
        pl.BlockSpec((1, MLA_HEADS, per, HEAD_PAD, tile), lambda b, t: (b, 0, t, 0, 0)),
        pl.BlockSpec((1, MLA_HEADS, per, tile, HEAD_PAD), lambda b, t: (b, 0, t, 0, 0)),
        pl.BlockSpec((1, MLA_HEADS, per, V_ROWS, tile), lambda b, t: (b, 0, t, 0, 0)),
    )
    return pl.pallas_call(
        _mla_proj_kernel,
        grid=(bsz, tlen // step),
        in_specs=in_specs,
        out_specs=out_specs,
        out_shape=out_shape,
        compiler_params=pltpu.CompilerParams(
            dimension_semantics=("arbitrary", "arbitrary"), vmem_limit_bytes=VMEM_LIMIT),
        name="mla_proj",
    )(h, *args, *tables)


def _strided_time(i):
    return (i & (SUBLANES - 1)) * GROUPS + (i >> (SUBLANES.bit_length() - 1))


def _attn_kernel(*refs, has_meta):
    if has_meta:
        qt_ref, k_ref, vt_ref, km_ref, vtm_ref, o_ref, s_ref, smax_ref = refs
    else:
        qt_ref, k_ref, vt_ref, o_ref, s_ref, smax_ref = refs
    tile = qt_ref.shape[4]
    step = pl.program_id(2)
    neg = jnp.finfo(F32).min
    heads = range(HEADS_PER_STEP)

    def softmax_update(state, scores, vt, keep=None, block_max=None, extra=None):
        def block():
            s = scores()
            return s if keep is None else jnp.where(keep, s, neg)
        m, acc = state
        if block_max is None:
            block_max = jnp.max(block(), axis=0, keepdims=True)
        if extra is not None:
            block_max = jnp.maximum(block_max, jnp.max(extra[0], axis=0, keepdims=True))
        m_new = jnp.maximum(m, block_max)
        p = jnp.exp2(block() - m_new).astype(BF16)
        if extra is not None:
            p = jnp.concatenate([p, jnp.exp2(extra[0] - m_new).astype(BF16)], axis=0)
            vt = jnp.concatenate([vt, extra[1]], axis=1)
        return m_new, jnp.exp2(m - m_new) * acc + _dot(vt, p)

    def produce(sub, slot, j, hh):
        s = _dot(k_ref[0, hh, j], qt_ref[0, hh, sub])
        s_ref[slot, hh] = s
        smax_ref[slot, hh] = jnp.max(s, axis=0, keepdims=True)

    def consume(sub, carry, slot, j, next_j, keep=None, with_meta=False):
        if with_meta:
            meta = [(_dot(km_ref[hh], qt_ref[0, hh, sub]), vtm_ref[hh]) for hh in heads]
        if next_j is not None:
            for hh in range(SCORE_LEAD):
                produce(sub, 1 - slot, next_j, hh)
        out = []
        for hh in heads:
            out.append(softmax_update(
                carry[hh], lambda: s_ref[slot, hh], vt_ref[0, hh, j], keep,
                smax_ref[slot, hh] if keep is None else None,
                meta[hh] if with_meta else None))
            if next_j is not None and hh + SCORE_LEAD < HEADS_PER_STEP:
                produce(sub, 1 - slot, next_j, hh + SCORE_LEAD)
        return tuple(out)

    kpos = _strided_time(lax.broadcasted_iota(jnp.int32, (tile, tile), 0))
    qpos = _strided_time(lax.broadcasted_iota(jnp.int32, (tile, tile), 1))
    keep = kpos <= qpos
    subs = qt_ref.shape[2]
    for sub in range(subs):
        qi = subs * step + sub
        carry = tuple((jnp.full((1, tile), neg, F32), jnp.zeros((V_ROWS, tile), F32))
                      for _ in heads)
        for hh in heads:
            produce(sub, 0, 0, hh)

        def group(first, count, c, sub=sub):
            for d in range(count):
                c = consume(sub, c, d % 2, first + d, first + d + 1)
            return c

        per = LOOP_TILES // subs
        carry = lax.fori_loop(0, step // per,
                              lambda jj, c: group(LOOP_TILES * jj, LOOP_TILES, c), carry)
        carry = lax.fori_loop(per * (step // per), step,
                              lambda jj, c: group(subs * jj, subs, c), carry)
        for d in range(sub):
            carry = consume(sub, carry, d % 2, subs * step + d, subs * step + d + 1)
        carry = consume(sub, carry, sub % 2, qi, None, keep, has_meta)
        outs = [acc[0:V_HEAD, :] / acc[V_HEAD:V_HEAD + 1, :] for _, acc in carry]
        o_ref[0, 0, sub * tile:(sub + 1) * tile, :] = (
            jnp.concatenate(outs, axis=0).T.astype(o_ref.dtype))


def _attention(qt, k5, vt5, meta_kv=None):
    bsz, nh, nt, tile, _ = k5.shape
    tlen = nt * tile
    hps = HEADS_PER_STEP
    subs = ATTN_Q_TILES if nt % ATTN_Q_TILES == 0 else 1
    assert subs % 2 == 0 or nt == 1
    in_specs = [
        pl.BlockSpec((1, hps, subs, HEAD_PAD, tile), lambda b, h, q: (b, h, q, 0, 0)),
        pl.BlockSpec((1, hps, nt, tile, HEAD_PAD), lambda b, h, q: (b, h, 0, 0, 0)),
        pl.BlockSpec((1, hps, nt, V_ROWS, tile), lambda b, h, q: (b, h, 0, 0, 0)),
    ]
    args = (qt, k5, vt5)
    if meta_kv is not None:
        in_specs += [pl.BlockSpec((hps,) + a.shape[1:], lambda b, h, q: (h, 0, 0))
                     for a in meta_kv]
        args += tuple(meta_kv)
    return pl.pallas_call(
        functools.partial(_attn_kernel, has_meta=meta_kv is not None),
        grid=(bsz, nh // hps, nt // subs),
        in_specs=in_specs,
        out_specs=pl.BlockSpec((1, 1, subs * tile, hps * V_HEAD), lambda b, h, q: (b, h, q, 0)),
        out_shape=jax.ShapeDtypeStruct((bsz, nh // hps, tlen, hps * V_HEAD), BF16),
        scratch_shapes=[pltpu.VMEM((2, hps, tile, tile), F32),
                        pltpu.VMEM((2, hps, 1, tile), F32)],
        compiler_params=pltpu.CompilerParams(
            dimension_semantics=("arbitrary", "arbitrary", "arbitrary"),
            vmem_limit_bytes=VMEM_LIMIT),
        name="mla_attention",
    )(*args)


def _to_strided(a, axis):
    shp = a.shape
    nt = shp[axis] // TILE_T
    a = a.reshape(shp[:axis] + (nt, SUBLANES, GROUPS) + shp[axis + 1:])
    return jnp.swapaxes(a, axis + 1, axis + 2).reshape(shp)


def _trunk(h, first_pos, states, p, state_at):
    depth = len(p["ffn"])
    tables = _rope_tables(first_pos, h.shape[1])
    new_states = []
    for layer in range(depth):
        st = states[layer]
        ffn = p["ffn"][layer]
        attn = None
        if layer % 2 == 0:
            h, mix_state = _even_layer(h, st["mix"], *p["even"][layer // 2], state_at=state_at)
        else:
            g, q_norm, kv_norm, weights, w_out = p["odd"][layer // 2]
            qt, k5, vt5 = _mla_proj(h, tables, g, q_norm, kv_norm, weights)
            attn = (_attention(qt, k5, vt5, st["mix"]), w_out)
            mix_state = None
            if state_at is not None:
                n = state_at + 1
                assert n <= GROUPS
                mix_state = (k5[0, :, 0, 0:n * SUBLANES:SUBLANES, :],
                             vt5[0, :, 0, :, 0:n * SUBLANES:SUBLANES])
        h, tail = _ffn_layer(h, st["ffn"], *ffn, attn=attn,
                             final_g=p["final"] if layer == depth - 1 else None,
                             state_at=state_at)
        new_states.append({"mix": mix_state, "ffn": tail})
    return h, new_states


def kernel(x, meta_tokens, ev_norm, ev_w_in, ev_conv_a, ev_conv_b, ev_conv_b_bias, ev_gate_r_w, ev_gate_r_b, ev_gate_i_w, ev_gate_i_b, ev_lru_lambda, ev_w_out, od_norm, od_w_in, od_q_norm, od_kv_norm, od_w_uq, od_w_ukv, od_w_out, ffn_norm, ffn_w_up, ffn_conv_w, ffn_conv_b, ffn_w_down, final_norm):
    bsz, seq, _ = x.shape
    depth = ffn_norm.shape[0]
    assert seq % TILE_T == 0 and N_META <= GROUPS
    ev_in, ev_out = ev_w_in.astype(BF16), ev_w_out.astype(BF16)
    w_up, w_down = ffn_w_up.astype(BF16), ffn_w_down.astype(BF16)
    params = {
        "even": [(ev_norm[j], _Stacked(ev_in, j), ev_conv_a[j], ev_conv_b[j],
                  ev_conv_b_bias[j], _block_diag(ev_gate_r_w[j]).astype(BF16), ev_gate_r_b[j],
                  _block_diag(ev_gate_i_w[j]).astype(BF16), ev_gate_i_b[j], ev_lru_lambda[j],
                  _Stacked(ev_out, j)) for j in range(ev_norm.shape[0])],
        "odd": [(od_norm[j], od_q_norm[j], od_kv_norm[j],
                 _mla_weights(od_w_in[j], od_w_uq[j], od_w_ukv[j]), od_w_out[j].astype(BF16))
                for j in range(od_norm.shape[0])],
        "ffn": [(ffn_norm[l], _Stacked(w_up, l), ffn_conv_w[l], ffn_conv_b[l],
                 _Stacked(w_down, l)) for l in range(depth)],
        "final": final_norm,
    }
    zero_states = []
    for layer in range(depth):
        mix = None
        if layer % 2 == 0:
            mix = (jnp.zeros(((ev_conv_a.shape[1] - 1) * SUBLANES, CONV_WIDTH), F32),
                   jnp.zeros(((ev_conv_b.shape[1] - 1) * SUBLANES, LRU_WIDTH), F32),
                   jnp.zeros((1, LRU_WIDTH), F32))
        zero_states.append(
            {"mix": mix, "ffn": jnp.zeros(((ffn_conv_w.shape[1] - 1) * SUBLANES, 2 * D_FF), F32)})

    meta_tile = jnp.concatenate(
        [meta_tokens.astype(x.dtype), jnp.zeros((TILE_T - N_META, D_MODEL), x.dtype)], axis=0)
    _, meta_states = _trunk(_to_strided(meta_tile[None], 1), 0.0, zero_states, params,
                            state_at=N_META - 1)
    h, _ = _trunk(_to_strided(x, 1), float(N_META), meta_states, params, state_at=None)
    return h
```

```python
import functools
import math
from typing import NamedTuple

import jax
import jax.numpy as jnp
from jax import lax
from jax.experimental import pallas as pl
from jax.experimental.pallas import tpu as pltpu

D_MODEL = 1024
N_META = 16
EPS = 1e-6
CONV_WIDTH = 512
LRU_WIDTH = 512
LRU_C = 8.0
MLA_HEADS = 16
QK_NOPE = 64
QK_ROPE = 32
QK_HEAD = QK_NOPE + QK_ROPE
V_HEAD = 64
Q_LORA = 384
KV_LORA = 256
ROPE_BASE = 10000.0
D_FF = 2816

LANES = 128
SUBLANES = 8
HEAD_PAD = 128
V_ROWS = V_HEAD + 16
HEADS_PER_STEP = 8
ATTN_Q_TILES = 4
SCORE_LEAD = 1
TILE_T = 256
GROUPS = TILE_T // SUBLANES
STEP_T = 1024
FF_CHUNK = 256
VMEM_LIMIT = 56 * 1024 * 1024

F32 = jnp.float32
BF16 = jnp.bfloat16


def _rms(x, g):
    ms = jnp.mean(x * x, axis=-1, keepdims=True)
    return x * lax.rsqrt(ms + EPS) * g


def _sigmoid(x):
    return 1.0 / (1.0 + jnp.exp(-x))


def _gelu_tanh(x):
    c = math.sqrt(2.0 / math.pi)
    return x * (0.5 * (1.0 + jnp.tanh(c * (x + 0.044715 * (x * x * x)))))


def _dot(a, b):
    return jnp.dot(a, b, preferred_element_type=F32)


def _dot_nt(a, b):
    return lax.dot_general(a, b, (((1,), (1,)), ((), ())), preferred_element_type=F32)


def _row_groups(x):
    return [x[v * SUBLANES:(v + 1) * SUBLANES, :] for v in range(x.shape[0] // SUBLANES)]


def _delays(u, prev, kmax):
    tile = u.shape[0]
    first = lax.broadcasted_iota(jnp.int32, (SUBLANES, u.shape[1]), 0) == 0
    wrapped = []
    for i in range(kmax):
        cur = u[tile - (kmax - i) * SUBLANES:tile - (kmax - i - 1) * SUBLANES, :]
        old = prev[i * SUBLANES:(i + 1) * SUBLANES, :]
        wrapped.append(jnp.where(first, pltpu.roll(old, 1, axis=0), pltpu.roll(cur, 1, axis=0)))
    return [jnp.concatenate(wrapped[kmax - k:] + [u[:tile - k * SUBLANES, :]], axis=0)
            for k in range(1, kmax + 1)]


def _tiles(x):
    return [x[i * TILE_T:(i + 1) * TILE_T, :] for i in range(x.shape[0] // TILE_T)]


def _stack(parts):
    return parts[0] if len(parts) == 1 else jnp.concatenate(parts, axis=0)


def _causal_conv(u, prev, taps, state_at=None):
    k = taps.shape[0]
    outs = []
    for ui in _tiles(u):
        delayed = _delays(ui, prev, k - 1)
        acc = delayed[k - 2] * taps[0:1, :]
        for j in range(1, k - 1):
            acc = acc + delayed[k - 2 - j] * taps[j:j + 1, :]
        outs.append(acc + ui * taps[k - 1:k, :])
        prev = _conv_tail(ui, k - 1, state_at)
    return _stack(outs), prev


def _conv_tail(u, kmax, state_at):
    tile = u.shape[0]
    if state_at is None:
        return u[tile - kmax * SUBLANES:, :]
    s, v = divmod(state_at, GROUPS)
    assert v >= kmax - 1
    rows = []
    for i in range(kmax):
        grp = u[(v - (kmax - 1 - i)) * SUBLANES:(v - (kmax - 2 - i)) * SUBLANES, :]
        rows.append(pltpu.roll(grp, SUBLANES - 1 - s, axis=0) if s != SUBLANES - 1 else grp)
    return jnp.concatenate(rows, axis=0)


def _lru_scan(a, b, carry, state_at=None):
    outs = []
    for ai, bi in zip(_tiles(a), _tiles(b)):
        hi, carry = _lru_scan_tile(ai, bi, carry, state_at)
        outs.append(hi)
    return _stack(outs), carry


def _lru_scan_tile(a, b, carry, state_at):
    a_rows, b_rows = _row_groups(a), _row_groups(b)
    prod, hzero = [a_rows[0]], [b_rows[0]]
    for v in range(1, len(a_rows)):
        prod.append(a_rows[v] * prod[-1])
        hzero.append(a_rows[v] * hzero[-1] + b_rows[v])
    pa, ph = prod[-1], hzero[-1]
    sub = lax.broadcasted_iota(jnp.int32, pa.shape, 0)
    for d in (1, 2, 4):
        ok = sub >= d
        ph = jnp.where(ok, pa * pltpu.roll(ph, d, axis=0) + ph, ph)
        pa = jnp.where(ok, pa * pltpu.roll(pa, d, axis=0), pa)
    ends = ph + pa * carry
    init = jnp.where(sub == 0, carry, pltpu.roll(ends, 1, axis=0))
    h = jnp.concatenate([hz + pr * init for hz, pr in zip(hzero, prod)], axis=0)
    if state_at is None:
        return h, ends[SUBLANES - 1:SUBLANES, :]
    s, v = divmod(state_at, GROUPS)
    row = v * SUBLANES + s
    return h, h[row:row + 1, :]


def _even_kernel(*refs, state_at):
    (x_ref, zin_ref, xbin_ref, hin_ref, g_ref, win_ref, ca_ref, cb_ref, cbb_ref, rw_ref, rb_ref,
     iw_ref, ib_ref, lam_ref, wout_ref, o_ref) = refs[:16]
    ztail, xbtail, hstate = refs[-3:]
    t = pl.program_id(1)

    @pl.when(t == 0)
    def _():
        ztail[...] = zin_ref[...]
        xbtail[...] = xbin_ref[...]
        hstate[...] = hin_ref[...]

    hn = _rms(x_ref[0], g_ref[...]).astype(BF16)
    u = _dot(hn, win_ref[0])
    cw = CONV_WIDTH
    gb = u[:, 0:cw]
    gc = u[:, cw:2 * cw]
    xa = u[:, 2 * cw:3 * cw]
    xb = u[:, 3 * cw:3 * cw + LRU_WIDTH]
    gate = u[:, 3 * cw + LRU_WIDTH:]

    conv_z, ztail[...] = _causal_conv(gc * xa, ztail[...], ca_ref[...], state_at)
    y_a = gb * conv_z

    conv_xb, xbtail[...] = _causal_conv(xb, xbtail[...], cb_ref[...], state_at)
    xc = conv_xb + cbb_ref[...]

    xcb = xc.astype(BF16)
    r = _sigmoid(_dot(xcb, rw_ref[...]) + rb_ref[...])
    i = _sigmoid(_dot(xcb, iw_ref[...]) + ib_ref[...])
    nlam = -lam_ref[...]
    softplus = jnp.maximum(nlam, 0.0) + jnp.log1p(jnp.exp(-jnp.abs(nlam)))
    log_a = -LRU_C * r * softplus
    a = jnp.exp(log_a)
    th = jnp.tanh(log_a)
    mult = jnp.sqrt(-2.0 * th / (1.0 - th))
    h, hstate[...] = _lru_scan(a, mult * (i * xc), hstate[...], state_at)

    y_b = _gelu_tanh(gate) * h
    y = jnp.concatenate([y_a, y_b], axis=-1).astype(BF16)
    o_ref[0] = x_ref[0] + _dot(y, wout_ref[0])
    if state_at is not None:
        zout_ref, xbout_ref, hout_ref = refs[16:19]
        zout_ref[...] = ztail[...]
        xbout_ref[...] = xbtail[...]
        hout_ref[...] = hstate[...]


def _const_spec(shape):
    nd = len(shape)
    return pl.BlockSpec(shape, lambda b, t: (0,) * nd, pipeline_mode=pl.Buffered(1))


class _Stacked(NamedTuple):
    array: jax.Array
    layer: int


def _weight_spec(a):
    if not isinstance(a, _Stacked):
        return _const_spec(a.shape)
    shape = (1,) + a.array.shape[1:]
    rest = (0,) * (len(shape) - 1)
    return pl.BlockSpec(shape, lambda b, t: (a.layer,) + rest, pipeline_mode=pl.Buffered(1))


def _operand(a):
    return a.array if isinstance(a, _Stacked) else a


def _shapes(arrays):
    return tuple(jax.ShapeDtypeStruct(a.shape, a.dtype) for a in arrays)


def _even_layer(h, state, g, w_in, conv_a, conv_b, conv_b_bias, rw, rb, iw, ib, lam, w_out,
                state_at=None):
    bsz, tlen, _ = h.shape
    tile = min(STEP_T, tlen)
    row_spec = pl.BlockSpec((1, tile, D_MODEL), lambda b, t: (b, t, 0))
    args = tuple(state) + (
        g.reshape(1, D_MODEL), w_in, conv_a, conv_b,
        conv_b_bias.reshape(1, LRU_WIDTH), rw, rb.reshape(1, LRU_WIDTH), iw,
        ib.reshape(1, LRU_WIDTH), lam.reshape(1, LRU_WIDTH), w_out)
    out_shape = [jax.ShapeDtypeStruct(h.shape, F32)]
    out_specs = [row_spec]
    if state_at is not None:
        assert bsz == 1 and tlen == tile
        out_shape += list(_shapes(state))
        out_specs += [_const_spec(s.shape) for s in state]
    res = pl.pallas_call(
        functools.partial(_even_kernel, state_at=state_at),
        grid=(bsz, tlen // tile),
        in_specs=[row_spec] + [_weight_spec(a) for a in args],
        out_specs=out_specs,
        out_shape=out_shape,
        scratch_shapes=[pltpu.VMEM(s.shape, F32) for s in state],
        compiler_params=pltpu.CompilerParams(
            dimension_semantics=("arbitrary", "arbitrary"), vmem_limit_bytes=VMEM_LIMIT),
        name="even_mixer",
    )(h, *map(_operand, args))
    return res[0], (tuple(res[1:]) if state_at is not None else None)


def _block_diag(w):
    nh, d, _ = w.shape
    eye = jnp.eye(nh, dtype=w.dtype)
    return jnp.einsum("hij,hg->higj", w, eye).reshape(nh * d, nh * d)


def _ffn_kernel(*refs, attn_input, final_norm, state_at):
    refs = list(refs)
    x_ref, tin_ref = refs.pop(0), refs.pop(0)
    attn_ref, wo_ref = (refs.pop(0), refs.pop(0)) if attn_input else (None, None)
    g_ref, wup_ref, cw_ref, cb_ref, wdn_ref = refs[:5]
    del refs[:5]
    fg_ref = refs.pop(0) if final_norm else None
    o_ref = refs.pop(0)
    tout_ref = refs.pop(0) if state_at is not None else None
    slab_ref = refs.pop() if final_norm else None
    hn_ref, act_ref, tail = refs
    t = pl.program_id(1)

    @pl.when(t == 0)
    def _():
        tail[...] = tin_ref[...]

    if attn_input:
        wg = attn_ref.shape[3]
        y = x_ref[0]
        for grp in range(attn_ref.shape[1]):
            y = y + _dot(attn_ref[0, grp], wo_ref[grp * wg:(grp + 1) * wg, :])
        o_ref[0] = y
    else:
        o_ref[0] = x_ref[0]
    hn_ref[...] = _rms(o_ref[0], g_ref[...]).astype(BF16)
    for c in range(D_FF // FF_CHUNK):
        halves = []
        for part in range(2):
            col = part * D_FF + c * FF_CHUNK
            cs = slice(col, col + FF_CHUNK)
            u = _dot(hn_ref[...], wup_ref[0, :, cs])
            conv_u, tail[:, cs] = _causal_conv(u, tail[:, cs], cw_ref[:, cs], state_at)
            halves.append(conv_u + cb_ref[:, cs])
        a, gte = halves
        act_ref[:, c * FF_CHUNK:(c + 1) * FF_CHUNK] = (a * _sigmoid(a) * gte).astype(BF16)
    y = o_ref[0] + _dot(act_ref[...], wdn_ref[0])
    if final_norm:
        y = _rms(y, fg_ref[...])
        slabs = range(D_MODEL // LANES)
        for c in slabs:
            slab_ref[c] = y[:, c * LANES:(c + 1) * LANES]
        for n in range(y.shape[0] // TILE_T):
            for s in range(SUBLANES):
                src = pl.ds(n * TILE_T + s, GROUPS, stride=SUBLANES)
                dst = pl.ds(n * TILE_T + s * GROUPS, GROUPS)
                for c in slabs:
                    o_ref[0, dst, c * LANES:(c + 1) * LANES] = slab_ref[c, src, :]
    else:
        o_ref[0] = y
    if state_at is not None:
        tout_ref[...] = tail[...]


def _ffn_layer(h, tail, g, w_up, conv_w, conv_b, w_down, attn=None, final_g=None, state_at=None):
    bsz, tlen, _ = h.shape
    tile = min(STEP_T, tlen)
    row_spec = pl.BlockSpec((1, tile, D_MODEL), lambda b, t: (b, t, 0))
    args = (g.reshape(1, D_MODEL), w_up, conv_w, conv_b.reshape(1, 2 * D_FF), w_down)
    if final_g is not None:
        args = args + (final_g.reshape(1, D_MODEL),)
    specs = [_weight_spec(a) for a in args]
    if attn is not None:
        args = attn + args
        attn_spec = pl.BlockSpec((1, attn[0].shape[1], tile, attn[0].shape[3]),
                                 lambda b, t: (b, 0, t, 0))
        specs = [attn_spec, _const_spec(attn[1].shape)] + specs
    out_shape = [jax.ShapeDtypeStruct(h.shape, F32)]
    out_specs = [row_spec]
    if state_at is not None:
        assert bsz == 1 and tlen == tile
        out_shape.append(jax.ShapeDtypeStruct(tail.shape, F32))
        out_specs.append(_const_spec(tail.shape))
    res = pl.pallas_call(
        functools.partial(_ffn_kernel, attn_input=attn is not None,
                          final_norm=final_g is not None, state_at=state_at),
        grid=(bsz, tlen // tile),
        in_specs=[row_spec, _const_spec(tail.shape)] + specs,
        out_specs=out_specs,
        out_shape=out_shape,
        scratch_shapes=[
            pltpu.VMEM((tile, D_MODEL), BF16),
            pltpu.VMEM((tile, D_FF), BF16),
            pltpu.VMEM(tail.shape, F32),
        ] + ([pltpu.VMEM((D_MODEL // LANES, tile, LANES), F32)] if final_g is not None else []),
        compiler_params=pltpu.CompilerParams(
            dimension_semantics=("arbitrary", "arbitrary"), vmem_limit_bytes=VMEM_LIMIT),
        name="conv_ffn",
    )(h, tail, *map(_operand, args))
    return res[0], (res[1] if state_at is not None else None)


def _mla_proj_kernel(x_ref, g_ref, win_ref, qn_ref, kvn_ref, wqt_ref, wk_ref, wvt_ref,
                     cosk_ref, sink_ref, cosq_ref, sinq_ref, qt_ref, k_ref, vt_ref):
    x = x_ref[0]
    hn = _rms(x, g_ref[...]).astype(BF16)
    u = _dot(hn, win_ref[...])
    cq = u[:, 0:Q_LORA]
    ckv = u[:, Q_LORA:Q_LORA + KV_LORA]
    kr = u[:, Q_LORA + KV_LORA:Q_LORA + KV_LORA + HEAD_PAD]
    kr_rot = u[:, Q_LORA + KV_LORA + HEAD_PAD:]
    cqn = _rms(cq, qn_ref[...]).astype(BF16)
    ckvn = _rms(ckv, kvn_ref[...]).astype(BF16)

    qt = _dot_nt(wqt_ref[...], cqn)
    kn = _dot(ckvn, wk_ref[...])
    vt = _dot_nt(wvt_ref[...], ckvn)
    k_rope = kr * cosk_ref[...] + kr_rot * sink_ref[...]
    cosq = cosq_ref[...]
    sinq = sinq_ref[...]
    scale = QK_HEAD ** -0.5 * math.log2(math.e)
    zeros = jnp.zeros((HEAD_PAD - QK_HEAD, qt.shape[1]), F32)
    ones = jnp.ones((V_ROWS - V_HEAD, qt.shape[1]), F32)
    for h in range(MLA_HEADS):
        qh = qt[h * HEAD_PAD:(h + 1) * HEAD_PAD, :]
        roped = qh[QK_NOPE:QK_HEAD, :] * cosq + qh[QK_HEAD:, :] * sinq
        q_out = jnp.concatenate(
            [qh[0:QK_NOPE, :] * scale, roped * scale, zeros], axis=0).astype(BF16)
        k_out = (kn[:, h * HEAD_PAD:(h + 1) * HEAD_PAD] + k_rope).astype(BF16)
        v_out = jnp.concatenate(
            [vt[h * V_HEAD:(h + 1) * V_HEAD, :], ones], axis=0).astype(BF16)
        for i in range(qt_ref.shape[2]):
            rows = slice(i * TILE_T, (i + 1) * TILE_T)
            qt_ref[0, h, i] = q_out[:, rows]
            k_ref[0, h, i] = k_out[rows, :]
            vt_ref[0, h, i] = v_out[:, rows]


def _rot_cols(w):
    half = QK_ROPE // 2
    return jnp.concatenate([-w[..., half:], w[..., :half]], axis=-1)


def _mla_weights(w_in, w_uq, w_ukv):
    w_kr = w_in[:, Q_LORA + KV_LORA:]
    padl = jnp.zeros((D_MODEL, QK_NOPE), F32)
    padr = jnp.zeros((D_MODEL, HEAD_PAD - QK_HEAD), F32)
    win_ext = jnp.concatenate(
        [w_in[:, :Q_LORA + KV_LORA], padl, w_kr, padr, padl, _rot_cols(w_kr), padr],
        axis=-1).astype(BF16)
    wq = w_uq.reshape(Q_LORA, MLA_HEADS, QK_HEAD)
    wq_ext = jnp.concatenate([wq, _rot_cols(wq[..., QK_NOPE:])], axis=-1)
    wqt = wq_ext.reshape(Q_LORA, MLA_HEADS * HEAD_PAD).T.astype(BF16)
    wkv = w_ukv.reshape(KV_LORA, MLA_HEADS, QK_NOPE + V_HEAD)
    wk = jnp.concatenate(
        [wkv[..., :QK_NOPE], jnp.zeros((KV_LORA, MLA_HEADS, HEAD_PAD - QK_NOPE), F32)],
        axis=-1).reshape(KV_LORA, MLA_HEADS * HEAD_PAD).astype(BF16)
    wvt = wkv[..., QK_NOPE:].reshape(KV_LORA, MLA_HEADS * V_HEAD).T.astype(BF16)
    return win_ext, wqt, wk, wvt


def _rope_tables(first_pos, t_len):
    pos = first_pos + jnp.arange(t_len, dtype=F32)
    inv_freq = ROPE_BASE ** (-jnp.arange(0, QK_ROPE, 2, dtype=F32) / QK_ROPE)
    ang = _to_strided(pos[:, None] * inv_freq[None, :], 0)
    cos2 = jnp.concatenate([jnp.cos(ang)] * 2, axis=-1)
    sin2 = jnp.concatenate([jnp.sin(ang)] * 2, axis=-1)
    lpad = ((0, 0), (QK_NOPE, HEAD_PAD - QK_HEAD))
    return jnp.pad(cos2, lpad), jnp.pad(sin2, lpad), cos2.T, sin2.T


def _mla_proj(h, tables, g, q_norm, kv_norm, weights):
    bsz, tlen, _ = h.shape
    tile = TILE_T
    nt = tlen // tile
    step = min(STEP_T, tlen)
    per = step // tile
    win_ext, wqt, wk, wvt = weights
    args = (g.reshape(1, D_MODEL), win_ext, q_norm.reshape(1, Q_LORA),
            kv_norm.reshape(1, KV_LORA), wqt, wk, wvt)
    row_spec = pl.BlockSpec((1, step, D_MODEL), lambda b, t: (b, t, 0))
    in_specs = [row_spec] + [_const_spec(a.shape) for a in args] + [
        pl.BlockSpec((step, HEAD_PAD), lambda b, t: (t, 0)),
        pl.BlockSpec((step, HEAD_PAD), lambda b, t: (t, 0)),
        pl.BlockSpec((QK_ROPE, step), lambda b, t: (0, t)),
        pl.BlockSpec((QK_ROPE, step), lambda b, t: (0, t)),
    ]
    out_shape = (
        jax.ShapeDtypeStruct((bsz, MLA_HEADS, nt, HEAD_PAD, tile), BF16),
        jax.ShapeDtypeStruct((bsz, MLA_HEADS, nt, tile, HEAD_PAD), BF16),
        jax.ShapeDtypeStruct((bsz, MLA_HEADS, nt, V_ROWS, tile), BF16),
    )
    out_specs = (
        pl.BlockSpec((1, MLA_HEADS, per, HEAD_PAD, tile), lambda b, t: (b, 0, t, 0, 0)),
        pl.BlockSpec((1, MLA_HEADS, per, tile, HEAD_PAD), lambda b, t: (b, 0, t, 0, 0)),
        pl.BlockSpec((1, MLA_HEADS, per, V_ROWS, tile), lambda b, t: (b, 0, t, 0, 0)),
    )
    return pl.pallas_call(
        _mla_proj_kernel,
        grid=(bsz, tlen // step),
        in_specs=in_specs,
        out_specs=out_specs,
        out_shape=out_shape,
        compiler_params=pltpu.CompilerParams(
            dimension_semantics=("arbitrary", "arbitrary"), vmem_limit_bytes=VMEM_LIMIT),
        name="mla_proj",
    )(h, *args, *tables)


def _strided_time(i):
    return (i & (SUBLANES - 1)) * GROUPS + (i >> (SUBLANES.bit_length() - 1))


def _attn_kernel(*refs, has_meta):
    if has_meta:
        qt_ref, k_ref, vt_ref, km_ref, vtm_ref, o_ref, s_ref, smax_ref = refs
    else:
        qt_ref, k_ref, vt_ref, o_ref, s_ref, smax_ref = refs
    tile = qt_ref.shape[4]
    step = pl.program_id(2)
    neg = jnp.finfo(F32).min
    heads = range(HEADS_PER_STEP)

    def softmax_update(state, scores, vt, keep=None, block_max=None, extra=None):
        def block():
            s = scores()
            return s if keep is None else jnp.where(keep, s, neg)
        m, acc = state
        if block_max is None:
            block_max = jnp.max(block(), axis=0, keepdims=True)
        if extra is not None:
            block_max = jnp.maximum(block_max, jnp.max(extra[0], axis=0, keepdims=True))
        m_new = jnp.maximum(m, block_max)
        p = jnp.exp2(block() - m_new).astype(BF16)
        if extra is not None:
            p = jnp.concatenate([p, jnp.exp2(extra[0] - m_new).astype(BF16)], axis=0)
            vt = jnp.concatenate([vt, extra[1]], axis=1)
        return m_new, jnp.exp2(m - m_new) * acc + _dot(vt, p)

    def produce(sub, slot, j, hh, diagonal=False):
        s = _dot(k_ref[0, hh, j], qt_ref[0, hh, sub])
        if diagonal:
            s = jnp.where(keep, s, neg)
        s_ref[slot, hh] = s
        smax_ref[slot, hh] = jnp.max(s, axis=0, keepdims=True)

    def consume(sub, carry, slot, j, next_j, keep=None, with_meta=False, next_diagonal=False):
        if with_meta:
            meta = [(_dot(km_ref[hh], qt_ref[0, hh, sub]), vtm_ref[hh]) for hh in heads]
        if next_j is not None:
            for hh in range(SCORE_LEAD):
                produce(sub, 1 - slot, next_j, hh, next_diagonal)
        out = []
        for hh in heads:
            out.append(softmax_update(
                carry[hh], lambda: s_ref[slot, hh], vt_ref[0, hh, j], keep,
                smax_ref[slot, hh] if keep is None else None,
                meta[hh] if with_meta else None))
            if next_j is not None and hh + SCORE_LEAD < HEADS_PER_STEP:
                produce(sub, 1 - slot, next_j, hh + SCORE_LEAD, next_diagonal)
        return tuple(out)

    kpos = _strided_time(lax.broadcasted_iota(jnp.int32, (tile, tile), 0))
    qpos = _strided_time(lax.broadcasted_iota(jnp.int32, (tile, tile), 1))
    keep = kpos <= qpos
    subs = qt_ref.shape[2]
    for sub in range(subs):
        qi = subs * step + sub
        carry = tuple((jnp.full((1, tile), neg, F32), jnp.zeros((V_ROWS, tile), F32))
                      for _ in heads)
        for hh in heads:
            produce(sub, 0, 0, hh)

        def group(jj, c, sub=sub):
            for d in range(subs):
                c = consume(sub, c, d % 2, subs * jj + d, subs * jj + d + 1)
            return c

        carry = lax.fori_loop(0, step, group, carry)
        for d in range(sub):
            carry = consume(sub, carry, d % 2, subs * step + d, subs * step + d + 1,
                            next_diagonal=d == sub - 1)
        carry = consume(sub, carry, sub % 2, qi, None, keep if sub == 0 else None, has_meta)
        outs = [acc[0:V_HEAD, :] / acc[V_HEAD:V_HEAD + 1, :] for _, acc in carry]
        o_ref[0, 0, sub * tile:(sub + 1) * tile, :] = (
            jnp.concatenate(outs, axis=0).T.astype(o_ref.dtype))


def _attention(qt, k5, vt5, meta_kv=None):
    bsz, nh, nt, tile, _ = k5.shape
    tlen = nt * tile
    hps = HEADS_PER_STEP
    subs = ATTN_Q_TILES if nt % ATTN_Q_TILES == 0 else 1
    assert subs % 2 == 0 or nt == 1
    in_specs = [
        pl.BlockSpec((1, hps, subs, HEAD_PAD, tile), lambda b, h, q: (b, h, q, 0, 0)),
        pl.BlockSpec((1, hps, nt, tile, HEAD_PAD), lambda b, h, q: (b, h, 0, 0, 0)),
        pl.BlockSpec((1, hps, nt, V_ROWS, tile), lambda b, h, q: (b, h, 0, 0, 0)),
    ]
    args = (qt, k5, vt5)
    if meta_kv is not None:
        in_specs += [pl.BlockSpec((hps,) + a.shape[1:], lambda b, h, q: (h, 0, 0))
                     for a in meta_kv]
        args += tuple(meta_kv)
    return pl.pallas_call(
        functools.partial(_attn_kernel, has_meta=meta_kv is not None),
        grid=(bsz, nh // hps, nt // subs),
        in_specs=in_specs,
        out_specs=pl.BlockSpec((1, 1, subs * tile, hps * V_HEAD), lambda b, h, q: (b, h, q, 0)),
        out_shape=jax.ShapeDtypeStruct((bsz, nh // hps, tlen, hps * V_HEAD), BF16),
        scratch_shapes=[pltpu.VMEM((2, hps, tile, tile), F32),
                        pltpu.VMEM((2, hps, 1, tile), F32)],
        compiler_params=pltpu.CompilerParams(
            dimension_semantics=("arbitrary", "arbitrary", "arbitrary"),
            vmem_limit_bytes=VMEM_LIMIT),
        name="mla_attention",
    )(*args)


def _to_strided(a, axis):
    shp = a.shape
    nt = shp[axis] // TILE_T
    a = a.reshape(shp[:axis] + (nt, SUBLANES, GROUPS) + shp[axis + 1:])
    return jnp.swapaxes(a, axis + 1, axis + 2).reshape(shp)


def _trunk(h, first_pos, states, p, state_at):
    depth = len(p["ffn"])
    tables = _rope_tables(first_pos, h.shape[1])
    new_states = []
    for layer in range(depth):
        st = states[layer]
        ffn = p["ffn"][layer]
        attn = None
        if layer % 2 == 0:
            h, mix_state = _even_layer(h, st["mix"], *p["even"][layer // 2], state_at=state_at)
        else:
            g, q_norm, kv_norm, weights, w_out = p["odd"][layer // 2]
            qt, k5, vt5 = _mla_proj(h, tables, g, q_norm, kv_norm, weights)
            attn = (_attention(qt, k5, vt5, st["mix"]), w_out)
            mix_state = None
            if state_at is not None:
                n = state_at + 1
                assert n <= GROUPS
                mix_state = (k5[0, :, 0, 0:n * SUBLANES:SUBLANES, :],
                             vt5[0, :, 0, :, 0:n * SUBLANES:SUBLANES])
        h, tail = _ffn_layer(h, st["ffn"], *ffn, attn=attn,
                             final_g=p["final"] if layer == depth - 1 else None,
                             state_at=state_at)
        new_states.append({"mix": mix_state, "ffn": tail})
    return h, new_states


def kernel(x, meta_tokens, ev_norm, ev_w_in, ev_conv_a, ev_conv_b, ev_conv_b_bias, ev_gate_r_w, ev_gate_r_b, ev_gate_i_w, ev_gate_i_b, ev_lru_lambda, ev_w_out, od_norm, od_w_in, od_q_norm, od_kv_norm, od_w_uq, od_w_ukv, od_w_out, ffn_norm, ffn_w_up, ffn_conv_w, ffn_conv_b, ffn_w_down, final_norm):
    bsz, seq, _ = x.shape
    depth = ffn_norm.shape[0]
    assert seq % TILE_T == 0 and N_META <= GROUPS
    ev_in, ev_out = ev_w_in.astype(BF16), ev_w_out.astype(BF16)
    w_up, w_down = ffn_w_up.astype(BF16), ffn_w_down.astype(BF16)
    params = {
        "even": [(ev_norm[j], _Stacked(ev_in, j), ev_conv_a[j], ev_conv_b[j],
                  ev_conv_b_bias[j], _block_diag(ev_gate_r_w[j]).astype(BF16), ev_gate_r_b[j],
                  _block_diag(ev_gate_i_w[j]).astype(BF16), ev_gate_i_b[j], ev_lru_lambda[j],
                  _Stacked(ev_out, j)) for j in range(ev_norm.shape[0])],
        "odd": [(od_norm[j], od_q_norm[j], od_kv_norm[j],
                 _mla_weights(od_w_in[j], od_w_uq[j], od_w_ukv[j]), od_w_out[j].astype(BF16))
                for j in range(od_norm.shape[0])],
        "ffn": [(ffn_norm[l], _Stacked(w_up, l), ffn_conv_w[l], ffn_conv_b[l],
                 _Stacked(w_down, l)) for l in range(depth)],
        "final": final_norm,
    }
    zero_states = []
    for layer in range(depth):
        mix = None
        if layer % 2 == 0:
            mix = (jnp.zeros(((ev_conv_a.shape[1] - 1) * SUBLANES, CONV_WIDTH), F32),
                   jnp.zeros(((ev_conv_b.shape[1] - 1) * SUBLANES, LRU_WIDTH), F32),
                   jnp.zeros((1, LRU_WIDTH), F32))
        zero_states.append(
            {"mix": mix, "ffn": jnp.zeros(((ffn_conv_w.shape[1] - 1) * SUBLANES, 2 * D_FF), F32)})

    meta_tile = jnp.concatenate(
        [meta_tokens.astype(x.dtype), jnp.zeros((TILE_T - N_META, D_MODEL), x.dtype)], axis=0)
    _, meta_states = _trunk(_to_strided(meta_tile[None], 1), 0.0, zero_states, params,
                            state_at=N_META - 1)
    h, _ = _trunk(_to_strided(x, 1), float(N_META), meta_states, params, state_at=None)
    return h
```
